```python
import jax, jax.numpy as jnp
from jax import lax
import numpy as np

D_MODEL = 1024
BATCH = 2
SEQ = 8192
DEPTH = 1

GRID_W = 64
CTX_LEN = 256
MLA_HEADS = 8
MLA_NOPE = 64
MLA_ROPE = 32
MLA_QK = MLA_NOPE + MLA_ROPE
MLA_V = 64
Q_LORA = 384
KV_LORA = 256
NA_HEADS = 8
NA_DIM = 64
NA_WIN_H = 8
NA_WIN_W = 16
MIX_WIDTH = MLA_HEADS * MLA_V + NA_HEADS * NA_DIM
Q_COLS = Q_LORA + NA_HEADS * NA_DIM
KV_COLS = KV_LORA + MLA_ROPE + 2 * NA_HEADS * NA_DIM
N_GROUPS = 4
EXPERTS_PER_GROUP = 4
N_EXPERTS = N_GROUPS * EXPERTS_PER_GROUP
TOP_K_IN_GROUP = 2
D_EXPERT = 512
Q_BLOCK = 128
ROPE_THETA = 10000.0
NORM_EPS = 1e-6
MASK_VALUE = -1e30

kernel_name = 'hybrid_mla_natten_hmoe_dit'


def rmsnorm(x, g):
    xf = x.astype(jnp.float32)
    y = xf * lax.rsqrt(jnp.mean(xf * xf, axis=-1, keepdims=True) + NORM_EPS)
    return (y * g.astype(jnp.float32)).astype(x.dtype)


def adaln(cond, w_mod, b_mod):
    return jnp.split(jax.nn.silu(cond) @ w_mod + b_mod, 6, axis=-1)


def modulate(h, shift, scale):
    return h * (1 + scale) + shift


def axial_rope_tables(row, col, dtype):
    half = MLA_ROPE // 2
    inv_freq = ROPE_THETA ** (-jnp.arange(0, half, 2, dtype=jnp.float32) / half)
    ang = jnp.concatenate([row.astype(jnp.float32)[:, None] * inv_freq,
                           col.astype(jnp.float32)[:, None] * inv_freq], axis=-1)
    return jnp.cos(ang).astype(dtype), jnp.sin(ang).astype(dtype)


def apply_rope(x, cos, sin):
    x1, x2 = jnp.split(x, 2, axis=-1)
    c, s = cos[:, None, :], sin[:, None, :]
    return jnp.concatenate([x1 * c - x2 * s, x1 * s + x2 * c], axis=-1)


def mixer_queries(pq, q_a_g, w_uq):
    B, L, _ = pq.shape
    q_lat, na_q = jnp.split(pq, [Q_LORA], axis=-1)
    q_mla = (rmsnorm(q_lat, q_a_g) @ w_uq).reshape(B, L, MLA_HEADS, MLA_QK)
    return q_mla, na_q.reshape(B, L, NA_HEADS, NA_DIM)


def mixer_keys_values(pkv, kv_a_g, w_ukv):
    B, L, _ = pkv.shape
    kv_lat, k_rope, na_k, na_v = jnp.split(
        pkv, [KV_LORA, KV_LORA + MLA_ROPE, KV_LORA + MLA_ROPE + NA_HEADS * NA_DIM], axis=-1)
    kv = (rmsnorm(kv_lat, kv_a_g) @ w_ukv).reshape(B, L, MLA_HEADS, MLA_NOPE + MLA_V)
    k_nope, v_mla = jnp.split(kv, [MLA_NOPE], axis=-1)
    return (k_nope, k_rope, v_mla,
            na_k.reshape(B, L, NA_HEADS, NA_DIM), na_v.reshape(B, L, NA_HEADS, NA_DIM))


def mla_keys(k_nope, k_rope_b):
    return jnp.concatenate([k_nope, jnp.broadcast_to(k_rope_b, k_nope.shape[:3] + (MLA_ROPE,))], axis=-1)


def dense_attention(q, k, v):
    s = jnp.einsum('bqhd,bkhd->bhqk', q, k).astype(jnp.float32) * (q.shape[-1] ** -0.5)
    p = jax.nn.softmax(s, axis=-1).astype(v.dtype)
    return jnp.einsum('bhqk,bkhd->bqhd', p, v)


def mla_latent_attention(q, k, v, k_ctx, v_ctx):
    B, N, H, Dq = q.shape
    k_all = jnp.concatenate([k_ctx, k], axis=1)
    v_all = jnp.concatenate([v_ctx, v], axis=1)
    q_blocks = jnp.moveaxis(q.reshape(B, N // Q_BLOCK, Q_BLOCK, H, Dq), 1, 0)
    out = lax.map(lambda qb: dense_attention(qb, k_all, v_all), q_blocks)
    return jnp.moveaxis(out, 0, 1).reshape(B, N, H, v.shape[-1])


def neighbourhood_attention(q, k, v, k_ctx, v_ctx, rel_bias):
    B, N, H, Dh = q.shape
    rows = N // GRID_W
    kh = min(NA_WIN_H, rows)
    kw = NA_WIN_W
    scale = Dh ** -0.5
    qg = jnp.moveaxis(q.reshape(B, rows, GRID_W, H, Dh), 1, 0)
    kg = k.reshape(B, rows, GRID_W, H, Dh)
    vg = v.reshape(B, rows, GRID_W, H, Dh)
    col = jnp.arange(GRID_W)
    col_start = jnp.clip(col - kw // 2, 0, GRID_W - kw)
    col_in = (col[None, :] >= col_start[:, None]) & (col[None, :] < col_start[:, None] + kw)
    bias_cols = rel_bias[:, :, col[None, :] - col[:, None] + (NA_WIN_W - 1)]

    def one_row(args):
        r, qr = args
        start = jnp.clip(r - kh // 2, 0, rows - kh)
        kr = lax.dynamic_slice_in_dim(kg, start, kh, axis=1)
        vr = lax.dynamic_slice_in_dim(vg, start, kh, axis=1)
        bias = bias_cols[:, start + jnp.arange(kh) - r + (NA_WIN_H - 1)]
        s = jnp.einsum('bqhd,bjkhd->bhqjk', qr, kr).astype(jnp.float32) * scale
        s = s + jnp.transpose(bias, (0, 2, 1, 3)).astype(jnp.float32)[None]
        s = jnp.where(col_in[None, None, :, None, :], s, MASK_VALUE).reshape(B, H, GRID_W, kh * GRID_W)
        s_ctx = jnp.einsum('bqhd,bchd->bhqc', qr, k_ctx).astype(jnp.float32) * scale
        p = jax.nn.softmax(jnp.concatenate([s, s_ctx], axis=-1), axis=-1).astype(v.dtype)
        p_win = p[..., :kh * GRID_W].reshape(B, H, GRID_W, kh, GRID_W)
        p_ctx = p[..., kh * GRID_W:]
        return (jnp.einsum('bhqjk,bjkhd->bqhd', p_win, vr)
                + jnp.einsum('bhqc,bchd->bqhd', p_ctx, v_ctx))

    out = lax.map(one_row, (jnp.arange(rows), qg))
    return jnp.moveaxis(out, 0, 1).reshape(B, N, H, Dh)


def hierarchical_moe(h, w_rg, b_rg, w_re, b_re, w_gate, w_up, w_down):
    B, L, D = h.shape
    t = h.reshape(B * L, D)
    T = t.shape[0]
    g_logits = (t @ w_rg + b_rg).astype(jnp.float32)
    g_prob = jax.nn.softmax(g_logits, axis=-1)
    g_sel = jnp.argmax(g_logits, axis=-1)
    g_w = jnp.take_along_axis(g_prob, g_sel[:, None], axis=-1)
    e_logits = (t @ w_re + b_re).astype(jnp.float32).reshape(T, N_GROUPS, EXPERTS_PER_GROUP)
    e_logits = jnp.take_along_axis(e_logits, g_sel[:, None, None], axis=1)[:, 0]
    top_p, top_i = lax.top_k(jax.nn.softmax(e_logits, axis=-1), TOP_K_IN_GROUP)
    top_p = top_p / jnp.sum(top_p, axis=-1, keepdims=True)
    within = jnp.sum(jax.nn.one_hot(top_i, EXPERTS_PER_GROUP, dtype=jnp.float32) * top_p[..., None], axis=1)
    combine = (jax.nn.one_hot(g_sel, N_GROUPS, dtype=jnp.float32)[:, :, None]
               * (g_w * within)[:, None, :]).astype(h.dtype)
    wg = w_gate.reshape(N_GROUPS, EXPERTS_PER_GROUP, D, D_EXPERT)
    wu = w_up.reshape(N_GROUPS, EXPERTS_PER_GROUP, D, D_EXPERT)
    wd = w_down.reshape(N_GROUPS, EXPERTS_PER_GROUP, D_EXPERT, D)

    def one_group(args):
        wg_g, wu_g, wd_g, comb_g = args
        hid = jax.nn.silu(jnp.einsum('td,edf->tef', t, wg_g)) * jnp.einsum('td,edf->tef', t, wu_g)
        return jnp.einsum('tef,efd->td', hid * comb_g[:, :, None], wd_g)

    y = jnp.sum(lax.map(one_group, (wg, wu, wd, jnp.moveaxis(combine, 1, 0))), axis=0)
    return y.reshape(B, L, D)


def setup_inputs(seed: int = 0) -> dict:
    key = jax.random.key(seed)
    ks = jax.random.split(key, 24)
    D = D_MODEL

    def nrm(k, shape, scale):
        return jax.random.normal(k, shape, jnp.float32) * scale

    return {
        'x': nrm(ks[0], (BATCH, SEQ, D), 1.0),
        'c': nrm(ks[1], (BATCH, D), 1.0),
        'ctx': nrm(ks[2], (BATCH, CTX_LEN, D), 1.0),
        'c_ctx': nrm(ks[3], (D,), 1.0),
        'w_mod': nrm(ks[4], (DEPTH, D, 6 * D), 0.5 * D ** -0.5),
        'b_mod': nrm(ks[5], (DEPTH, 6 * D), 0.02),
        'norm_attn_g': 1.0 + nrm(ks[6], (DEPTH, D), 0.02),
        'norm_ffn_g': 1.0 + nrm(ks[7], (DEPTH, D), 0.02),
        'w_in': nrm(ks[8], (DEPTH, D, Q_COLS + KV_COLS), D ** -0.5),
        'q_a_norm_g': 1.0 + nrm(ks[9], (DEPTH, Q_LORA), 0.02),
        'kv_a_norm_g': 1.0 + nrm(ks[10], (DEPTH, KV_LORA), 0.02),
        'w_uq': nrm(ks[11], (DEPTH, Q_LORA, MLA_HEADS * MLA_QK), Q_LORA ** -0.5),
        'w_ukv': nrm(ks[12], (DEPTH, KV_LORA, MLA_HEADS * (MLA_NOPE + MLA_V)), KV_LORA ** -0.5),
        'na_rel_bias': nrm(ks[13], (DEPTH, NA_HEADS, 2 * NA_WIN_H - 1, 2 * NA_WIN_W - 1), 0.5),
        'w_out': nrm(ks[14], (DEPTH, MIX_WIDTH, D), MIX_WIDTH ** -0.5),
        'w_router_group': nrm(ks[15], (DEPTH, D, N_GROUPS), D ** -0.5),
        'b_router_group': nrm(ks[16], (DEPTH, N_GROUPS), 0.01),
        'w_router_expert': nrm(ks[17], (DEPTH, D, N_EXPERTS), D ** -0.5),
        'b_router_expert': nrm(ks[18], (DEPTH, N_EXPERTS), 0.01),
        'w_gate': nrm(ks[19], (DEPTH, N_EXPERTS, D, D_EXPERT), D ** -0.5),
        'w_up': nrm(ks[20], (DEPTH, N_EXPERTS, D, D_EXPERT), D ** -0.5),
        'w_down': nrm(ks[21], (DEPTH, N_EXPERTS, D_EXPERT, D), D_EXPERT ** -0.5),
        'final_norm_g': 1.0 + nrm(ks[22], (D,), 0.02),
    }


def reference(x, c, ctx, c_ctx, w_mod, b_mod, norm_attn_g, norm_ffn_g, w_in, q_a_norm_g, kv_a_norm_g,
              w_uq, w_ukv, na_rel_bias, w_out, w_router_group, b_router_group, w_router_expert,
              b_router_expert, w_gate, w_up, w_down, final_norm_g):
    B, N, D = x.shape
    pos = jnp.arange(N)
    cos, sin = axial_rope_tables(pos // GRID_W, pos % GRID_W, x.dtype)
    for l in range(DEPTH):
        sh1, sc1, g1, sh2, sc2, g2 = [m[:, None, :] for m in adaln(c, w_mod[l], b_mod[l])]
        csh1, csc1, cg1, csh2, csc2, cg2 = adaln(c_ctx, w_mod[l], b_mod[l])
        hx = modulate(rmsnorm(x, norm_attn_g[l]), sh1, sc1)
        hc = modulate(rmsnorm(ctx, norm_attn_g[l]), csh1, csc1)

        kn_c, kr_c, v_c, nk_c, nv_c = mixer_keys_values(hc @ w_in[l][:, Q_COLS:], kv_a_norm_g[l], w_ukv[l])
        k_c = mla_keys(kn_c, kr_c[:, :, None, :])

        proj = hx @ w_in[l]
        q_m, nq = mixer_queries(proj[..., :Q_COLS], q_a_norm_g[l], w_uq[l])
        kn, kr, v_m, nk, nv = mixer_keys_values(proj[..., Q_COLS:], kv_a_norm_g[l], w_ukv[l])
        q_m = jnp.concatenate([q_m[..., :MLA_NOPE], apply_rope(q_m[..., MLA_NOPE:], cos, sin)], axis=-1)
        k_m = mla_keys(kn, apply_rope(kr[:, :, None, :], cos, sin))
        o_mla = mla_latent_attention(q_m, k_m, v_m, k_c, v_c)
        o_na = neighbourhood_attention(nq, nk, nv, nk_c, nv_c, na_rel_bias[l])
        mixed = jnp.concatenate([o_mla.reshape(B, N, -1), o_na.reshape(B, N, -1)], axis=-1) @ w_out[l]

        if l < DEPTH - 1:
            Bc, C = ctx.shape[:2]
            q_c, nq_c = mixer_queries(hc @ w_in[l][:, :Q_COLS], q_a_norm_g[l], w_uq[l])
            mixed_c = jnp.concatenate([dense_attention(q_c, k_c, v_c).reshape(Bc, C, -1),
                                       dense_attention(nq_c, nk_c, nv_c).reshape(Bc, C, -1)], axis=-1) @ w_out[l]
            ctx = ctx + cg1 * mixed_c
            hc2 = modulate(rmsnorm(ctx, norm_ffn_g[l]), csh2, csc2)
            ctx = ctx + cg2 * hierarchical_moe(hc2, w_router_group[l], b_router_group[l], w_router_expert[l],
                                               b_router_expert[l], w_gate[l], w_up[l], w_down[l])

        x = x + g1 * mixed
        hx2 = modulate(rmsnorm(x, norm_ffn_g[l]), sh2, sc2)
        x = x + g2 * hierarchical_moe(hx2, w_router_group[l], b_router_group[l], w_router_expert[l],
                                      b_router_expert[l], w_gate[l], w_up[l], w_down[l])
    return rmsnorm(x, final_norm_g)
```

```python
import functools

import jax
import jax.numpy as jnp
from jax import lax
from jax.experimental import pallas as pl
from jax.experimental.pallas import tpu as pltpu

D_MODEL = 1024
GRID_W = 64
CTX_LEN = 256
MLA_HEADS = 8
MLA_NOPE = 64
MLA_ROPE = 32
MLA_QK = MLA_NOPE + MLA_ROPE
MLA_V = 64
Q_LORA = 384
KV_LORA = 256
NA_HEADS = 8
NA_DIM = 64
NA_WIN_H = 8
NA_WIN_W = 16
N_GROUPS = 4
EXPERTS_PER_GROUP = 4
N_EXPERTS = N_GROUPS * EXPERTS_PER_GROUP
D_EXPERT = 512
ROPE_THETA = 10000.0
NORM_EPS = 1e-6
MASK_VALUE = -1e30

LANES = 128
HEAD_PAIRS = 4
TOK_BLOCK = 256
ROWS_PER_BLOCK = TOK_BLOCK // GRID_W
VMEM_LIMIT = 48 * 1024 * 1024

F32 = jnp.float32
BF16 = jnp.bfloat16


def _rms(x, g):
    return x * lax.rsqrt(jnp.mean(x * x, axis=-1, keepdims=True) + NORM_EPS) * g


def _adaln_kernel(c_ref, w_ref, b_ref, o_ref):
    c = c_ref[...]
    s = c * jax.nn.sigmoid(c)
    o_ref[...] = jnp.dot(s, w_ref[...], precision=lax.Precision.HIGHEST,
                         preferred_element_type=F32) + b_ref[...]


def _adaln(cond8, w_mod, b_mod):
    n_out = w_mod.shape[1]
    bn = 1024
    return pl.pallas_call(
        _adaln_kernel,
        grid=(n_out // bn,),
        in_specs=[pl.BlockSpec((8, D_MODEL), lambda n: (0, 0)),
                  pl.BlockSpec((D_MODEL, bn), lambda n: (0, n)),
                  pl.BlockSpec((1, bn), lambda n: (0, n))],
        out_specs=pl.BlockSpec((8, bn), lambda n: (0, n)),
        out_shape=jax.ShapeDtypeStruct((8, n_out), F32),
        name="adaln",
    )(cond8, w_mod, b_mod.reshape(1, n_out))


NA_VARIANTS = ((-1, False), (0, False), (1, False), (0, True), (1, True), (-1, True), (0, None))
V_PREV, V_CUR, V_NEXT, V_CUR_ALL, V_NEXT_ALL, V_PREV_ALL, V_NONE = range(7)
N_DR = 2 * NA_WIN_H - 1
N_DC = 2 * NA_WIN_W - 1


def _na_bias_kernel(rel_ref, o_ref):
    h = pl.program_id(0)
    qc = lax.broadcasted_iota(jnp.int32, (GRID_W, TOK_BLOCK), 0)
    kk = lax.broadcasted_iota(jnp.int32, (GRID_W, TOK_BLOCK), 1)
    kc = kk % GRID_W
    kblk = kk // GRID_W
    col_start = jnp.clip(qc - NA_WIN_W // 2, 0, GRID_W - NA_WIN_W)
    col_in = (kc >= col_start) & (kc < col_start + NA_WIN_W)
    dc = kc - qc + (NA_WIN_W - 1)
    tables = []
    for d in range(N_DR):
        t = jnp.full((GRID_W, TOK_BLOCK), MASK_VALUE, F32)
        base = (h * N_DR + d) * N_DC
        for m in range(N_DC):
            t = jnp.where(dc == m, rel_ref[base + m], t)
        tables.append(jnp.where(col_in, t, MASK_VALUE))
    for v, (boff, all_valid) in enumerate(NA_VARIANTS):
        for a in range(ROWS_PER_BLOCK):
            tile = jnp.full((GRID_W, TOK_BLOCK), MASK_VALUE, F32)
            if all_valid is not None:
                for cb in range(ROWS_PER_BLOCK):
                    dr = ROWS_PER_BLOCK * boff + cb - a
                    if all_valid or (-(NA_WIN_H // 2) <= dr < NA_WIN_H - NA_WIN_H // 2):
                        tile = jnp.where(kblk == cb, tables[dr + NA_WIN_H - 1], tile)
            o_ref[v, 0, a * GRID_W:(a + 1) * GRID_W, :] = tile


def _na_bias(rel_bias):
    return pl.pallas_call(
        _na_bias_kernel,
        grid=(NA_HEADS,),
        in_specs=[pl.BlockSpec(memory_space=pltpu.SMEM)],
        out_specs=pl.BlockSpec((len(NA_VARIANTS), 1, TOK_BLOCK, TOK_BLOCK), lambda h: (0, h, 0, 0)),
        out_shape=jax.ShapeDtypeStruct((len(NA_VARIANTS), NA_HEADS, TOK_BLOCK, TOK_BLOCK), F32),
        name="na_bias",
    )(rel_bias.reshape(-1))


C_QLAT = 0
C_NAQ = Q_LORA
C_KVLAT = C_NAQ + NA_HEADS * NA_DIM
C_KROPE = C_KVLAT + KV_LORA
C_NAK = C_KROPE + LANES
C_NAV = C_NAK + NA_HEADS * NA_DIM
C_END = C_NAV + NA_HEADS * NA_DIM
MLA_SCALE = MLA_QK ** -0.5
NA_SCALE = NA_DIM ** -0.5


def _rope(x, c, sa, sb):
    return x * c + pltpu.roll(x, LANES - MLA_ROPE // 2, 1) * sa + pltpu.roll(x, MLA_ROPE // 2, 1) * sb


def _proj_kernel(x_ref, ctx_ref, mod_ref, g_ref, win_ref, qg_ref, kvg_ref, wuq_ref, wukv_ref,
                 c_ref, sa_ref, sb_ref,
                 q_out, k_out, v_out, nq_out, nk_out, nv_out, h_scr):
    j = pl.program_id(1)
    shift = mod_ref[0, 0:1, :]
    scale = mod_ref[0, 1:2, :]

    def norm_mod(xt):
        h_scr[...] = (_rms(xt, g_ref[...]) * (1.0 + scale) + shift).astype(BF16)

    @pl.when(j == 0)
    def _():
        norm_mod(ctx_ref[0])

    @pl.when(j > 0)
    def _():
        norm_mod(x_ref[0])

    proj = jnp.dot(h_scr[...], win_ref[...], preferred_element_type=F32)
    c, sa, sb = c_ref[...], sa_ref[...], sb_ref[...]

    qn = _rms(proj[:, C_QLAT:C_NAQ], qg_ref[...]).astype(BF16)
    q = jnp.dot(qn, wuq_ref[...], preferred_element_type=F32) * MLA_SCALE
    for h in range(MLA_HEADS):
        q_out[0, h] = _rope(q[:, h * LANES:(h + 1) * LANES], c, sa, sb).astype(BF16)

    kvn = _rms(proj[:, C_KVLAT:C_KROPE], kvg_ref[...]).astype(BF16)
    kv = jnp.dot(kvn, wukv_ref[...], preferred_element_type=F32)
    kr = _rope(proj[:, C_KROPE:C_NAK], c, sa, sb)
    for h in range(MLA_HEADS):
        k_out[0, h] = (kv[:, h * LANES:(h + 1) * LANES] + kr).astype(BF16)
    v0 = MLA_HEADS * LANES
    for p in range(HEAD_PAIRS):
        v_out[0, p] = kv[:, v0 + p * LANES:v0 + (p + 1) * LANES].astype(BF16)
        nq_out[0, p] = (proj[:, C_NAQ + p * LANES:C_NAQ + (p + 1) * LANES] * NA_SCALE).astype(BF16)
        nk_out[0, p] = proj[:, C_NAK + p * LANES:C_NAK + (p + 1) * LANES].astype(BF16)
        nv_out[0, p] = proj[:, C_NAV + p * LANES:C_NAV + (p + 1) * LANES].astype(BF16)


def _proj(x, ctx, mod, g, win, qg, kvg, wuq, wukv, rc, rsa, rsb):
    B, N, D = x.shape
    nb = N // TOK_BLOCK
    ntot = N + CTX_LEN
    full = lambda shape: pl.BlockSpec(shape, lambda b, j: (0,) * len(shape))
    lat = lambda b, j: (b, 0, jnp.maximum(j - 1, 0), 0)
    alltok = lambda b, j: (b, 0, j, 0)
    return pl.pallas_call(
        _proj_kernel,
        grid=(B, nb + 1),
        in_specs=[pl.BlockSpec((1, TOK_BLOCK, D), lambda b, j: (b, jnp.maximum(j - 1, 0), 0)),
                  pl.BlockSpec((1, CTX_LEN, D), lambda b, j: (b, 0, 0)),
                  pl.BlockSpec((1, 6, D), lambda b, j: (jnp.where(j == 0, B, b), 0, 0)),
                  full((1, D)), full(win.shape), full((1, Q_LORA)), full((1, KV_LORA)),
                  full(wuq.shape), full(wukv.shape),
                  pl.BlockSpec((TOK_BLOCK, LANES), lambda b, j: (j, 0)),
                  pl.BlockSpec((TOK_BLOCK, LANES), lambda b, j: (j, 0)),
                  pl.BlockSpec((TOK_BLOCK, LANES), lambda b, j: (j, 0))],
        out_specs=[pl.BlockSpec((1, MLA_HEADS, TOK_BLOCK, LANES), lat),
                   pl.BlockSpec((1, MLA_HEADS, TOK_BLOCK, LANES), alltok),
                   pl.BlockSpec((1, HEAD_PAIRS, TOK_BLOCK, LANES), alltok),
                   pl.BlockSpec((1, HEAD_PAIRS, TOK_BLOCK, LANES), lat),
                   pl.BlockSpec((1, HEAD_PAIRS, TOK_BLOCK, LANES), alltok),
                   pl.BlockSpec((1, HEAD_PAIRS, TOK_BLOCK, LANES), alltok)],
        out_shape=[jax.ShapeDtypeStruct((B, MLA_HEADS, N, LANES), BF16),
                   jax.ShapeDtypeStruct((B, MLA_HEADS, ntot, LANES), BF16),
                   jax.ShapeDtypeStruct((B, HEAD_PAIRS, ntot, LANES), BF16),
                   jax.ShapeDtypeStruct((B, HEAD_PAIRS, N, LANES), BF16),
                   jax.ShapeDtypeStruct((B, HEAD_PAIRS, ntot, LANES), BF16),
                   jax.ShapeDtypeStruct((B, HEAD_PAIRS, ntot, LANES), BF16)],
        scratch_shapes=[pltpu.VMEM((TOK_BLOCK, D), BF16)],
        compiler_params=pltpu.CompilerParams(vmem_limit_bytes=VMEM_LIMIT),
        name="proj",
    )(x, ctx, mod, g, win, qg, kvg, wuq, wukv, rc, rsa, rsb)


MLA_TQ = 256
MLA_TK = 768


def _mla_kernel(q_ref, k_ref, v_ref, o_ref, m_scr, l_scr, acc_scr):
    n_chunks = k_ref.shape[2] // MLA_TK
    outs = []
    for hh in range(2):
        q = q_ref[0, hh]
        m_scr[...] = jnp.full(m_scr.shape, -jnp.inf, F32)
        l_scr[...] = jnp.zeros(l_scr.shape, F32)
        acc_scr[...] = jnp.zeros(acc_scr.shape, F32)

        def body(ci, carry, hh=hh, q=q):
            off = pl.multiple_of(ci * MLA_TK, MLA_TK)
            k = k_ref[0, hh, pl.ds(off, MLA_TK), :]
            v = v_ref[0, 0, pl.ds(off, MLA_TK), :]
            s = lax.dot_general(q, k, (((1,), (1,)), ((), ())), preferred_element_type=F32)
            m_prev = m_scr[...]
            m_next = jnp.maximum(m_prev, jnp.max(s, axis=1, keepdims=True))
            p = jnp.exp(s - pltpu.repeat(m_next, MLA_TK // LANES, axis=1))
            alpha = jnp.exp(m_prev - m_next)
            l_scr[...] = alpha * l_scr[...] + jnp.sum(p, axis=1, keepdims=True)
            acc_scr[...] = alpha * acc_scr[...] + jnp.dot(p.astype(BF16), v, preferred_element_type=F32)
            m_scr[...] = m_next
            return carry

        lax.fori_loop(0, n_chunks, body, 0)
        outs.append(acc_scr[...] / l_scr[...])
    lane = lax.broadcasted_iota(jnp.int32, outs[0].shape, 1)
    o_ref[0] = jnp.where(lane < MLA_V, outs[0], outs[1]).astype(o_ref.dtype)


def _mla(q_m, k_m, v_m):
    B, _, N, _ = q_m.shape
    ntot = k_m.shape[2]
    return pl.pallas_call(
        _mla_kernel,
        grid=(B, HEAD_PAIRS, N // MLA_TQ),
        in_specs=[pl.BlockSpec((1, 2, MLA_TQ, LANES), lambda b, p, i: (b, p, i, 0)),
                  pl.BlockSpec((1, 2, ntot, LANES), lambda b, p, i: (b, p, 0, 0)),
                  pl.BlockSpec((1, 1, ntot, LANES), lambda b, p, i: (b, p, 0, 0))],
        out_specs=pl.BlockSpec((1, MLA_TQ, LANES), lambda b, p, i: (b, i, p)),
        out_shape=jax.ShapeDtypeStruct((B, N, MLA_HEADS * MLA_V), BF16),
        scratch_shapes=[pltpu.VMEM((MLA_TQ, LANES), F32)] * 3,
        compiler_params=pltpu.CompilerParams(vmem_limit_bytes=VMEM_LIMIT),
        name="mla",
    )(q_m, k_m, v_m)


def _na_kernel(q_ref, kc_ref, kp_ref, k0_ref, kn_ref, vc_ref, vp_ref, v0_ref, vn_ref,
               bp_ref, b0_ref, bn_ref, o_ref):
    lane = lax.broadcasted_iota(jnp.int32, (TOK_BLOCK, LANES), 1)
    dn = (((1,), (1,)), ((), ()))
    for p in range(HEAD_PAIRS):
        outs = []
        for half in range(2):
            h = 2 * p + half
            in_half = (lane >= NA_DIM) if half else (lane < NA_DIM)
            q = jnp.where(in_half, q_ref[0, p], jnp.zeros((), BF16))
            s = [lax.dot_general(q, kc_ref[0, p], dn, preferred_element_type=F32),
                 lax.dot_general(q, kp_ref[0, p], dn, preferred_element_type=F32) + bp_ref[0, h],
                 lax.dot_general(q, k0_ref[0, p], dn, preferred_element_type=F32) + b0_ref[0, h],
                 lax.dot_general(q, kn_ref[0, p], dn, preferred_element_type=F32) + bn_ref[0, h]]
            m = jnp.max(jnp.maximum(jnp.maximum(s[0], s[1]), jnp.maximum(s[2], s[3])), axis=1, keepdims=True)
            e = [jnp.exp(t - m) for t in s]
            l = jnp.sum(e[0] + e[1] + e[2] + e[3], axis=1, keepdims=True)
            o = jnp.dot(e[0].astype(BF16), vc_ref[0, p], preferred_element_type=F32)
            for t, v_ref in zip(e[1:], (vp_ref, v0_ref, vn_ref)):
                o = o + jnp.dot(t.astype(BF16), v_ref[0, p], preferred_element_type=F32)
            outs.append(o / l)
        o_ref[0, :, p * LANES:(p + 1) * LANES] = jnp.where(lane < NA_DIM, outs[0], outs[1]).astype(o_ref.dtype)


def _natten(nq, nk, nv, bias):
    B, _, N, _ = nq.shape
    nb = N // TOK_BLOCK
    last = nb - 1
    blk = (1, HEAD_PAIRS, TOK_BLOCK, LANES)
    ctx_map = lambda b, i: (b, 0, 0, 0)
    prev_map = lambda b, i: (b, 0, jnp.maximum(i - 1, 0) + 1, 0)
    cur_map = lambda b, i: (b, 0, i + 1, 0)
    next_map = lambda b, i: (b, 0, jnp.minimum(i + 1, last) + 1, 0)
    bblk = (1, NA_HEADS, TOK_BLOCK, TOK_BLOCK)
    bprev = lambda b, i: (jnp.where(i == 0, V_NONE, jnp.where(i == last, V_PREV_ALL, V_PREV)), 0, 0, 0)
    bcur = lambda b, i: (jnp.where((i == 0) | (i == last), V_CUR_ALL, V_CUR), 0, 0, 0)
    bnext = lambda b, i: (jnp.where(i == 0, V_NEXT_ALL, jnp.where(i == last, V_NONE, V_NEXT)), 0, 0, 0)
    return pl.pallas_call(
        _na_kernel,
        grid=(B, nb),
        in_specs=[pl.BlockSpec(blk, lambda b, i: (b, 0, i, 0)),
                  pl.BlockSpec(blk, ctx_map), pl.BlockSpec(blk, prev_map),
                  pl.BlockSpec(blk, cur_map), pl.BlockSpec(blk, next_map),
                  pl.BlockSpec(blk, ctx_map), pl.BlockSpec(blk, prev_map),
                  pl.BlockSpec(blk, cur_map), pl.BlockSpec(blk, next_map),
                  pl.BlockSpec(bblk, bprev), pl.BlockSpec(bblk, bcur), pl.BlockSpec(bblk, bnext)],
        out_specs=pl.BlockSpec((1, TOK_BLOCK, NA_HEADS * NA_DIM), lambda b, i: (b, i, 0)),
        out_shape=jax.ShapeDtypeStruct((B, N, NA_HEADS * NA_DIM), BF16),
        compiler_params=pltpu.CompilerParams(vmem_limit_bytes=VMEM_LIMIT),
        name="natten",
    )(nq, nk, nk, nk, nk, nv, nv, nv, nv, bias, bias, bias)


OUT_TM = 512
R_G1, R_SH2, R_SC2, R_G2 = 2, 3, 4, 5


def _first_max_lane(x, valid, lane):
    xm = jnp.where(valid, x, -jnp.inf)
    mx = jnp.max(xm, axis=1, keepdims=True)
    idx = jnp.min(jnp.where(valid & (xm == mx), lane, LANES), axis=1, keepdims=True)
    return mx, idx


def _outproj_kernel(om_ref, on_ref, x_ref, mod_ref, wo_ref, g_ref, wr_ref, br_ref,
                    x1_out, h2_out, comb_out):
    half = wo_ref.shape[0] // 2
    mixed = (jnp.dot(om_ref[...], wo_ref[:half, :], preferred_element_type=F32)
             + jnp.dot(on_ref[...], wo_ref[half:, :], preferred_element_type=F32))
    x1 = x_ref[...] + mod_ref[0, R_G1:R_G1 + 1, :] * mixed
    x1_out[...] = x1
    h2 = _rms(x1, g_ref[...]) * (1.0 + mod_ref[0, R_SC2:R_SC2 + 1, :]) + mod_ref[0, R_SH2:R_SH2 + 1, :]
    h2_out[...] = h2.astype(BF16)

    logits = jnp.dot(h2, wr_ref[...], precision=lax.Precision.HIGHEST,
                     preferred_element_type=F32) + br_ref[...]
    lane = lax.broadcasted_iota(jnp.int32, logits.shape, 1)
    is_g = lane < N_GROUPS
    g_max, g_sel = _first_max_lane(logits, is_g, lane)
    g_w = 1.0 / jnp.sum(jnp.where(is_g, jnp.exp(logits - g_max), 0.0), axis=1, keepdims=True)
    e_lo = N_GROUPS + g_sel * EXPERTS_PER_GROUP
    in_grp = (lane >= e_lo) & (lane < e_lo + EXPERTS_PER_GROUP)
    m1, i1 = _first_max_lane(logits, in_grp, lane)
    m2, i2 = _first_max_lane(logits, in_grp & (lane != i1), lane)
    e2 = jnp.exp(m2 - m1)
    w1 = 1.0 / (1.0 + e2)
    w2 = e2 / (1.0 + e2)
    comb = jnp.where(lane == i1, g_w * w1, jnp.where(lane == i2, g_w * w2, 0.0))
    comb_out[...] = pltpu.roll(comb, LANES - N_GROUPS, 1)


def _outproj(o_mla, o_na, x2d, mod, wo, g, wr, br, n_per_batch):
    T, D = x2d.shape
    per_b = n_per_batch // OUT_TM
    tok = lambda w: pl.BlockSpec((OUT_TM, w), lambda t: (t, 0))
    full = lambda shape: pl.BlockSpec(shape, lambda t: (0,) * len(shape))
    return pl.pallas_call(
        _outproj_kernel,
        grid=(T // OUT_TM,),
        in_specs=[tok(o_mla.shape[1]), tok(o_na.shape[1]), tok(D),
                  pl.BlockSpec((1, 6, D), lambda t: (t // per_b, 0, 0)),
                  full(wo.shape), full((1, D)), full(wr.shape), full(br.shape)],
        out_specs=[tok(D), tok(D), tok(LANES)],
        out_shape=[jax.ShapeDtypeStruct((T, D), F32), jax.ShapeDtypeStruct((T, D), BF16),
                   jax.ShapeDtypeStruct((T, LANES), F32)],
        compiler_params=pltpu.CompilerParams(vmem_limit_bytes=VMEM_LIMIT),
        name="outproj",
    )(o_mla, o_na, x2d, mod, wo, g, wr, br)


MOE_TM = 1024


def _moe_kernel(h_ref, comb_ref, x1_ref, mod_ref, wg_ref, wu_ref, wd_ref, fg_ref, o_ref, acc_scr):
    e = pl.program_id(1)

    @pl.when(e == 0)
    def _():
        acc_scr[...] = jnp.zeros(acc_scr.shape, F32)

    h = h_ref[...]
    gate = jnp.dot(h, wg_ref[0], preferred_element_type=F32)
    up = jnp.dot(h, wu_ref[0], preferred_element_type=F32)
    lane = lax.broadcasted_iota(jnp.int32, comb_ref.shape, 1)
    w = jnp.sum(jnp.where(lane == e, comb_ref[...], 0.0), axis=1, keepdims=True)
    hid = (gate * jax.nn.sigmoid(gate) * up * w).astype(BF16)
    acc_scr[...] += jnp.dot(hid, wd_ref[0], preferred_element_type=F32)

    @pl.when(e == N_EXPERTS - 1)
    def _():
        y = x1_ref[...] + mod_ref[0, R_G2:R_G2 + 1, :] * acc_scr[...]
        o_ref[...] = _rms(y, fg_ref[...])


def _moe(h2, comb, x1, mod, wg, wu, wd, fg, n_per_batch):
    T, D = x1.shape
    per_b = n_per_batch // MOE_TM
    tok = lambda w: pl.BlockSpec((MOE_TM, w), lambda t, e: (t, 0))
    return pl.pallas_call(
        _moe_kernel,
        grid=(T // MOE_TM, N_EXPERTS),
        in_specs=[tok(D), tok(LANES), tok(D),
                  pl.BlockSpec((1, 6, D), lambda t, e: (t // per_b, 0, 0)),
                  pl.BlockSpec((1, D, D_EXPERT), lambda t, e: (e, 0, 0)),
                  pl.BlockSpec((1, D, D_EXPERT), lambda t, e: (e, 0, 0)),
                  pl.BlockSpec((1, D_EXPERT, D), lambda t, e: (e, 0, 0)),
                  pl.BlockSpec((1, D), lambda t, e: (0, 0))],
        out_specs=tok(D),
        out_shape=jax.ShapeDtypeStruct((T, D), F32),
        scratch_shapes=[pltpu.VMEM((MOE_TM, D), F32)],
        compiler_params=pltpu.CompilerParams(vmem_limit_bytes=VMEM_LIMIT),
        name="moe",
    )(h2, comb, x1, mod, wg, wu, wd, fg)


def _rope_tables(n):
    pos = jnp.arange(n)
    half = MLA_ROPE // 2
    inv_freq = ROPE_THETA ** (-jnp.arange(0, half, 2, dtype=F32) / half)
    ang = jnp.concatenate([(pos // GRID_W).astype(F32)[:, None] * inv_freq,
                           (pos % GRID_W).astype(F32)[:, None] * inv_freq], axis=-1)
    cos, sin = jnp.cos(ang), jnp.sin(ang)
    one = jnp.ones((n, MLA_NOPE), F32)
    zero = jnp.zeros((n, MLA_NOPE), F32)
    pad = jnp.zeros((n, LANES - MLA_QK), F32)
    z16 = jnp.zeros((n, half), F32)
    c = jnp.concatenate([one, cos, cos, pad], axis=1)
    sa = jnp.concatenate([zero, -sin, z16, pad], axis=1)
    sb = jnp.concatenate([zero, z16, sin, pad], axis=1)
    ident = jnp.concatenate([jnp.ones((CTX_LEN, MLA_QK), F32), jnp.zeros((CTX_LEN, LANES - MLA_QK), F32)], axis=1)
    zc = jnp.zeros((CTX_LEN, LANES), F32)
    return (jnp.concatenate([ident, c], axis=0), jnp.concatenate([zc, sa], axis=0),
            jnp.concatenate([zc, sb], axis=0))


def kernel(x, c, ctx, c_ctx, w_mod, b_mod, norm_attn_g, norm_ffn_g, w_in, q_a_norm_g, kv_a_norm_g, w_uq, w_ukv,
           na_rel_bias, w_out, w_router_group, b_router_group, w_router_expert, b_router_expert, w_gate, w_up,
           w_down, final_norm_g):
    B, N, D = x.shape
    assert w_mod.shape[0] == 1, "single-layer kernel"
    assert (B, N, D) == (2, 8192, D_MODEL) and ctx.shape == (B, CTX_LEN, D)

    cond8 = jnp.concatenate([c, c_ctx[None, :], jnp.zeros((8 - B - 1, D), F32)], axis=0)
    mod = _adaln(cond8, w_mod[0], b_mod[0]).reshape(8, 6, D)

    wi = w_in[0]
    na_w = NA_HEADS * NA_DIM
    k0 = Q_LORA + na_w + KV_LORA
    win = jnp.concatenate([wi[:, :k0], jnp.zeros((D, MLA_NOPE), F32), wi[:, k0:k0 + MLA_ROPE],
                           jnp.zeros((D, LANES - MLA_QK), F32), wi[:, k0 + MLA_ROPE:]], axis=1).astype(BF16)
    assert win.shape[1] == C_END
    wuq = jnp.pad(w_uq[0].reshape(Q_LORA, MLA_HEADS, MLA_QK), ((0, 0), (0, 0), (0, LANES - MLA_QK)))
    wuq = wuq.reshape(Q_LORA, MLA_HEADS * LANES).astype(BF16)
    wkv = w_ukv[0].reshape(KV_LORA, MLA_HEADS, MLA_NOPE + MLA_V)
    wk = jnp.pad(wkv[:, :, :MLA_NOPE], ((0, 0), (0, 0), (0, LANES - MLA_NOPE))).reshape(KV_LORA, MLA_HEADS * LANES)
    wv = wkv[:, :, MLA_NOPE:].reshape(KV_LORA, MLA_HEADS * MLA_V)
    wukv = jnp.concatenate([wk, wv], axis=1).astype(BF16)
    rc, rsa, rsb = _rope_tables(N)

    q_m, k_m, v_m, nq, nk, nv = _proj(x, ctx, mod, norm_attn_g, win, q_a_norm_g, kv_a_norm_g, wuq, wukv,
                                      rc, rsa, rsb)
    o_mla = _mla(q_m, k_m, v_m)
    o_na = _natten(nq, nk, nv, _na_bias(na_rel_bias[0]))

    n_r = N_GROUPS + N_EXPERTS
    wr = jnp.pad(jnp.concatenate([w_router_group[0], w_router_expert[0]], axis=1), ((0, 0), (0, LANES - n_r)))
    br = jnp.pad(jnp.concatenate([b_router_group[0], b_router_expert[0]]), (0, LANES - n_r)).reshape(1, LANES)
    T = B * N
    x1, h2, comb = _outproj(o_mla.reshape(T, -1), o_na.reshape(T, -1), x.reshape(T, D), mod,
                            w_out[0].astype(BF16), norm_ffn_g, wr, br, N)
    out = _moe(h2, comb, x1, mod, w_gate[0].astype(BF16), w_up[0].astype(BF16), w_down[0].astype(BF16),
               final_norm_g.reshape(1, D), N)
    return out.reshape(B, N, D)
```

```python
import functools

import jax
import jax.numpy as jnp
from jax import lax
from jax.experimental import pallas as pl
from jax.experimental.pallas import tpu as pltpu

D_MODEL = 1024
GRID_W = 64
CTX_LEN = 256
MLA_HEADS = 8
MLA_NOPE = 64
MLA_ROPE = 32
MLA_QK = MLA_NOPE + MLA_ROPE
MLA_V = 64
Q_LORA = 384
KV_LORA = 256
NA_HEADS = 8
NA_DIM = 64
NA_WIN_H = 8
NA_WIN_W = 16
N_GROUPS = 4
EXPERTS_PER_GROUP = 4
N_EXPERTS = N_GROUPS * EXPERTS_PER_GROUP
D_EXPERT = 512
ROPE_THETA = 10000.0
NORM_EPS = 1e-6
MASK_VALUE = -1e30

LANES = 128
HEAD_PAIRS = 4
TOK_BLOCK = 256
ROWS_PER_BLOCK = TOK_BLOCK // GRID_W
VMEM_LIMIT = 48 * 1024 * 1024

F32 = jnp.float32
BF16 = jnp.bfloat16


def _rms(x, g):
    return x * lax.rsqrt(jnp.mean(x * x, axis=-1, keepdims=True) + NORM_EPS) * g


def _adaln_kernel(c_ref, w_ref, b_ref, o_ref):
    c = c_ref[...]
    s = c * jax.nn.sigmoid(c)
    o_ref[...] = jnp.dot(s, w_ref[...], precision=lax.Precision.HIGHEST,
                         preferred_element_type=F32) + b_ref[...]


def _adaln(cond8, w_mod, b_mod):
    n_out = w_mod.shape[1]
    bn = 1024
    return pl.pallas_call(
        _adaln_kernel,
        grid=(n_out // bn,),
        in_specs=[pl.BlockSpec((8, D_MODEL), lambda n: (0, 0)),
                  pl.BlockSpec((D_MODEL, bn), lambda n: (0, n)),
                  pl.BlockSpec((1, bn), lambda n: (0, n))],
        out_specs=pl.BlockSpec((8, bn), lambda n: (0, n)),
        out_shape=jax.ShapeDtypeStruct((8, n_out), F32),
        name="adaln",
    )(cond8, w_mod, b_mod.reshape(1, n_out))


NA_VARIANTS = ((-1, False), (0, False), (1, False), (0, True), (1, True), (-1, True), (0, None))
V_PREV, V_CUR, V_NEXT, V_CUR_ALL, V_NEXT_ALL, V_PREV_ALL, V_NONE = range(7)
N_DR = 2 * NA_WIN_H - 1
N_DC = 2 * NA_WIN_W - 1


def _na_bias_kernel(rel_ref, o_ref):
    h = pl.program_id(0)
    qc = lax.broadcasted_iota(jnp.int32, (GRID_W, TOK_BLOCK), 0)
    kk = lax.broadcasted_iota(jnp.int32, (GRID_W, TOK_BLOCK), 1)
    kc = kk % GRID_W
    kblk = kk // GRID_W
    col_start = jnp.clip(qc - NA_WIN_W // 2, 0, GRID_W - NA_WIN_W)
    col_in = (kc >= col_start) & (kc < col_start + NA_WIN_W)
    dc = kc - qc + (NA_WIN_W - 1)
    tables = []
    for d in range(N_DR):
        t = jnp.full((GRID_W, TOK_BLOCK), MASK_VALUE, F32)
        base = (h * N_DR + d) * N_DC
        for m in range(N_DC):
            t = jnp.where(dc == m, rel_ref[base + m], t)
        tables.append(jnp.where(col_in, t, MASK_VALUE))
    for v, (boff, all_valid) in enumerate(NA_VARIANTS):
        for a in range(ROWS_PER_BLOCK):
            tile = jnp.full((GRID_W, TOK_BLOCK), MASK_VALUE, F32)
            if all_valid is not None:
                for cb in range(ROWS_PER_BLOCK):
                    dr = ROWS_PER_BLOCK * boff + cb - a
                    if all_valid or (-(NA_WIN_H // 2) <= dr < NA_WIN_H - NA_WIN_H // 2):
                        tile = jnp.where(kblk == cb, tables[dr + NA_WIN_H - 1], tile)
            o_ref[v, 0, a * GRID_W:(a + 1) * GRID_W, :] = tile


def _na_bias(rel_bias):
    return pl.pallas_call(
        _na_bias_kernel,
        grid=(NA_HEADS,),
        in_specs=[pl.BlockSpec(memory_space=pltpu.SMEM)],
        out_specs=pl.BlockSpec((len(NA_VARIANTS), 1, TOK_BLOCK, TOK_BLOCK), lambda h: (0, h, 0, 0)),
        out_shape=jax.ShapeDtypeStruct((len(NA_VARIANTS), NA_HEADS, TOK_BLOCK, TOK_BLOCK), F32),
        name="na_bias",
    )(rel_bias.reshape(-1))


C_QLAT = 0
C_NAQ = Q_LORA
C_KVLAT = C_NAQ + NA_HEADS * NA_DIM
C_KROPE = C_KVLAT + KV_LORA
C_NAK = C_KROPE + LANES
C_NAV = C_NAK + NA_HEADS * NA_DIM
C_END = C_NAV + NA_HEADS * NA_DIM
LOG2E = 1.4426950408889634
MLA_SCALE = MLA_QK ** -0.5 * LOG2E
NA_SCALE = NA_DIM ** -0.5
ONES_LANE = MLA_V


def _rope(x, c, sa, sb):
    return x * c + pltpu.roll(x, LANES - MLA_ROPE // 2, 1) * sa + pltpu.roll(x, MLA_ROPE // 2, 1) * sb


def _proj_kernel(x_ref, ctx_ref, mod_ref, g_ref, win_ref, qg_ref, kvg_ref, wuq_ref, wukv_ref,
                 c_ref, sa_ref, sb_ref,
                 q_out, k_out, v_out, nq_out, nk_out, nv_out, h_scr):
    j = pl.program_id(1)
    shift = mod_ref[0, 0:1, :]
    scale = mod_ref[0, 1:2, :]

    def norm_mod(xt):
        h_scr[...] = (_rms(xt, g_ref[...]) * (1.0 + scale) + shift).astype(BF16)

    @pl.when(j == 0)
    def _():
        norm_mod(ctx_ref[0])

    @pl.when(j > 0)
    def _():
        norm_mod(x_ref[0])

    proj = jnp.dot(h_scr[...], win_ref[...], preferred_element_type=F32)
    c, sa, sb = c_ref[...], sa_ref[...], sb_ref[...]

    qn = _rms(proj[:, C_QLAT:C_NAQ], qg_ref[...]).astype(BF16)
    q = jnp.dot(qn, wuq_ref[...], preferred_element_type=F32) * MLA_SCALE
    for h in range(MLA_HEADS):
        q_out[0, h] = _rope(q[:, h * LANES:(h + 1) * LANES], c, sa, sb).astype(BF16)

    kvn = _rms(proj[:, C_KVLAT:C_KROPE], kvg_ref[...]).astype(BF16)
    kv = jnp.dot(kvn, wukv_ref[...], preferred_element_type=F32)
    kr = _rope(proj[:, C_KROPE:C_NAK], c, sa, sb)
    v0 = MLA_HEADS * LANES
    lane = lax.broadcasted_iota(jnp.int32, (TOK_BLOCK, LANES), 1)
    for h in range(MLA_HEADS):
        k_out[0, h] = (kv[:, h * LANES:(h + 1) * LANES] + kr).astype(BF16)
        v_out[0, h] = jnp.where(lane == ONES_LANE, 1.0, kv[:, v0 + h * LANES:v0 + (h + 1) * LANES]).astype(BF16)
    for p in range(HEAD_PAIRS):
        nq_out[0, p] = (proj[:, C_NAQ + p * LANES:C_NAQ + (p + 1) * LANES] * NA_SCALE).astype(BF16)
        nk_out[0, p] = proj[:, C_NAK + p * LANES:C_NAK + (p + 1) * LANES].astype(BF16)
        nv_out[0, p] = proj[:, C_NAV + p * LANES:C_NAV + (p + 1) * LANES].astype(BF16)


def _proj(x, ctx, mod, g, win, qg, kvg, wuq, wukv, rc, rsa, rsb):
    B, N, D = x.shape
    nb = N // TOK_BLOCK
    ntot = N + CTX_LEN
    full = lambda shape: pl.BlockSpec(shape, lambda b, j: (0,) * len(shape))
    lat = lambda b, j: (b, 0, jnp.maximum(j - 1, 0), 0)
    alltok = lambda b, j: (b, 0, j, 0)
    return pl.pallas_call(
        _proj_kernel,
        grid=(B, nb + 1),
        in_specs=[pl.BlockSpec((1, TOK_BLOCK, D), lambda b, j: (b, jnp.maximum(j - 1, 0), 0)),
                  pl.BlockSpec((1, CTX_LEN, D), lambda b, j: (b, 0, 0)),
                  pl.BlockSpec((1, 6, D), lambda b, j: (jnp.where(j == 0, B, b), 0, 0)),
                  full((1, D)), full(win.shape), full((1, Q_LORA)), full((1, KV_LORA)),
                  full(wuq.shape), full(wukv.shape),
                  pl.BlockSpec((TOK_BLOCK, LANES), lambda b, j: (j, 0)),
                  pl.BlockSpec((TOK_BLOCK, LANES), lambda b, j: (j, 0)),
                  pl.BlockSpec((TOK_BLOCK, LANES), lambda b, j: (j, 0))],
        out_specs=[pl.BlockSpec((1, MLA_HEADS, TOK_BLOCK, LANES), lat),
                   pl.BlockSpec((1, MLA_HEADS, TOK_BLOCK, LANES), alltok),
                   pl.BlockSpec((1, MLA_HEADS, TOK_BLOCK, LANES), alltok),
                   pl.BlockSpec((1, HEAD_PAIRS, TOK_BLOCK, LANES), lat),
                   pl.BlockSpec((1, HEAD_PAIRS, TOK_BLOCK, LANES), alltok),
                   pl.BlockSpec((1, HEAD_PAIRS, TOK_BLOCK, LANES), alltok)],
        out_shape=[jax.ShapeDtypeStruct((B, MLA_HEADS, N, LANES), BF16),
                   jax.ShapeDtypeStruct((B, MLA_HEADS, ntot, LANES), BF16),
                   jax.ShapeDtypeStruct((B, MLA_HEADS, ntot, LANES), BF16),
                   jax.ShapeDtypeStruct((B, HEAD_PAIRS, N, LANES), BF16),
                   jax.ShapeDtypeStruct((B, HEAD_PAIRS, ntot, LANES), BF16),
                   jax.ShapeDtypeStruct((B, HEAD_PAIRS, ntot, LANES), BF16)],
        scratch_shapes=[pltpu.VMEM((TOK_BLOCK, D), BF16)],
        compiler_params=pltpu.CompilerParams(vmem_limit_bytes=VMEM_LIMIT),
        name="proj",
    )(x, ctx, mod, g, win, qg, kvg, wuq, wukv, rc, rsa, rsb)


MLA_TQ = 512
MLA_TK = 768


def _mla_kernel(q_ref, k_ref, v_ref, o_ref, m_scr, alpha_scr, acc_scr, p_scr):
    n_chunks = k_ref.shape[2] // MLA_TK
    m_scr[...] = jnp.full(m_scr.shape, -jnp.inf, F32)
    acc_scr[...] = jnp.zeros(acc_scr.shape, F32)

    def scores(ci, hh):
        off = pl.multiple_of(ci * MLA_TK, MLA_TK)
        k = k_ref[0, hh, pl.ds(off, MLA_TK), :]
        s = lax.dot_general(q_ref[0, hh], k, (((1,), (1,)), ((), ())), preferred_element_type=F32)
        m_prev = m_scr[hh]
        m_next = jnp.maximum(m_prev, jnp.max(s, axis=1, keepdims=True))
        p_scr[hh] = jnp.exp2((s - pltpu.repeat(m_next, MLA_TK // LANES, axis=1)).astype(BF16))
        alpha_scr[hh] = jnp.exp2(m_prev - m_next)
        m_scr[hh] = m_next

    def values(ci, hh):
        off = pl.multiple_of(ci * MLA_TK, MLA_TK)
        v = v_ref[0, hh, pl.ds(off, MLA_TK), :]
        acc_scr[hh] = alpha_scr[hh] * acc_scr[hh] + jnp.dot(p_scr[hh], v, preferred_element_type=F32)

    for hh in range(2):
        scores(0, hh)

    def body(ci, carry):
        for hh in range(2):
            values(ci - 1, hh)
            scores(ci, hh)
        return carry

    lax.fori_loop(1, n_chunks, body, 0)
    for hh in range(2):
        values(n_chunks - 1, hh)
    lane = lax.broadcasted_iota(jnp.int32, (MLA_TQ, LANES), 1)
    outs = []
    for hh in range(2):
        acc = acc_scr[hh]
        denom = jnp.sum(jnp.where(lane == ONES_LANE, acc, 0.0), axis=1, keepdims=True)
        outs.append(acc / denom)
    o_ref[0] = jnp.where(lane < MLA_V, outs[0], pltpu.roll(outs[1], MLA_V, 1)).astype(o_ref.dtype)


def _mla(q_m, k_m, v_m):
    B, _, N, _ = q_m.shape
    ntot = k_m.shape[2]
    return pl.pallas_call(
        _mla_kernel,
        grid=(B, HEAD_PAIRS, N // MLA_TQ),
        in_specs=[pl.BlockSpec((1, 2, MLA_TQ, LANES), lambda b, p, i: (b, p, i, 0)),
                  pl.BlockSpec((1, 2, ntot, LANES), lambda b, p, i: (b, p, 0, 0)),
                  pl.BlockSpec((1, 2, ntot, LANES), lambda b, p, i: (b, p, 0, 0))],
        out_specs=pl.BlockSpec((1, MLA_TQ, LANES), lambda b, p, i: (b, i, p)),
        out_shape=jax.ShapeDtypeStruct((B, N, MLA_HEADS * MLA_V), BF16),
        scratch_shapes=[pltpu.VMEM((2, MLA_TQ, LANES), F32)] * 3 + [pltpu.VMEM((2, MLA_TQ, MLA_TK), BF16)],
        compiler_params=pltpu.CompilerParams(vmem_limit_bytes=VMEM_LIMIT),
        name="mla",
    )(q_m, k_m, v_m)


def _na_kernel(q_ref, kc_ref, kp_ref, k0_ref, kn_ref, vc_ref, vp_ref, v0_ref, vn_ref,
               bp_ref, b0_ref, bn_ref, o_ref):
    lane = lax.broadcasted_iota(jnp.int32, (TOK_BLOCK, LANES), 1)
    dn = (((1,), (1,)), ((), ()))
    for p in range(HEAD_PAIRS):
        outs = []
        for half in range(2):
            h = 2 * p + half
            in_half = (lane >= NA_DIM) if half else (lane < NA_DIM)
            q = jnp.where(in_half, q_ref[0, p], jnp.zeros((), BF16))
            s = [lax.dot_general(q, kc_ref[0, p], dn, preferred_element_type=F32),
                 lax.dot_general(q, kp_ref[0, p], dn, preferred_element_type=F32) + bp_ref[0, h],
                 lax.dot_general(q, k0_ref[0, p], dn, preferred_element_type=F32) + b0_ref[0, h],
                 lax.dot_general(q, kn_ref[0, p], dn, preferred_element_type=F32) + bn_ref[0, h]]
            m = jnp.max(jnp.maximum(jnp.maximum(s[0], s[1]), jnp.maximum(s[2], s[3])), axis=1, keepdims=True)
            e = [jnp.exp(t - m) for t in s]
            l = jnp.sum(e[0] + e[1] + e[2] + e[3], axis=1, keepdims=True)
            o = jnp.dot(e[0].astype(BF16), vc_ref[0, p], preferred_element_type=F32)
            for t, v_ref in zip(e[1:], (vp_ref, v0_ref, vn_ref)):
                o = o + jnp.dot(t.astype(BF16), v_ref[0, p], preferred_element_type=F32)
            outs.append(o / l)
        o_ref[0, :, p * LANES:(p + 1) * LANES] = jnp.where(lane < NA_DIM, outs[0], outs[1]).astype(o_ref.dtype)


def _natten(nq, nk, nv, bias):
    B, _, N, _ = nq.shape
    nb = N // TOK_BLOCK
    last = nb - 1
    blk = (1, HEAD_PAIRS, TOK_BLOCK, LANES)
    ctx_map = lambda b, i: (b, 0, 0, 0)
    prev_map = lambda b, i: (b, 0, jnp.maximum(i - 1, 0) + 1, 0)
    cur_map = lambda b, i: (b, 0, i + 1, 0)
    next_map = lambda b, i: (b, 0, jnp.minimum(i + 1, last) + 1, 0)
    bblk = (1, NA_HEADS, TOK_BLOCK, TOK_BLOCK)
    bprev = lambda b, i: (jnp.where(i == 0, V_NONE, jnp.where(i == last, V_PREV_ALL, V_PREV)), 0, 0, 0)
    bcur = lambda b, i: (jnp.where((i == 0) | (i == last), V_CUR_ALL, V_CUR), 0, 0, 0)
    bnext = lambda b, i: (jnp.where(i == 0, V_NEXT_ALL, jnp.where(i == last, V_NONE, V_NEXT)), 0, 0, 0)
    return pl.pallas_call(
        _na_kernel,
        grid=(B, nb),
        in_specs=[pl.BlockSpec(blk, lambda b, i: (b, 0, i, 0)),
                  pl.BlockSpec(blk, ctx_map), pl.BlockSpec(blk, prev_map),
                  pl.BlockSpec(blk, cur_map), pl.BlockSpec(blk, next_map),
                  pl.BlockSpec(blk, ctx_map), pl.BlockSpec(blk, prev_map),
                  pl.BlockSpec(blk, cur_map), pl.BlockSpec(blk, next_map),
                  pl.BlockSpec(bblk, bprev), pl.BlockSpec(bblk, bcur), pl.BlockSpec(bblk, bnext)],
        out_specs=pl.BlockSpec((1, TOK_BLOCK, NA_HEADS * NA_DIM), lambda b, i: (b, i, 0)),
        out_shape=jax.ShapeDtypeStruct((B, N, NA_HEADS * NA_DIM), BF16),
        compiler_params=pltpu.CompilerParams(vmem_limit_bytes=VMEM_LIMIT),
        name="natten",
    )(nq, nk, nk, nk, nk, nv, nv, nv, nv, bias, bias, bias)


OUT_TM = 512
R_G1, R_SH2, R_SC2, R_G2 = 2, 3, 4, 5


def _first_max_lane(x, valid, lane):
    xm = jnp.where(valid, x, -jnp.inf)
    mx = jnp.max(xm, axis=1, keepdims=True)
    idx = jnp.min(jnp.where(valid & (xm == mx), lane, LANES), axis=1, keepdims=True)
    return mx, idx


def _outproj_kernel(om_ref, on_ref, x_ref, mod_ref, wo_ref, g_ref, wr_ref, br_ref,
                    x1_out, h2_out, comb_out):
    half = wo_ref.shape[0] // 2
    mixed = (jnp.dot(om_ref[...], wo_ref[:half, :], preferred_element_type=F32)
             + jnp.dot(on_ref[...], wo_ref[half:, :], preferred_element_type=F32))
    x1 = x_ref[...] + mod_ref[0, R_G1:R_G1 + 1, :] * mixed
    x1_out[...] = x1
    h2 = _rms(x1, g_ref[...]) * (1.0 + mod_ref[0, R_SC2:R_SC2 + 1, :]) + mod_ref[0, R_SH2:R_SH2 + 1, :]
    h2_out[...] = h2.astype(BF16)

    logits = jnp.dot(h2, wr_ref[...], precision=lax.Precision.HIGHEST,
                     preferred_element_type=F32) + br_ref[...]
    lane = lax.broadcasted_iota(jnp.int32, logits.shape, 1)
    is_g = lane < N_GROUPS
    g_max, g_sel = _first_max_lane(logits, is_g, lane)
    g_w = 1.0 / jnp.sum(jnp.where(is_g, jnp.exp(logits - g_max), 0.0), axis=1, keepdims=True)
    e_lo = N_GROUPS + g_sel * EXPERTS_PER_GROUP
    in_grp = (lane >= e_lo) & (lane < e_lo + EXPERTS_PER_GROUP)
    m1, i1 = _first_max_lane(logits, in_grp, lane)
    m2, i2 = _first_max_lane(logits, in_grp & (lane != i1), lane)
    e2 = jnp.exp(m2 - m1)
    w1 = 1.0 / (1.0 + e2)
    w2 = e2 / (1.0 + e2)
    comb = jnp.where(lane == i1, g_w * w1, jnp.where(lane == i2, g_w * w2, 0.0))
    comb_out[...] = pltpu.roll(comb, LANES - N_GROUPS, 1)


def _outproj(o_mla, o_na, x2d, mod, wo, g, wr, br, n_per_batch):
    T, D = x2d.shape
    per_b = n_per_batch // OUT_TM
    tok = lambda w: pl.BlockSpec((OUT_TM, w), lambda t: (t, 0))
    full = lambda shape: pl.BlockSpec(shape, lambda t: (0,) * len(shape))
    return pl.pallas_call(
        _outproj_kernel,
        grid=(T // OUT_TM,),
        in_specs=[tok(o_mla.shape[1]), tok(o_na.shape[1]), tok(D),
                  pl.BlockSpec((1, 6, D), lambda t: (t // per_b, 0, 0)),
                  full(wo.shape), full((1, D)), full(wr.shape), full(br.shape)],
        out_specs=[tok(D), tok(D), tok(LANES)],
        out_shape=[jax.ShapeDtypeStruct((T, D), F32), jax.ShapeDtypeStruct((T, D), BF16),
                   jax.ShapeDtypeStruct((T, LANES), F32)],
        compiler_params=pltpu.CompilerParams(vmem_limit_bytes=VMEM_LIMIT),
        name="outproj",
    )(o_mla, o_na, x2d, mod, wo, g, wr, br)


MOE_TM = 1024


def _moe_kernel(h_ref, comb_ref, x1_ref, mod_ref, wg_ref, wu_ref, wd_ref, fg_ref, o_ref, acc_scr):
    e = pl.program_id(1)

    @pl.when(e == 0)
    def _():
        acc_scr[...] = jnp.zeros(acc_scr.shape, F32)

    h = h_ref[...]
    gate = jnp.dot(h, wg_ref[0], preferred_element_type=F32)
    up = jnp.dot(h, wu_ref[0], preferred_element_type=F32)
    lane = lax.broadcasted_iota(jnp.int32, comb_ref.shape, 1)
    w = jnp.sum(jnp.where(lane == e, comb_ref[...], 0.0), axis=1, keepdims=True)
    hid = (gate * jax.nn.sigmoid(gate) * up * w).astype(BF16)
    acc_scr[...] += jnp.dot(hid, wd_ref[0], preferred_element_type=F32)

    @pl.when(e == N_EXPERTS - 1)
    def _():
        y = x1_ref[...] + mod_ref[0, R_G2:R_G2 + 1, :] * acc_scr[...]
        o_ref[...] = _rms(y, fg_ref[...])


def _moe(h2, comb, x1, mod, wg, wu, wd, fg, n_per_batch):
    T, D = x1.shape
    per_b = n_per_batch // MOE_TM
    tok = lambda w: pl.BlockSpec((MOE_TM, w), lambda t, e: (t, 0))
    return pl.pallas_call(
        _moe_kernel,
        grid=(T // MOE_TM, N_EXPERTS),
        in_specs=[tok(D), tok(LANES), tok(D),
                  pl.BlockSpec((1, 6, D), lambda t, e: (t // per_b, 0, 0)),
                  pl.BlockSpec((1, D, D_EXPERT), lambda t, e: (e, 0, 0)),
                  pl.BlockSpec((1, D, D_EXPERT), lambda t, e: (e, 0, 0)),
                  pl.BlockSpec((1, D_EXPERT, D), lambda t, e: (e, 0, 0)),
                  pl.BlockSpec((1, D), lambda t, e: (0, 0))],
        out_specs=tok(D),
        out_shape=jax.ShapeDtypeStruct((T, D), F32),
        scratch_shapes=[pltpu.VMEM((MOE_TM, D), F32)],
        compiler_params=pltpu.CompilerParams(vmem_limit_bytes=VMEM_LIMIT),
        name="moe",
    )(h2, comb, x1, mod, wg, wu, wd, fg)


def _rope_tables(n):
    pos = jnp.arange(n)
    half = MLA_ROPE // 2
    inv_freq = ROPE_THETA ** (-jnp.arange(0, half, 2, dtype=F32) / half)
    ang = jnp.concatenate([(pos // GRID_W).astype(F32)[:, None] * inv_freq,
                           (pos % GRID_W).astype(F32)[:, None] * inv_freq], axis=-1)
    cos, sin = jnp.cos(ang), jnp.sin(ang)
    one = jnp.ones((n, MLA_NOPE), F32)
    zero = jnp.zeros((n, MLA_NOPE), F32)
    pad = jnp.zeros((n, LANES - MLA_QK), F32)
    z16 = jnp.zeros((n, half), F32)
    c = jnp.concatenate([one, cos, cos, pad], axis=1)
    sa = jnp.concatenate([zero, -sin, z16, pad], axis=1)
    sb = jnp.concatenate([zero, z16, sin, pad], axis=1)
    ident = jnp.concatenate([jnp.ones((CTX_LEN, MLA_QK), F32), jnp.zeros((CTX_LEN, LANES - MLA_QK), F32)], axis=1)
    zc = jnp.zeros((CTX_LEN, LANES), F32)
    return (jnp.concatenate([ident, c], axis=0), jnp.concatenate([zc, sa], axis=0),
            jnp.concatenate([zc, sb], axis=0))


def kernel(x, c, ctx, c_ctx, w_mod, b_mod, norm_attn_g, norm_ffn_g, w_in, q_a_norm_g, kv_a_norm_g, w_uq, w_ukv,
           na_rel_bias, w_out, w_router_group, b_router_group, w_router_expert, b_router_expert, w_gate, w_up,
           w_down, final_norm_g):
    B, N, D = x.shape
    assert w_mod.shape[0] == 1, "single-layer kernel"
    assert (B, N, D) == (2, 8192, D_MODEL) and ctx.shape == (B, CTX_LEN, D)

    cond8 = jnp.concatenate([c, c_ctx[None, :], jnp.zeros((8 - B - 1, D), F32)], axis=0)
    mod = _adaln(cond8, w_mod[0], b_mod[0]).reshape(8, 6, D)

    wi = w_in[0]
    na_w = NA_HEADS * NA_DIM
    k0 = Q_LORA + na_w + KV_LORA
    win = jnp.concatenate([wi[:, :k0], jnp.zeros((D, MLA_NOPE), F32), wi[:, k0:k0 + MLA_ROPE],
                           jnp.zeros((D, LANES - MLA_QK), F32), wi[:, k0 + MLA_ROPE:]], axis=1).astype(BF16)
    assert win.shape[1] == C_END
    wuq = jnp.pad(w_uq[0].reshape(Q_LORA, MLA_HEADS, MLA_QK), ((0, 0), (0, 0), (0, LANES - MLA_QK)))
    wuq = wuq.reshape(Q_LORA, MLA_HEADS * LANES).astype(BF16)
    wkv = w_ukv[0].reshape(KV_LORA, MLA_HEADS, MLA_NOPE + MLA_V)
    wk = jnp.pad(wkv[:, :, :MLA_NOPE], ((0, 0), (0, 0), (0, LANES - MLA_NOPE))).reshape(KV_LORA, MLA_HEADS * LANES)
    wv = jnp.pad(wkv[:, :, MLA_NOPE:], ((0, 0), (0, 0), (0, LANES - MLA_V))).reshape(KV_LORA, MLA_HEADS * LANES)
    wukv = jnp.concatenate([wk, wv], axis=1).astype(BF16)
    rc, rsa, rsb = _rope_tables(N)

    q_m, k_m, v_m, nq, nk, nv = _proj(x, ctx, mod, norm_attn_g, win, q_a_norm_g, kv_a_norm_g, wuq, wukv,
                                      rc, rsa, rsb)
    o_mla = _mla(q_m, k_m, v_m)
    o_na = _natten(nq, nk, nv, _na_bias(na_rel_bias[0]))

    n_r = N_GROUPS + N_EXPERTS
    wr = jnp.pad(jnp.concatenate([w_router_group[0], w_router_expert[0]], axis=1), ((0, 0), (0, LANES - n_r)))
    br = jnp.pad(jnp.concatenate([b_router_group[0], b_router_expert[0]]), (0, LANES - n_r)).reshape(1, LANES)
    T = B * N
    x1, h2, comb = _outproj(o_mla.reshape(T, -1), o_na.reshape(T, -1), x.reshape(T, D), mod,
                            w_out[0].astype(BF16), norm_ffn_g, wr, br, N)
    out = _moe(h2, comb, x1, mod, w_gate[0].astype(BF16), w_up[0].astype(BF16), w_down[0].astype(BF16),
               final_norm_g.reshape(1, D), N)
    return out.reshape(B, N, D)
```

```python
import functools

import jax
import jax.numpy as jnp
from jax import lax
from jax.experimental import pallas as pl
from jax.experimental.pallas import tpu as pltpu

D_MODEL = 1024
GRID_W = 64
CTX_LEN = 256
MLA_HEADS = 8
MLA_NOPE = 64
MLA_ROPE = 32
MLA_QK = MLA_NOPE + MLA_ROPE
MLA_V = 64
Q_LORA = 384
KV_LORA = 256
NA_HEADS = 8
NA_DIM = 64
NA_WIN_H = 8
NA_WIN_W = 16
N_GROUPS = 4
EXPERTS_PER_GROUP = 4
N_EXPERTS = N_GROUPS * EXPERTS_PER_GROUP
D_EXPERT = 512
ROPE_THETA = 10000.0
NORM_EPS = 1e-6
MASK_VALUE = -1e30

LANES = 128
HEAD_PAIRS = 4
TOK_BLOCK = 256
ROWS_PER_BLOCK = TOK_BLOCK // GRID_W
VMEM_LIMIT = 48 * 1024 * 1024

F32 = jnp.float32
BF16 = jnp.bfloat16


def _rms(x, g):
    return x * lax.rsqrt(jnp.mean(x * x, axis=-1, keepdims=True) + NORM_EPS) * g


def _adaln_kernel(c_ref, w_ref, b_ref, o_ref):
    c = c_ref[...]
    s = c * jax.nn.sigmoid(c)
    o_ref[...] = jnp.dot(s, w_ref[...], precision=lax.Precision.HIGHEST,
                         preferred_element_type=F32) + b_ref[...]


def _adaln(cond8, w_mod, b_mod):
    n_out = w_mod.shape[1]
    bn = 1024
    return pl.pallas_call(
        _adaln_kernel,
        grid=(n_out // bn,),
        in_specs=[pl.BlockSpec((8, D_MODEL), lambda n: (0, 0)),
                  pl.BlockSpec((D_MODEL, bn), lambda n: (0, n)),
                  pl.BlockSpec((1, bn), lambda n: (0, n))],
        out_specs=pl.BlockSpec((8, bn), lambda n: (0, n)),
        out_shape=jax.ShapeDtypeStruct((8, n_out), F32),
        name="adaln",
    )(cond8, w_mod, b_mod.reshape(1, n_out))


NA_VARIANTS = ((-1, False), (0, False), (1, False), (0, True), (1, True), (-1, True), (0, None))
V_PREV, V_CUR, V_NEXT, V_CUR_ALL, V_NEXT_ALL, V_PREV_ALL, V_NONE = range(7)
N_DR = 2 * NA_WIN_H - 1
N_DC = 2 * NA_WIN_W - 1


def _na_bias_kernel(rel_ref, o_ref):
    h = pl.program_id(0)
    qc = lax.broadcasted_iota(jnp.int32, (GRID_W, TOK_BLOCK), 0)
    kk = lax.broadcasted_iota(jnp.int32, (GRID_W, TOK_BLOCK), 1)
    kc = kk % GRID_W
    kblk = kk // GRID_W
    col_start = jnp.clip(qc - NA_WIN_W // 2, 0, GRID_W - NA_WIN_W)
    col_in = (kc >= col_start) & (kc < col_start + NA_WIN_W)
    dc = kc - qc + (NA_WIN_W - 1)
    tables = []
    for d in range(N_DR):
        t = jnp.full((GRID_W, TOK_BLOCK), MASK_VALUE, F32)
        base = (h * N_DR + d) * N_DC
        for m in range(N_DC):
            t = jnp.where(dc == m, rel_ref[base + m], t)
        tables.append(jnp.where(col_in, t, MASK_VALUE))
    for v, (boff, all_valid) in enumerate(NA_VARIANTS):
        for a in range(ROWS_PER_BLOCK):
            tile = jnp.full((GRID_W, TOK_BLOCK), MASK_VALUE, F32)
            if all_valid is not None:
                for cb in range(ROWS_PER_BLOCK):
                    dr = ROWS_PER_BLOCK * boff + cb - a
                    if all_valid or (-(NA_WIN_H // 2) <= dr < NA_WIN_H - NA_WIN_H // 2):
                        tile = jnp.where(kblk == cb, tables[dr + NA_WIN_H - 1], tile)
            o_ref[v, 0, a * GRID_W:(a + 1) * GRID_W, :] = tile


def _na_bias(rel_bias):
    return pl.pallas_call(
        _na_bias_kernel,
        grid=(NA_HEADS,),
        in_specs=[pl.BlockSpec(memory_space=pltpu.SMEM)],
        out_specs=pl.BlockSpec((len(NA_VARIANTS), 1, TOK_BLOCK, TOK_BLOCK), lambda h: (0, h, 0, 0)),
        out_shape=jax.ShapeDtypeStruct((len(NA_VARIANTS), NA_HEADS, TOK_BLOCK, TOK_BLOCK), F32),
        name="na_bias",
    )(rel_bias.reshape(-1))


C_QLAT = 0
C_NAQ = Q_LORA
C_KVLAT = C_NAQ + NA_HEADS * NA_DIM
C_KROPE = C_KVLAT + KV_LORA
C_NAK = C_KROPE + LANES
C_NAV = C_NAK + NA_HEADS * NA_DIM
C_END = C_NAV + NA_HEADS * NA_DIM
LOG2E = 1.4426950408889634
MLA_SCALE = MLA_QK ** -0.5 * LOG2E
NA_SCALE = NA_DIM ** -0.5
ONES_LANE = MLA_V


def _rope(x, c, sa, sb):
    return x * c + pltpu.roll(x, LANES - MLA_ROPE // 2, 1) * sa + pltpu.roll(x, MLA_ROPE // 2, 1) * sb


def _proj_kernel(x_ref, ctx_ref, mod_ref, g_ref, win_ref, qg_ref, kvg_ref, wuq_ref, wukv_ref,
                 c_ref, sa_ref, sb_ref,
                 q_out, k_out, v_out, nq_out, nk_out, nv_out, h_scr):
    j = pl.program_id(1)
    shift = mod_ref[0, 0:1, :]
    scale = mod_ref[0, 1:2, :]

    def norm_mod(xt):
        h_scr[...] = (_rms(xt, g_ref[...]) * (1.0 + scale) + shift).astype(BF16)

    @pl.when(j == 0)
    def _():
        norm_mod(ctx_ref[0])

    @pl.when(j > 0)
    def _():
        norm_mod(x_ref[0])

    proj = jnp.dot(h_scr[...], win_ref[...], preferred_element_type=F32)
    c, sa, sb = c_ref[...], sa_ref[...], sb_ref[...]

    qn = _rms(proj[:, C_QLAT:C_NAQ], qg_ref[...]).astype(BF16)
    q = jnp.dot(qn, wuq_ref[...], preferred_element_type=F32) * MLA_SCALE
    for h in range(MLA_HEADS):
        q_out[0, h] = _rope(q[:, h * LANES:(h + 1) * LANES], c, sa, sb).astype(BF16)

    kvn = _rms(proj[:, C_KVLAT:C_KROPE], kvg_ref[...]).astype(BF16)
    kv = jnp.dot(kvn, wukv_ref[...], preferred_element_type=F32)
    kr = _rope(proj[:, C_KROPE:C_NAK], c, sa, sb)
    v0 = MLA_HEADS * LANES
    lane = lax.broadcasted_iota(jnp.int32, (TOK_BLOCK, LANES), 1)
    for h in range(MLA_HEADS):
        k_out[0, h] = (kv[:, h * LANES:(h + 1) * LANES] + kr).astype(BF16)
        v_out[0, h] = jnp.where(lane == ONES_LANE, 1.0, kv[:, v0 + h * LANES:v0 + (h + 1) * LANES]).astype(BF16)
    for p in range(HEAD_PAIRS):
        nq_out[0, p] = (proj[:, C_NAQ + p * LANES:C_NAQ + (p + 1) * LANES] * NA_SCALE).astype(BF16)
        nk_out[0, p] = proj[:, C_NAK + p * LANES:C_NAK + (p + 1) * LANES].astype(BF16)
        nv_out[0, p] = proj[:, C_NAV + p * LANES:C_NAV + (p + 1) * LANES].astype(BF16)


def _proj(x, ctx, mod, g, win, qg, kvg, wuq, wukv, rc, rsa, rsb):
    B, N, D = x.shape
    nb = N // TOK_BLOCK
    ntot = N + CTX_LEN
    full = lambda shape: pl.BlockSpec(shape, lambda b, j: (0,) * len(shape))
    lat = lambda b, j: (b, 0, jnp.maximum(j - 1, 0), 0)
    alltok = lambda b, j: (b, 0, j, 0)
    return pl.pallas_call(
        _proj_kernel,
        grid=(B, nb + 1),
        in_specs=[pl.BlockSpec((1, TOK_BLOCK, D), lambda b, j: (b, jnp.maximum(j - 1, 0), 0)),
                  pl.BlockSpec((1, CTX_LEN, D), lambda b, j: (b, 0, 0)),
                  pl.BlockSpec((1, 6, D), lambda b, j: (jnp.where(j == 0, B, b), 0, 0)),
                  full((1, D)), full(win.shape), full((1, Q_LORA)), full((1, KV_LORA)),
                  full(wuq.shape), full(wukv.shape),
                  pl.BlockSpec((TOK_BLOCK, LANES), lambda b, j: (j, 0)),
                  pl.BlockSpec((TOK_BLOCK, LANES), lambda b, j: (j, 0)),
                  pl.BlockSpec((TOK_BLOCK, LANES), lambda b, j: (j, 0))],
        out_specs=[pl.BlockSpec((1, MLA_HEADS, TOK_BLOCK, LANES), lat),
                   pl.BlockSpec((1, MLA_HEADS, TOK_BLOCK, LANES), alltok),
                   pl.BlockSpec((1, MLA_HEADS, TOK_BLOCK, LANES), alltok),
                   pl.BlockSpec((1, HEAD_PAIRS, TOK_BLOCK, LANES), lat),
                   pl.BlockSpec((1, HEAD_PAIRS, TOK_BLOCK, LANES), alltok),
                   pl.BlockSpec((1, HEAD_PAIRS, TOK_BLOCK, LANES), alltok)],
        out_shape=[jax.ShapeDtypeStruct((B, MLA_HEADS, N, LANES), BF16),
                   jax.ShapeDtypeStruct((B, MLA_HEADS, ntot, LANES), BF16),
                   jax.ShapeDtypeStruct((B, MLA_HEADS, ntot, LANES), BF16),
                   jax.ShapeDtypeStruct((B, HEAD_PAIRS, N, LANES), BF16),
                   jax.ShapeDtypeStruct((B, HEAD_PAIRS, ntot, LANES), BF16),
                   jax.ShapeDtypeStruct((B, HEAD_PAIRS, ntot, LANES), BF16)],
        scratch_shapes=[pltpu.VMEM((TOK_BLOCK, D), BF16)],
        compiler_params=pltpu.CompilerParams(vmem_limit_bytes=VMEM_LIMIT),
        name="proj",
    )(x, ctx, mod, g, win, qg, kvg, wuq, wukv, rc, rsa, rsb)


MLA_TQ = 512
MLA_TK = 768


def _mla_kernel(q_ref, k_ref, v_ref, o_ref, m_scr, alpha_scr, acc_scr, p_scr):
    n_chunks = k_ref.shape[2] // MLA_TK
    m_scr[...] = jnp.full(m_scr.shape, -jnp.inf, F32)
    acc_scr[...] = jnp.zeros(acc_scr.shape, F32)

    def scores(ci, hh):
        off = pl.multiple_of(ci * MLA_TK, MLA_TK)
        k = k_ref[0, hh, pl.ds(off, MLA_TK), :]
        s = lax.dot_general(q_ref[0, hh], k, (((1,), (1,)), ((), ())), preferred_element_type=F32)
        m_prev = m_scr[hh]
        m_next = jnp.maximum(m_prev, jnp.max(s, axis=1, keepdims=True))
        p_scr[hh] = jnp.exp2((s - pltpu.repeat(m_next, MLA_TK // LANES, axis=1)).astype(BF16))
        alpha_scr[hh] = jnp.exp2(m_prev - m_next)
        m_scr[hh] = m_next

    def values(ci, hh):
        off = pl.multiple_of(ci * MLA_TK, MLA_TK)
        v = v_ref[0, hh, pl.ds(off, MLA_TK), :]
        acc_scr[hh] = alpha_scr[hh] * acc_scr[hh] + jnp.dot(p_scr[hh], v, preferred_element_type=F32)

    for hh in range(2):
        scores(0, hh)

    def body(ci, carry):
        for hh in range(2):
            values(ci - 1, hh)
            scores(ci, hh)
        return carry

    lax.fori_loop(1, n_chunks, body, 0)
    for hh in range(2):
        values(n_chunks - 1, hh)
    lane = lax.broadcasted_iota(jnp.int32, (MLA_TQ, LANES), 1)
    outs = []
    for hh in range(2):
        acc = acc_scr[hh]
        denom = jnp.sum(jnp.where(lane == ONES_LANE, acc, 0.0), axis=1, keepdims=True)
        outs.append(acc / denom)
    o_ref[0] = jnp.where(lane < MLA_V, outs[0], pltpu.roll(outs[1], MLA_V, 1)).astype(o_ref.dtype)


def _mla(q_m, k_m, v_m):
    B, _, N, _ = q_m.shape
    ntot = k_m.shape[2]
    return pl.pallas_call(
        _mla_kernel,
        grid=(B, HEAD_PAIRS, N // MLA_TQ),
        in_specs=[pl.BlockSpec((1, 2, MLA_TQ, LANES), lambda b, p, i: (b, p, i, 0)),
                  pl.BlockSpec((1, 2, ntot, LANES), lambda b, p, i: (b, p, 0, 0)),
                  pl.BlockSpec((1, 2, ntot, LANES), lambda b, p, i: (b, p, 0, 0))],
        out_specs=pl.BlockSpec((1, MLA_TQ, LANES), lambda b, p, i: (b, i, p)),
        out_shape=jax.ShapeDtypeStruct((B, N, MLA_HEADS * MLA_V), BF16),
        scratch_shapes=[pltpu.VMEM((2, MLA_TQ, LANES), F32)] * 3 + [pltpu.VMEM((2, MLA_TQ, MLA_TK), BF16)],
        compiler_params=pltpu.CompilerParams(vmem_limit_bytes=VMEM_LIMIT),
        name="mla",
    )(q_m, k_m, v_m)


def _na_kernel(q_ref, kc_ref, kp_ref, k0_ref, kn_ref, vc_ref, vp_ref, v0_ref, vn_ref,
               bp_ref, b0_ref, bn_ref, o_ref):
    lane = lax.broadcasted_iota(jnp.int32, (TOK_BLOCK, LANES), 1)
    dn = (((1,), (1,)), ((), ()))
    for p in range(HEAD_PAIRS):
        outs = []
        for half in range(2):
            h = 2 * p + half
            in_half = (lane >= NA_DIM) if half else (lane < NA_DIM)
            q = jnp.where(in_half, q_ref[0, p], jnp.zeros((), BF16))
            s = [lax.dot_general(q, kc_ref[0, p], dn, preferred_element_type=F32),
                 lax.dot_general(q, kp_ref[0, p], dn, preferred_element_type=F32) + bp_ref[0, h],
                 lax.dot_general(q, k0_ref[0, p], dn, preferred_element_type=F32) + b0_ref[0, h],
                 lax.dot_general(q, kn_ref[0, p], dn, preferred_element_type=F32) + bn_ref[0, h]]
            m = jnp.max(jnp.maximum(jnp.maximum(s[0], s[1]), jnp.maximum(s[2], s[3])), axis=1, keepdims=True)
            e = [jnp.exp(t - m) for t in s]
            l = jnp.sum(e[0] + e[1] + e[2] + e[3], axis=1, keepdims=True)
            o = jnp.dot(e[0].astype(BF16), vc_ref[0, p], preferred_element_type=F32)
            for t, v_ref in zip(e[1:], (vp_ref, v0_ref, vn_ref)):
                o = o + jnp.dot(t.astype(BF16), v_ref[0, p], preferred_element_type=F32)
            outs.append(o / l)
        o_ref[0, :, p * LANES:(p + 1) * LANES] = jnp.where(lane < NA_DIM, outs[0], outs[1]).astype(o_ref.dtype)


def _natten(nq, nk, nv, bias):
    B, _, N, _ = nq.shape
    nb = N // TOK_BLOCK
    last = nb - 1
    blk = (1, HEAD_PAIRS, TOK_BLOCK, LANES)
    ctx_map = lambda b, i: (b, 0, 0, 0)
    prev_map = lambda b, i: (b, 0, jnp.maximum(i - 1, 0) + 1, 0)
    cur_map = lambda b, i: (b, 0, i + 1, 0)
    next_map = lambda b, i: (b, 0, jnp.minimum(i + 1, last) + 1, 0)
    bblk = (1, NA_HEADS, TOK_BLOCK, TOK_BLOCK)
    bprev = lambda b, i: (jnp.where(i == 0, V_NONE, jnp.where(i == last, V_PREV_ALL, V_PREV)), 0, 0, 0)
    bcur = lambda b, i: (jnp.where((i == 0) | (i == last), V_CUR_ALL, V_CUR), 0, 0, 0)
    bnext = lambda b, i: (jnp.where(i == 0, V_NEXT_ALL, jnp.where(i == last, V_NONE, V_NEXT)), 0, 0, 0)
    return pl.pallas_call(
        _na_kernel,
        grid=(B, nb),
        in_specs=[pl.BlockSpec(blk, lambda b, i: (b, 0, i, 0)),
                  pl.BlockSpec(blk, ctx_map), pl.BlockSpec(blk, prev_map),
                  pl.BlockSpec(blk, cur_map), pl.BlockSpec(blk, next_map),
                  pl.BlockSpec(blk, ctx_map), pl.BlockSpec(blk, prev_map),
                  pl.BlockSpec(blk, cur_map), pl.BlockSpec(blk, next_map),
                  pl.BlockSpec(bblk, bprev), pl.BlockSpec(bblk, bcur), pl.BlockSpec(bblk, bnext)],
        out_specs=pl.BlockSpec((1, TOK_BLOCK, NA_HEADS * NA_DIM), lambda b, i: (b, i, 0)),
        out_shape=jax.ShapeDtypeStruct((B, N, NA_HEADS * NA_DIM), BF16),
        compiler_params=pltpu.CompilerParams(vmem_limit_bytes=VMEM_LIMIT),
        name="natten",
    )(nq, nk, nk, nk, nk, nv, nv, nv, nv, bias, bias, bias)


OUT_TM = 512
R_G1, R_SH2, R_SC2, R_G2 = 2, 3, 4, 5
GROUP_LANE = N_EXPERTS


def _first_max_lane(x, valid, lane):
    xm = jnp.where(valid, x, -jnp.inf)
    mx = jnp.max(xm, axis=1, keepdims=True)
    idx = jnp.min(jnp.where(valid & (xm == mx), lane, LANES), axis=1, keepdims=True)
    return mx, idx


def _outproj_kernel(om_ref, on_ref, x_ref, mod_ref, wo_ref, g_ref, wr_ref, br_ref,
                    x1_out, h2_out, comb_out):
    half = wo_ref.shape[0] // 2
    mixed = (jnp.dot(om_ref[...], wo_ref[:half, :], preferred_element_type=F32)
             + jnp.dot(on_ref[...], wo_ref[half:, :], preferred_element_type=F32))
    x1 = x_ref[...] + mod_ref[0, R_G1:R_G1 + 1, :] * mixed
    x1_out[...] = x1
    h2 = _rms(x1, g_ref[...]) * (1.0 + mod_ref[0, R_SC2:R_SC2 + 1, :]) + mod_ref[0, R_SH2:R_SH2 + 1, :]
    h2_out[...] = h2.astype(BF16)

    logits = jnp.dot(h2, wr_ref[...], precision=lax.Precision.HIGHEST,
                     preferred_element_type=F32) + br_ref[...]
    lane = lax.broadcasted_iota(jnp.int32, logits.shape, 1)
    is_g = lane < N_GROUPS
    g_max, g_sel = _first_max_lane(logits, is_g, lane)
    g_w = 1.0 / jnp.sum(jnp.where(is_g, jnp.exp(logits - g_max), 0.0), axis=1, keepdims=True)
    e_lo = N_GROUPS + g_sel * EXPERTS_PER_GROUP
    in_grp = (lane >= e_lo) & (lane < e_lo + EXPERTS_PER_GROUP)
    m1, i1 = _first_max_lane(logits, in_grp, lane)
    m2, i2 = _first_max_lane(logits, in_grp & (lane != i1), lane)
    e2 = jnp.exp(m2 - m1)
    w1 = 1.0 / (1.0 + e2)
    w2 = e2 / (1.0 + e2)
    comb = jnp.where(lane == i1, g_w * w1, jnp.where(lane == i2, g_w * w2, 0.0))
    comb_out[...] = jnp.where(lane == GROUP_LANE, g_sel.astype(F32), pltpu.roll(comb, LANES - N_GROUPS, 1))


def _outproj(o_mla, o_na, x2d, mod, wo, g, wr, br, n_per_batch):
    T, D = x2d.shape
    per_b = n_per_batch // OUT_TM
    tok = lambda w: pl.BlockSpec((OUT_TM, w), lambda t: (t, 0))
    full = lambda shape: pl.BlockSpec(shape, lambda t: (0,) * len(shape))
    return pl.pallas_call(
        _outproj_kernel,
        grid=(T // OUT_TM,),
        in_specs=[tok(o_mla.shape[1]), tok(o_na.shape[1]), tok(D),
                  pl.BlockSpec((1, 6, D), lambda t: (t // per_b, 0, 0)),
                  full(wo.shape), full((1, D)), full(wr.shape), full(br.shape)],
        out_specs=[tok(D), tok(D), tok(LANES)],
        out_shape=[jax.ShapeDtypeStruct((T, D), F32), jax.ShapeDtypeStruct((T, D), BF16),
                   jax.ShapeDtypeStruct((T, LANES), F32)],
        compiler_params=pltpu.CompilerParams(vmem_limit_bytes=VMEM_LIMIT),
        name="outproj",
    )(o_mla, o_na, x2d, mod, wo, g, wr, br)


MOE_TT = 1024
MOE_CH = 128
MOE_NBLK = MOE_TT // MOE_CH + (N_GROUPS - 1)
MOE_TTP = MOE_NBLK * MOE_CH


def _moe_sort_kernel(h_ref, comb_ref, xs_out, cs_out, pos_out, cnt_out, tri_scr):
    @pl.when(pl.program_id(0) == 0)
    def _():
        r = lax.broadcasted_iota(jnp.int32, tri_scr.shape, 0)
        c = lax.broadcasted_iota(jnp.int32, tri_scr.shape, 1)
        tri_scr[...] = jnp.where(c < r, 1.0, 0.0).astype(BF16)

    comb = comb_ref[...]
    lane = lax.broadcasted_iota(jnp.int32, comb.shape, 1)
    lane_f = lane.astype(F32)
    gid = jnp.sum(jnp.where(lane == GROUP_LANE, comb, 0.0), axis=1, keepdims=True)
    onehot = jnp.where((lane_f == gid) & (lane < N_GROUPS), 1.0, 0.0)
    ahead = jnp.dot(tri_scr[...], onehot.astype(BF16), preferred_element_type=F32)
    n = jnp.sum(onehot, axis=0, keepdims=True)
    padded = jnp.ceil(n * (1.0 / MOE_CH)) * MOE_CH
    start = jnp.zeros_like(padded)
    for k in range(1, N_GROUPS):
        start = start + jnp.where(lane[:1] >= k, pltpu.roll(padded, k, 1), 0.0)
    pos = jnp.sum(onehot * (ahead + start), axis=1, keepdims=True)
    pos_rep = jnp.broadcast_to(pos, comb.shape)
    pos_out[...] = pos_rep
    cnt_out[0] = jnp.broadcast_to(n, (8, LANES))

    pos_row = jnp.transpose(pos_rep)[0:1, :]
    row = lax.broadcasted_iota(jnp.int32, (MOE_TTP, MOE_TT), 0).astype(F32)
    perm = jnp.where(row == pos_row, 1.0, 0.0).astype(BF16)
    xs_out[...] = jnp.dot(perm, h_ref[...], preferred_element_type=F32).astype(BF16)
    hi = comb.astype(BF16)
    lo = (comb - hi.astype(F32)).astype(BF16)
    cs_out[...] = (jnp.dot(perm, hi, preferred_element_type=F32)
                   + jnp.dot(perm, lo, preferred_element_type=F32))


def _moe_sort(h2, comb):
    T, D = h2.shape
    nt = T // MOE_TT
    return pl.pallas_call(
        _moe_sort_kernel,
        grid=(nt,),
        in_specs=[pl.BlockSpec((MOE_TT, D), lambda t: (t, 0)),
                  pl.BlockSpec((MOE_TT, LANES), lambda t: (t, 0))],
        out_specs=[pl.BlockSpec((MOE_TTP, D), lambda t: (t, 0)),
                   pl.BlockSpec((MOE_TTP, LANES), lambda t: (t, 0)),
                   pl.BlockSpec((MOE_TT, LANES), lambda t: (t, 0)),
                   pl.BlockSpec((1, 8, LANES), lambda t: (t, 0, 0))],
        out_shape=[jax.ShapeDtypeStruct((nt * MOE_TTP, D), BF16),
                   jax.ShapeDtypeStruct((nt * MOE_TTP, LANES), F32),
                   jax.ShapeDtypeStruct((T, LANES), F32),
                   jax.ShapeDtypeStruct((nt, 8, LANES), F32)],
        scratch_shapes=[pltpu.VMEM((MOE_TT, MOE_TT), BF16)],
        compiler_params=pltpu.CompilerParams(vmem_limit_bytes=VMEM_LIMIT),
        name="moe_sort",
    )(h2, comb)


def _moe_experts_kernel(blk_ref, grp_ref, valid_ref, x_ref, c_ref, wg_ref, wu_ref, wd_ref, y_ref):
    s = pl.program_id(0)

    @pl.when(valid_ref[s] == 0)
    def _():
        y_ref[...] = jnp.zeros(y_ref.shape, y_ref.dtype)

    @pl.when(valid_ref[s] != 0)
    def _():
        x = x_ref[...]
        comb = c_ref[...]
        lane = lax.broadcasted_iota(jnp.int32, comb.shape, 1)
        e0 = grp_ref[s] * EXPERTS_PER_GROUP
        y = None
        for e in range(EXPERTS_PER_GROUP):
            gate = jnp.dot(x, wg_ref[e], preferred_element_type=F32)
            up = jnp.dot(x, wu_ref[e], preferred_element_type=F32)
            w = jnp.sum(jnp.where(lane == e0 + e, comb, 0.0), axis=1, keepdims=True)
            hid = (gate * jax.nn.sigmoid(gate) * up * w).astype(BF16)
            part = jnp.dot(hid, wd_ref[e], preferred_element_type=F32)
            y = part if y is None else y + part
        y_ref[...] = y.astype(y_ref.dtype)


def _moe_experts(blk, grp, valid, xs, cs, wg, wu, wd):
    D = xs.shape[1]
    n_slots = blk.shape[0]
    return pl.pallas_call(
        _moe_experts_kernel,
        grid_spec=pltpu.PrefetchScalarGridSpec(
            num_scalar_prefetch=3,
            grid=(n_slots,),
            in_specs=[pl.BlockSpec((MOE_CH, D), lambda s, blk, grp, valid: (blk[s], 0)),
                      pl.BlockSpec((MOE_CH, LANES), lambda s, blk, grp, valid: (blk[s], 0)),
                      pl.BlockSpec((EXPERTS_PER_GROUP, D, D_EXPERT), lambda s, blk, grp, valid: (grp[s], 0, 0)),
                      pl.BlockSpec((EXPERTS_PER_GROUP, D, D_EXPERT), lambda s, blk, grp, valid: (grp[s], 0, 0)),
                      pl.BlockSpec((EXPERTS_PER_GROUP, D_EXPERT, D), lambda s, blk, grp, valid: (grp[s], 0, 0))],
            out_specs=pl.BlockSpec((MOE_CH, D), lambda s, blk, grp, valid: (blk[s], 0)),
        ),
        out_shape=jax.ShapeDtypeStruct(xs.shape, BF16),
        compiler_params=pltpu.CompilerParams(vmem_limit_bytes=VMEM_LIMIT),
        name="moe_experts",
    )(blk, grp, valid, xs, cs, wg, wu, wd)


def _moe_final_kernel(ys_ref, pos_ref, x1_ref, mod_ref, fg_ref, o_ref):
    col = lax.broadcasted_iota(jnp.int32, (MOE_TT, MOE_TTP), 1).astype(F32)
    unperm = jnp.where(col == pltpu.repeat(pos_ref[...], MOE_NBLK, axis=1), 1.0, 0.0).astype(BF16)
    y = jnp.dot(unperm, ys_ref[...], preferred_element_type=F32)
    o_ref[...] = _rms(x1_ref[...] + mod_ref[0, R_G2:R_G2 + 1, :] * y, fg_ref[...])


def _moe_final(ys, pos, x1, mod, fg, n_per_batch):
    T, D = x1.shape
    per_b = n_per_batch // MOE_TT
    return pl.pallas_call(
        _moe_final_kernel,
        grid=(T // MOE_TT,),
        in_specs=[pl.BlockSpec((MOE_TTP, D), lambda t: (t, 0)),
                  pl.BlockSpec((MOE_TT, LANES), lambda t: (t, 0)),
                  pl.BlockSpec((MOE_TT, D), lambda t: (t, 0)),
                  pl.BlockSpec((1, 6, D), lambda t: (t // per_b, 0, 0)),
                  pl.BlockSpec((1, D), lambda t: (0, 0))],
        out_specs=pl.BlockSpec((MOE_TT, D), lambda t: (t, 0)),
        out_shape=jax.ShapeDtypeStruct((T, D), F32),
        compiler_params=pltpu.CompilerParams(vmem_limit_bytes=VMEM_LIMIT),
        name="moe_final",
    )(ys, pos, x1, mod, fg)


def _chunk_schedule(counts):
    nt = counts.shape[0]
    nchunks = jnp.ceil(counts * (1.0 / MOE_CH)).astype(jnp.int32)
    ends = jnp.cumsum(nchunks, axis=1)
    j = jnp.arange(MOE_NBLK, dtype=jnp.int32)
    grp = jnp.sum(j[None, :, None] >= ends[:, None, :], axis=2).astype(jnp.int32)
    valid = grp < N_GROUPS
    grp = jnp.minimum(grp, N_GROUPS - 1).reshape(-1)
    blk = (jnp.arange(nt, dtype=jnp.int32)[:, None] * MOE_NBLK + j[None, :]).reshape(-1)
    order = jnp.argsort(grp, stable=True)
    return blk[order], grp[order], valid.reshape(-1).astype(jnp.int32)[order]


def _rope_tables(n):
    pos = jnp.arange(n)
    half = MLA_ROPE // 2
    inv_freq = ROPE_THETA ** (-jnp.arange(0, half, 2, dtype=F32) / half)
    ang = jnp.concatenate([(pos // GRID_W).astype(F32)[:, None] * inv_freq,
                           (pos % GRID_W).astype(F32)[:, None] * inv_freq], axis=-1)
    cos, sin = jnp.cos(ang), jnp.sin(ang)
    one = jnp.ones((n, MLA_NOPE), F32)
    zero = jnp.zeros((n, MLA_NOPE), F32)
    pad = jnp.zeros((n, LANES - MLA_QK), F32)
    z16 = jnp.zeros((n, half), F32)
    c = jnp.concatenate([one, cos, cos, pad], axis=1)
    sa = jnp.concatenate([zero, -sin, z16, pad], axis=1)
    sb = jnp.concatenate([zero, z16, sin, pad], axis=1)
    ident = jnp.concatenate([jnp.ones((CTX_LEN, MLA_QK), F32), jnp.zeros((CTX_LEN, LANES - MLA_QK), F32)], axis=1)
    zc = jnp.zeros((CTX_LEN, LANES), F32)
    return (jnp.concatenate([ident, c], axis=0), jnp.concatenate([zc, sa], axis=0),
            jnp.concatenate([zc, sb], axis=0))


def kernel(x, c, ctx, c_ctx, w_mod, b_mod, norm_attn_g, norm_ffn_g, w_in, q_a_norm_g, kv_a_norm_g, w_uq, w_ukv,
           na_rel_bias, w_out, w_router_group, b_router_group, w_router_expert, b_router_expert, w_gate, w_up,
           w_down, final_norm_g):
    B, N, D = x.shape
    assert w_mod.shape[0] == 1, "single-layer kernel"
    assert (B, N, D) == (2, 8192, D_MODEL) and ctx.shape == (B, CTX_LEN, D)

    cond8 = jnp.concatenate([c, c_ctx[None, :], jnp.zeros((8 - B - 1, D), F32)], axis=0)
    mod = _adaln(cond8, w_mod[0], b_mod[0]).reshape(8, 6, D)

    wi = w_in[0]
    na_w = NA_HEADS * NA_DIM
    k0 = Q_LORA + na_w + KV_LORA
    win = jnp.concatenate([wi[:, :k0], jnp.zeros((D, MLA_NOPE), F32), wi[:, k0:k0 + MLA_ROPE],
                           jnp.zeros((D, LANES - MLA_QK), F32), wi[:, k0 + MLA_ROPE:]], axis=1).astype(BF16)
    assert win.shape[1] == C_END
    wuq = jnp.pad(w_uq[0].reshape(Q_LORA, MLA_HEADS, MLA_QK), ((0, 0), (0, 0), (0, LANES - MLA_QK)))
    wuq = wuq.reshape(Q_LORA, MLA_HEADS * LANES).astype(BF16)
    wkv = w_ukv[0].reshape(KV_LORA, MLA_HEADS, MLA_NOPE + MLA_V)
    wk = jnp.pad(wkv[:, :, :MLA_NOPE], ((0, 0), (0, 0), (0, LANES - MLA_NOPE))).reshape(KV_LORA, MLA_HEADS * LANES)
    wv = jnp.pad(wkv[:, :, MLA_NOPE:], ((0, 0), (0, 0), (0, LANES - MLA_V))).reshape(KV_LORA, MLA_HEADS * LANES)
    wukv = jnp.concatenate([wk, wv], axis=1).astype(BF16)
    rc, rsa, rsb = _rope_tables(N)

    q_m, k_m, v_m, nq, nk, nv = _proj(x, ctx, mod, norm_attn_g, win, q_a_norm_g, kv_a_norm_g, wuq, wukv,
                                      rc, rsa, rsb)
    o_mla = _mla(q_m, k_m, v_m)
    o_na = _natten(nq, nk, nv, _na_bias(na_rel_bias[0]))

    n_r = N_GROUPS + N_EXPERTS
    wr = jnp.pad(jnp.concatenate([w_router_group[0], w_router_expert[0]], axis=1), ((0, 0), (0, LANES - n_r)))
    br = jnp.pad(jnp.concatenate([b_router_group[0], b_router_expert[0]]), (0, LANES - n_r)).reshape(1, LANES)
    T = B * N
    x1, h2, comb = _outproj(o_mla.reshape(T, -1), o_na.reshape(T, -1), x.reshape(T, D), mod,
                            w_out[0].astype(BF16), norm_ffn_g, wr, br, N)
    xs, cs, pos, counts = _moe_sort(h2, comb)
    blk, grp, valid = _chunk_schedule(counts[:, 0, :N_GROUPS])
    ys = _moe_experts(blk, grp, valid, xs, cs, w_gate[0].astype(BF16), w_up[0].astype(BF16),
                      w_down[0].astype(BF16))
    out = _moe_final(ys, pos, x1, mod, final_norm_g.reshape(1, D), N)
    return out.reshape(B, N, D)
```

```python
import functools

import jax
import jax.numpy as jnp
from jax import lax
from jax.experimental import pallas as pl
from jax.experimental.pallas import tpu as pltpu

D_MODEL = 1024
GRID_W = 64
CTX_LEN = 256
MLA_HEADS = 8
MLA_NOPE = 64
MLA_ROPE = 32
MLA_QK = MLA_NOPE + MLA_ROPE
MLA_V = 64
Q_LORA = 384
KV_LORA = 256
NA_HEADS = 8
NA_DIM = 64
NA_WIN_H = 8
NA_WIN_W = 16
N_GROUPS = 4
EXPERTS_PER_GROUP = 4
N_EXPERTS = N_GROUPS * EXPERTS_PER_GROUP
D_EXPERT = 512
ROPE_THETA = 10000.0
NORM_EPS = 1e-6
MASK_VALUE = -1e30

LANES = 128
HEAD_PAIRS = 4
TOK_BLOCK = 256
ROWS_PER_BLOCK = TOK_BLOCK // GRID_W
VMEM_LIMIT = 48 * 1024 * 1024

F32 = jnp.float32
BF16 = jnp.bfloat16


def _rms(x, g):
    return x * lax.rsqrt(jnp.mean(x * x, axis=-1, keepdims=True) + NORM_EPS) * g


def _adaln_kernel(c_ref, w_ref, b_ref, o_ref):
    c = c_ref[...]
    s = c * jax.nn.sigmoid(c)
    o_ref[...] = jnp.dot(s, w_ref[...], precision=lax.Precision.HIGHEST,
                         preferred_element_type=F32) + b_ref[...]


def _adaln(cond8, w_mod, b_mod):
    n_out = w_mod.shape[1]
    bn = 1024
    return pl.pallas_call(
        _adaln_kernel,
        grid=(n_out // bn,),
        in_specs=[pl.BlockSpec((8, D_MODEL), lambda n: (0, 0)),
                  pl.BlockSpec((D_MODEL, bn), lambda n: (0, n)),
                  pl.BlockSpec((1, bn), lambda n: (0, n))],
        out_specs=pl.BlockSpec((8, bn), lambda n: (0, n)),
        out_shape=jax.ShapeDtypeStruct((8, n_out), F32),
        name="adaln",
    )(cond8, w_mod, b_mod.reshape(1, n_out))


NA_VARIANTS = ((-1, False), (0, False), (1, False), (0, True), (1, True), (-1, True), (0, None))
V_PREV, V_CUR, V_NEXT, V_CUR_ALL, V_NEXT_ALL, V_PREV_ALL, V_NONE = range(7)
N_DR = 2 * NA_WIN_H - 1
N_DC = 2 * NA_WIN_W - 1


def _na_bias_kernel(rel_ref, o_ref):
    h = pl.program_id(0)
    qc = lax.broadcasted_iota(jnp.int32, (GRID_W, TOK_BLOCK), 0)
    kk = lax.broadcasted_iota(jnp.int32, (GRID_W, TOK_BLOCK), 1)
    kc = kk % GRID_W
    kblk = kk // GRID_W
    col_start = jnp.clip(qc - NA_WIN_W // 2, 0, GRID_W - NA_WIN_W)
    col_in = (kc >= col_start) & (kc < col_start + NA_WIN_W)
    dc = kc - qc + (NA_WIN_W - 1)
    tables = []
    for d in range(N_DR):
        t = jnp.full((GRID_W, TOK_BLOCK), MASK_VALUE, F32)
        base = (h * N_DR + d) * N_DC
        for m in range(N_DC):
            t = jnp.where(dc == m, rel_ref[base + m], t)
        tables.append(jnp.where(col_in, t, MASK_VALUE))
    for v, (boff, all_valid) in enumerate(NA_VARIANTS):
        for a in range(ROWS_PER_BLOCK):
            tile = jnp.full((GRID_W, TOK_BLOCK), MASK_VALUE, F32)
            if all_valid is not None:
                for cb in range(ROWS_PER_BLOCK):
                    dr = ROWS_PER_BLOCK * boff + cb - a
                    if all_valid or (-(NA_WIN_H // 2) <= dr < NA_WIN_H - NA_WIN_H // 2):
                        tile = jnp.where(kblk == cb, tables[dr + NA_WIN_H - 1], tile)
            o_ref[v, 0, a * GRID_W:(a + 1) * GRID_W, :] = tile


def _na_bias(rel_bias):
    return pl.pallas_call(
        _na_bias_kernel,
        grid=(NA_HEADS,),
        in_specs=[pl.BlockSpec(memory_space=pltpu.SMEM)],
        out_specs=pl.BlockSpec((len(NA_VARIANTS), 1, TOK_BLOCK, TOK_BLOCK), lambda h: (0, h, 0, 0)),
        out_shape=jax.ShapeDtypeStruct((len(NA_VARIANTS), NA_HEADS, TOK_BLOCK, TOK_BLOCK), F32),
        name="na_bias",
    )(rel_bias.reshape(-1))


C_QLAT = 0
C_NAQ = Q_LORA
C_KVLAT = C_NAQ + NA_HEADS * NA_DIM
C_KROPE = C_KVLAT + KV_LORA
C_NAK = C_KROPE + LANES
C_NAV = C_NAK + NA_HEADS * NA_DIM
C_END = C_NAV + NA_HEADS * NA_DIM
LOG2E = 1.4426950408889634
MLA_SCALE = MLA_QK ** -0.5 * LOG2E
NA_SCALE = NA_DIM ** -0.5
ONES_LANE = MLA_V


def _rope(x, c, sa, sb):
    return x * c + pltpu.roll(x, LANES - MLA_ROPE // 2, 1) * sa + pltpu.roll(x, MLA_ROPE // 2, 1) * sb


def _proj_kernel(x_ref, ctx_ref, mod_ref, g_ref, win_ref, qg_ref, kvg_ref, wuq_ref, wukv_ref,
                 c_ref, sa_ref, sb_ref,
                 q_out, k_out, v_out, nq_out, nk_out, nv_out, h_scr):
    j = pl.program_id(1)
    shift = mod_ref[0, 0:1, :]
    scale = mod_ref[0, 1:2, :]

    def norm_mod(xt):
        h_scr[...] = (_rms(xt, g_ref[...]) * (1.0 + scale) + shift).astype(BF16)

    @pl.when(j == 0)
    def _():
        norm_mod(ctx_ref[0])

    @pl.when(j > 0)
    def _():
        norm_mod(x_ref[0])

    proj = jnp.dot(h_scr[...], win_ref[...], preferred_element_type=F32)
    c, sa, sb = c_ref[...], sa_ref[...], sb_ref[...]

    qn = _rms(proj[:, C_QLAT:C_NAQ], qg_ref[...]).astype(BF16)
    q = jnp.dot(qn, wuq_ref[...], preferred_element_type=F32) * MLA_SCALE
    for h in range(MLA_HEADS):
        q_out[0, h] = _rope(q[:, h * LANES:(h + 1) * LANES], c, sa, sb).astype(BF16)

    kvn = _rms(proj[:, C_KVLAT:C_KROPE], kvg_ref[...]).astype(BF16)
    kv = jnp.dot(kvn, wukv_ref[...], preferred_element_type=F32)
    kr = _rope(proj[:, C_KROPE:C_NAK], c, sa, sb)
    v0 = MLA_HEADS * LANES
    lane = lax.broadcasted_iota(jnp.int32, (TOK_BLOCK, LANES), 1)
    for h in range(MLA_HEADS):
        k_out[0, h] = (kv[:, h * LANES:(h + 1) * LANES] + kr).astype(BF16)
        v = jnp.where(lane == ONES_LANE, 1.0, kv[:, v0 + h * LANES:v0 + (h + 1) * LANES])
        v_out[0, h] = jnp.transpose(v).astype(BF16)
    for p in range(HEAD_PAIRS):
        nq_out[0, p] = (proj[:, C_NAQ + p * LANES:C_NAQ + (p + 1) * LANES] * NA_SCALE).astype(BF16)
        nk_out[0, p] = proj[:, C_NAK + p * LANES:C_NAK + (p + 1) * LANES].astype(BF16)
        nv_out[0, p] = proj[:, C_NAV + p * LANES:C_NAV + (p + 1) * LANES].astype(BF16)


def _proj(x, ctx, mod, g, win, qg, kvg, wuq, wukv, rc, rsa, rsb):
    B, N, D = x.shape
    nb = N // TOK_BLOCK
    ntot = N + CTX_LEN
    full = lambda shape: pl.BlockSpec(shape, lambda b, j: (0,) * len(shape))
    lat = lambda b, j: (b, 0, jnp.maximum(j - 1, 0), 0)
    alltok = lambda b, j: (b, 0, j, 0)
    return pl.pallas_call(
        _proj_kernel,
        grid=(B, nb + 1),
        in_specs=[pl.BlockSpec((1, TOK_BLOCK, D), lambda b, j: (b, jnp.maximum(j - 1, 0), 0)),
                  pl.BlockSpec((1, CTX_LEN, D), lambda b, j: (b, 0, 0)),
                  pl.BlockSpec((1, 6, D), lambda b, j: (jnp.where(j == 0, B, b), 0, 0)),
                  full((1, D)), full(win.shape), full((1, Q_LORA)), full((1, KV_LORA)),
                  full(wuq.shape), full(wukv.shape),
                  pl.BlockSpec((TOK_BLOCK, LANES), lambda b, j: (j, 0)),
                  pl.BlockSpec((TOK_BLOCK, LANES), lambda b, j: (j, 0)),
                  pl.BlockSpec((TOK_BLOCK, LANES), lambda b, j: (j, 0))],
        out_specs=[pl.BlockSpec((1, MLA_HEADS, TOK_BLOCK, LANES), lat),
                   pl.BlockSpec((1, MLA_HEADS, TOK_BLOCK, LANES), alltok),
                   pl.BlockSpec((1, MLA_HEADS, LANES, TOK_BLOCK), lambda b, j: (b, 0, 0, j)),
                   pl.BlockSpec((1, HEAD_PAIRS, TOK_BLOCK, LANES), lat),
                   pl.BlockSpec((1, HEAD_PAIRS, TOK_BLOCK, LANES), alltok),
                   pl.BlockSpec((1, HEAD_PAIRS, TOK_BLOCK, LANES), alltok)],
        out_shape=[jax.ShapeDtypeStruct((B, MLA_HEADS, N, LANES), BF16),
                   jax.ShapeDtypeStruct((B, MLA_HEADS, ntot, LANES), BF16),
                   jax.ShapeDtypeStruct((B, MLA_HEADS, LANES, ntot), BF16),
                   jax.ShapeDtypeStruct((B, HEAD_PAIRS, N, LANES), BF16),
                   jax.ShapeDtypeStruct((B, HEAD_PAIRS, ntot, LANES), BF16),
                   jax.ShapeDtypeStruct((B, HEAD_PAIRS, ntot, LANES), BF16)],
        scratch_shapes=[pltpu.VMEM((TOK_BLOCK, D), BF16)],
        compiler_params=pltpu.CompilerParams(vmem_limit_bytes=VMEM_LIMIT),
        name="proj",
    )(x, ctx, mod, g, win, qg, kvg, wuq, wukv, rc, rsa, rsb)


MLA_TQ = 1024
MLA_TK = 768


def _mla_kernel(q_ref, k_ref, v_ref, o_ref, m_scr, alpha_scr, acc_scr, p_scr, s_scr, smax_scr):
    n_chunks = k_ref.shape[2] // MLA_TK
    m_scr[...] = jnp.full(m_scr.shape, -jnp.inf, F32)
    acc_scr[...] = jnp.zeros(acc_scr.shape, F32)

    def scores(ci, hh):
        off = pl.multiple_of(ci * MLA_TK, MLA_TK)
        k = k_ref[0, hh, pl.ds(off, MLA_TK), :]
        s = lax.dot_general(k, q_ref[0, hh], (((1,), (1,)), ((), ())), preferred_element_type=F32)
        s_scr[hh] = s
        smax_scr[hh] = jnp.max(s, axis=0, keepdims=True)

    def softmax(hh):
        m_prev = m_scr[hh]
        m_next = jnp.maximum(m_prev, smax_scr[hh])
        p_scr[hh] = jnp.exp2((s_scr[hh] - m_next).astype(BF16))
        alpha_scr[hh] = jnp.exp2(m_prev - m_next)
        m_scr[hh] = m_next

    def values(ci, hh):
        off = pl.multiple_of(ci * MLA_TK, MLA_TK)
        vt = v_ref[0, hh, :, pl.ds(off, MLA_TK)]
        acc_scr[hh] = alpha_scr[hh] * acc_scr[hh] + jnp.dot(vt, p_scr[hh], preferred_element_type=F32)

    for hh in range(2):
        scores(0, hh)
    for hh in range(2):
        softmax(hh)
        scores(1, hh)

    def body(ci, carry):
        for hh in range(2):
            values(ci - 1, hh)
            softmax(hh)
            scores(ci + 1, hh)
        return carry

    lax.fori_loop(1, n_chunks - 1, body, 0)
    for hh in range(2):
        values(n_chunks - 2, hh)
        softmax(hh)
    for hh in range(2):
        values(n_chunks - 1, hh)
    outs = []
    for hh in range(2):
        acc = acc_scr[hh]
        outs.append(jnp.transpose(acc / acc[ONES_LANE:ONES_LANE + 1, :]))
    lane = lax.broadcasted_iota(jnp.int32, (MLA_TQ, LANES), 1)
    o_ref[0] = jnp.where(lane < MLA_V, outs[0], pltpu.roll(outs[1], MLA_V, 1)).astype(o_ref.dtype)


def _mla(q_m, k_m, v_m):
    B, _, N, _ = q_m.shape
    ntot = k_m.shape[2]
    return pl.pallas_call(
        _mla_kernel,
        grid=(B, HEAD_PAIRS, N // MLA_TQ),
        in_specs=[pl.BlockSpec((1, 2, MLA_TQ, LANES), lambda b, p, i: (b, p, i, 0)),
                  pl.BlockSpec((1, 2, ntot, LANES), lambda b, p, i: (b, p, 0, 0)),
                  pl.BlockSpec((1, 2, LANES, ntot), lambda b, p, i: (b, p, 0, 0))],
        out_specs=pl.BlockSpec((1, MLA_TQ, LANES), lambda b, p, i: (b, i, p)),
        out_shape=jax.ShapeDtypeStruct((B, N, MLA_HEADS * MLA_V), BF16),
        scratch_shapes=[pltpu.VMEM((2, 1, MLA_TQ), F32), pltpu.VMEM((2, 1, MLA_TQ), F32),
                        pltpu.VMEM((2, LANES, MLA_TQ), F32), pltpu.VMEM((2, MLA_TK, MLA_TQ), BF16),
                        pltpu.VMEM((2, MLA_TK, MLA_TQ), F32), pltpu.VMEM((2, 1, MLA_TQ), F32)],
        compiler_params=pltpu.CompilerParams(vmem_limit_bytes=VMEM_LIMIT),
        name="mla",
    )(q_m, k_m, v_m)


def _na_kernel(q_ref, kc_ref, kp_ref, k0_ref, kn_ref, vc_ref, vp_ref, v0_ref, vn_ref,
               bp_ref, b0_ref, bn_ref, o_ref):
    lane = lax.broadcasted_iota(jnp.int32, (TOK_BLOCK, LANES), 1)
    dn = (((1,), (1,)), ((), ()))
    for p in range(HEAD_PAIRS):
        outs = []
        for half in range(2):
            h = 2 * p + half
            in_half = (lane >= NA_DIM) if half else (lane < NA_DIM)
            q = jnp.where(in_half, q_ref[0, p], jnp.zeros((), BF16))
            s = [lax.dot_general(q, kc_ref[0, p], dn, preferred_element_type=F32),
                 lax.dot_general(q, kp_ref[0, p], dn, preferred_element_type=F32) + bp_ref[0, h],
                 lax.dot_general(q, k0_ref[0, p], dn, preferred_element_type=F32) + b0_ref[0, h],
                 lax.dot_general(q, kn_ref[0, p], dn, preferred_element_type=F32) + bn_ref[0, h]]
            m = jnp.max(jnp.maximum(jnp.maximum(s[0], s[1]), jnp.maximum(s[2], s[3])), axis=1, keepdims=True)
            e = [jnp.exp(t - m) for t in s]
            l = jnp.sum(e[0] + e[1] + e[2] + e[3], axis=1, keepdims=True)
            o = jnp.dot(e[0].astype(BF16), vc_ref[0, p], preferred_element_type=F32)
            for t, v_ref in zip(e[1:], (vp_ref, v0_ref, vn_ref)):
                o = o + jnp.dot(t.astype(BF16), v_ref[0, p], preferred_element_type=F32)
            outs.append(o / l)
        o_ref[0, :, p * LANES:(p + 1) * LANES] = jnp.where(lane < NA_DIM, outs[0], outs[1]).astype(o_ref.dtype)


def _natten(nq, nk, nv, bias):
    B, _, N, _ = nq.shape
    nb = N // TOK_BLOCK
    last = nb - 1
    blk = (1, HEAD_PAIRS, TOK_BLOCK, LANES)
    ctx_map = lambda b, i: (b, 0, 0, 0)
    prev_map = lambda b, i: (b, 0, jnp.maximum(i - 1, 0) + 1, 0)
    cur_map = lambda b, i: (b, 0, i + 1, 0)
    next_map = lambda b, i: (b, 0, jnp.minimum(i + 1, last) + 1, 0)
    bblk = (1, NA_HEADS, TOK_BLOCK, TOK_BLOCK)
    bprev = lambda b, i: (jnp.where(i == 0, V_NONE, jnp.where(i == last, V_PREV_ALL, V_PREV)), 0, 0, 0)
    bcur = lambda b, i: (jnp.where((i == 0) | (i == last), V_CUR_ALL, V_CUR), 0, 0, 0)
    bnext = lambda b, i: (jnp.where(i == 0, V_NEXT_ALL, jnp.where(i == last, V_NONE, V_NEXT)), 0, 0, 0)
    return pl.pallas_call(
        _na_kernel,
        grid=(B, nb),
        in_specs=[pl.BlockSpec(blk, lambda b, i: (b, 0, i, 0)),
                  pl.BlockSpec(blk, ctx_map), pl.BlockSpec(blk, prev_map),
                  pl.BlockSpec(blk, cur_map), pl.BlockSpec(blk, next_map),
                  pl.BlockSpec(blk, ctx_map), pl.BlockSpec(blk, prev_map),
                  pl.BlockSpec(blk, cur_map), pl.BlockSpec(blk, next_map),
                  pl.BlockSpec(bblk, bprev), pl.BlockSpec(bblk, bcur), pl.BlockSpec(bblk, bnext)],
        out_specs=pl.BlockSpec((1, TOK_BLOCK, NA_HEADS * NA_DIM), lambda b, i: (b, i, 0)),
        out_shape=jax.ShapeDtypeStruct((B, N, NA_HEADS * NA_DIM), BF16),
        compiler_params=pltpu.CompilerParams(vmem_limit_bytes=VMEM_LIMIT),
        name="natten",
    )(nq, nk, nk, nk, nk, nv, nv, nv, nv, bias, bias, bias)


OUT_TM = 512
R_G1, R_SH2, R_SC2, R_G2 = 2, 3, 4, 5
GROUP_LANE = N_EXPERTS


def _first_max_lane(x, valid, lane):
    xm = jnp.where(valid, x, -jnp.inf)
    mx = jnp.max(xm, axis=1, keepdims=True)
    idx = jnp.min(jnp.where(valid & (xm == mx), lane, LANES), axis=1, keepdims=True)
    return mx, idx


def _outproj_kernel(om_ref, on_ref, x_ref, mod_ref, wo_ref, g_ref, wr_ref, br_ref,
                    x1_out, h2_out, comb_out):
    half = wo_ref.shape[0] // 2
    mixed = (jnp.dot(om_ref[...], wo_ref[:half, :], preferred_element_type=F32)
             + jnp.dot(on_ref[...], wo_ref[half:, :], preferred_element_type=F32))
    x1 = x_ref[...] + mod_ref[0, R_G1:R_G1 + 1, :] * mixed
    x1_out[...] = x1
    h2 = _rms(x1, g_ref[...]) * (1.0 + mod_ref[0, R_SC2:R_SC2 + 1, :]) + mod_ref[0, R_SH2:R_SH2 + 1, :]
    h2_out[...] = h2.astype(BF16)

    logits = jnp.dot(h2, wr_ref[...], precision=lax.Precision.HIGHEST,
                     preferred_element_type=F32) + br_ref[...]
    lane = lax.broadcasted_iota(jnp.int32, logits.shape, 1)
    is_g = lane < N_GROUPS
    g_max, g_sel = _first_max_lane(logits, is_g, lane)
    g_w = 1.0 / jnp.sum(jnp.where(is_g, jnp.exp(logits - g_max), 0.0), axis=1, keepdims=True)
    e_lo = N_GROUPS + g_sel * EXPERTS_PER_GROUP
    in_grp = (lane >= e_lo) & (lane < e_lo + EXPERTS_PER_GROUP)
    m1, i1 = _first_max_lane(logits, in_grp, lane)
    m2, i2 = _first_max_lane(logits, in_grp & (lane != i1), lane)
    e2 = jnp.exp(m2 - m1)
    w1 = 1.0 / (1.0 + e2)
    w2 = e2 / (1.0 + e2)
    comb = jnp.where(lane == i1, g_w * w1, jnp.where(lane == i2, g_w * w2, 0.0))
    comb_out[...] = jnp.where(lane == GROUP_LANE, g_sel.astype(F32), pltpu.roll(comb, LANES - N_GROUPS, 1))


def _outproj(o_mla, o_na, x2d, mod, wo, g, wr, br, n_per_batch):
    T, D = x2d.shape
    per_b = n_per_batch // OUT_TM
    tok = lambda w: pl.BlockSpec((OUT_TM, w), lambda t: (t, 0))
    full = lambda shape: pl.BlockSpec(shape, lambda t: (0,) * len(shape))
    return pl.pallas_call(
        _outproj_kernel,
        grid=(T // OUT_TM,),
        in_specs=[tok(o_mla.shape[1]), tok(o_na.shape[1]), tok(D),
                  pl.BlockSpec((1, 6, D), lambda t: (t // per_b, 0, 0)),
                  full(wo.shape), full((1, D)), full(wr.shape), full(br.shape)],
        out_specs=[tok(D), tok(D), tok(LANES)],
        out_shape=[jax.ShapeDtypeStruct((T, D), F32), jax.ShapeDtypeStruct((T, D), BF16),
                   jax.ShapeDtypeStruct((T, LANES), F32)],
        compiler_params=pltpu.CompilerParams(vmem_limit_bytes=VMEM_LIMIT),
        name="outproj",
    )(o_mla, o_na, x2d, mod, wo, g, wr, br)


MOE_TT = 1024
MOE_CH = 128
MOE_NBLK = MOE_TT // MOE_CH + (N_GROUPS - 1)
MOE_TTP = MOE_NBLK * MOE_CH


def _moe_sort_kernel(h_ref, comb_ref, xs_out, cs_out, pos_out, cnt_out, tri_scr):
    @pl.when(pl.program_id(0) == 0)
    def _():
        r = lax.broadcasted_iota(jnp.int32, tri_scr.shape, 0)
        c = lax.broadcasted_iota(jnp.int32, tri_scr.shape, 1)
        tri_scr[...] = jnp.where(c < r, 1.0, 0.0).astype(BF16)

    comb = comb_ref[...]
    lane = lax.broadcasted_iota(jnp.int32, comb.shape, 1)
    lane_f = lane.astype(F32)
    gid = jnp.sum(jnp.where(lane == GROUP_LANE, comb, 0.0), axis=1, keepdims=True)
    onehot = jnp.where((lane_f == gid) & (lane < N_GROUPS), 1.0, 0.0)
    ahead = jnp.dot(tri_scr[...], onehot.astype(BF16), preferred_element_type=F32)
    n = jnp.sum(onehot, axis=0, keepdims=True)
    padded = jnp.ceil(n * (1.0 / MOE_CH)) * MOE_CH
    start = jnp.zeros_like(padded)
    for k in range(1, N_GROUPS):
        start = start + jnp.where(lane[:1] >= k, pltpu.roll(padded, k, 1), 0.0)
    pos = jnp.sum(onehot * (ahead + start), axis=1, keepdims=True)
    pos_rep = jnp.broadcast_to(pos, comb.shape)
    pos_out[...] = pos_rep
    cnt_out[0] = jnp.broadcast_to(n, (8, LANES))

    pos_row = jnp.transpose(pos_rep)[0:1, :]
    row = lax.broadcasted_iota(jnp.int32, (MOE_TTP, MOE_TT), 0).astype(F32)
    perm = jnp.where(row == pos_row, 1.0, 0.0).astype(BF16)
    xs_out[...] = jnp.dot(perm, h_ref[...], preferred_element_type=F32).astype(BF16)
    hi = comb.astype(BF16)
    lo = (comb - hi.astype(F32)).astype(BF16)
    cs_out[...] = (jnp.dot(perm, hi, preferred_element_type=F32)
                   + jnp.dot(perm, lo, preferred_element_type=F32))


def _moe_sort(h2, comb):
    T, D = h2.shape
    nt = T // MOE_TT
    return pl.pallas_call(
        _moe_sort_kernel,
        grid=(nt,),
        in_specs=[pl.BlockSpec((MOE_TT, D), lambda t: (t, 0)),
                  pl.BlockSpec((MOE_TT, LANES), lambda t: (t, 0))],
        out_specs=[pl.BlockSpec((MOE_TTP, D), lambda t: (t, 0)),
                   pl.BlockSpec((MOE_TTP, LANES), lambda t: (t, 0)),
                   pl.BlockSpec((MOE_TT, LANES), lambda t: (t, 0)),
                   pl.BlockSpec((1, 8, LANES), lambda t: (t, 0, 0))],
        out_shape=[jax.ShapeDtypeStruct((nt * MOE_TTP, D), BF16),
                   jax.ShapeDtypeStruct((nt * MOE_TTP, LANES), F32),
                   jax.ShapeDtypeStruct((T, LANES), F32),
                   jax.ShapeDtypeStruct((nt, 8, LANES), F32)],
        scratch_shapes=[pltpu.VMEM((MOE_TT, MOE_TT), BF16)],
        compiler_params=pltpu.CompilerParams(vmem_limit_bytes=VMEM_LIMIT),
        name="moe_sort",
    )(h2, comb)


def _moe_experts_kernel(blk_ref, grp_ref, valid_ref, x_ref, c_ref, wg_ref, wu_ref, wd_ref, y_ref):
    s = pl.program_id(0)

    @pl.when(valid_ref[s] == 0)
    def _():
        y_ref[...] = jnp.zeros(y_ref.shape, y_ref.dtype)

    @pl.when(valid_ref[s] != 0)
    def _():
        x = x_ref[...]
        comb = c_ref[...]
        lane = lax.broadcasted_iota(jnp.int32, comb.shape, 1)
        e0 = grp_ref[s] * EXPERTS_PER_GROUP
        y = None
        for e in range(EXPERTS_PER_GROUP):
            gate = jnp.dot(x, wg_ref[e], preferred_element_type=F32)
            up = jnp.dot(x, wu_ref[e], preferred_element_type=F32)
            w = jnp.sum(jnp.where(lane == e0 + e, comb, 0.0), axis=1, keepdims=True)
            hid = (gate * jax.nn.sigmoid(gate) * up * w).astype(BF16)
            part = jnp.dot(hid, wd_ref[e], preferred_element_type=F32)
            y = part if y is None else y + part
        y_ref[...] = y.astype(y_ref.dtype)


def _moe_experts(blk, grp, valid, xs, cs, wg, wu, wd):
    D = xs.shape[1]
    n_slots = blk.shape[0]
    return pl.pallas_call(
        _moe_experts_kernel,
        grid_spec=pltpu.PrefetchScalarGridSpec(
            num_scalar_prefetch=3,
            grid=(n_slots,),
            in_specs=[pl.BlockSpec((MOE_CH, D), lambda s, blk, grp, valid: (blk[s], 0)),
                      pl.BlockSpec((MOE_CH, LANES), lambda s, blk, grp, valid: (blk[s], 0)),
                      pl.BlockSpec((EXPERTS_PER_GROUP, D, D_EXPERT), lambda s, blk, grp, valid: (grp[s], 0, 0)),
                      pl.BlockSpec((EXPERTS_PER_GROUP, D, D_EXPERT), lambda s, blk, grp, valid: (grp[s], 0, 0)),
                      pl.BlockSpec((EXPERTS_PER_GROUP, D_EXPERT, D), lambda s, blk, grp, valid: (grp[s], 0, 0))],
            out_specs=pl.BlockSpec((MOE_CH, D), lambda s, blk, grp, valid: (blk[s], 0)),
        ),
        out_shape=jax.ShapeDtypeStruct(xs.shape, BF16),
        compiler_params=pltpu.CompilerParams(vmem_limit_bytes=VMEM_LIMIT),
        name="moe_experts",
    )(blk, grp, valid, xs, cs, wg, wu, wd)


def _moe_final_kernel(ys_ref, pos_ref, x1_ref, mod_ref, fg_ref, o_ref):
    col = lax.broadcasted_iota(jnp.int32, (MOE_TT, MOE_TTP), 1).astype(F32)
    unperm = jnp.where(col == pltpu.repeat(pos_ref[...], MOE_NBLK, axis=1), 1.0, 0.0).astype(BF16)
    y = jnp.dot(unperm, ys_ref[...], preferred_element_type=F32)
    o_ref[...] = _rms(x1_ref[...] + mod_ref[0, R_G2:R_G2 + 1, :] * y, fg_ref[...])


def _moe_final(ys, pos, x1, mod, fg, n_per_batch):
    T, D = x1.shape
    per_b = n_per_batch // MOE_TT
    return pl.pallas_call(
        _moe_final_kernel,
        grid=(T // MOE_TT,),
        in_specs=[pl.BlockSpec((MOE_TTP, D), lambda t: (t, 0)),
                  pl.BlockSpec((MOE_TT, LANES), lambda t: (t, 0)),
                  pl.BlockSpec((MOE_TT, D), lambda t: (t, 0)),
                  pl.BlockSpec((1, 6, D), lambda t: (t // per_b, 0, 0)),
                  pl.BlockSpec((1, D), lambda t: (0, 0))],
        out_specs=pl.BlockSpec((MOE_TT, D), lambda t: (t, 0)),
        out_shape=jax.ShapeDtypeStruct((T, D), F32),
        compiler_params=pltpu.CompilerParams(vmem_limit_bytes=VMEM_LIMIT),
        name="moe_final",
    )(ys, pos, x1, mod, fg)


def _chunk_schedule(counts):
    nt = counts.shape[0]
    nchunks = jnp.ceil(counts * (1.0 / MOE_CH)).astype(jnp.int32)
    ends = jnp.cumsum(nchunks, axis=1)
    j = jnp.arange(MOE_NBLK, dtype=jnp.int32)
    grp = jnp.sum(j[None, :, None] >= ends[:, None, :], axis=2).astype(jnp.int32)
    valid = grp < N_GROUPS
    grp = jnp.minimum(grp, N_GROUPS - 1).reshape(-1)
    blk = (jnp.arange(nt, dtype=jnp.int32)[:, None] * MOE_NBLK + j[None, :]).reshape(-1)
    order = jnp.argsort(grp, stable=True)
    return blk[order], grp[order], valid.reshape(-1).astype(jnp.int32)[order]


def _rope_tables(n):
    pos = jnp.arange(n)
    half = MLA_ROPE // 2
    inv_freq = ROPE_THETA ** (-jnp.arange(0, half, 2, dtype=F32) / half)
    ang = jnp.concatenate([(pos // GRID_W).astype(F32)[:, None] * inv_freq,
                           (pos % GRID_W).astype(F32)[:, None] * inv_freq], axis=-1)
    cos, sin = jnp.cos(ang), jnp.sin(ang)
    one = jnp.ones((n, MLA_NOPE), F32)
    zero = jnp.zeros((n, MLA_NOPE), F32)
    pad = jnp.zeros((n, LANES - MLA_QK), F32)
    z16 = jnp.zeros((n, half), F32)
    c = jnp.concatenate([one, cos, cos, pad], axis=1)
    sa = jnp.concatenate([zero, -sin, z16, pad], axis=1)
    sb = jnp.concatenate([zero, z16, sin, pad], axis=1)
    ident = jnp.concatenate([jnp.ones((CTX_LEN, MLA_QK), F32), jnp.zeros((CTX_LEN, LANES - MLA_QK), F32)], axis=1)
    zc = jnp.zeros((CTX_LEN, LANES), F32)
    return (jnp.concatenate([ident, c], axis=0), jnp.concatenate([zc, sa], axis=0),
            jnp.concatenate([zc, sb], axis=0))


def kernel(x, c, ctx, c_ctx, w_mod, b_mod, norm_attn_g, norm_ffn_g, w_in, q_a_norm_g, kv_a_norm_g, w_uq, w_ukv,
           na_rel_bias, w_out, w_router_group, b_router_group, w_router_expert, b_router_expert, w_gate, w_up,
           w_down, final_norm_g):
    B, N, D = x.shape
    assert w_mod.shape[0] == 1, "single-layer kernel"
    assert (B, N, D) == (2, 8192, D_MODEL) and ctx.shape == (B, CTX_LEN, D)

    cond8 = jnp.concatenate([c, c_ctx[None, :], jnp.zeros((8 - B - 1, D), F32)], axis=0)
    mod = _adaln(cond8, w_mod[0], b_mod[0]).reshape(8, 6, D)

    wi = w_in[0]
    na_w = NA_HEADS * NA_DIM
    k0 = Q_LORA + na_w + KV_LORA
    win = jnp.concatenate([wi[:, :k0], jnp.zeros((D, MLA_NOPE), F32), wi[:, k0:k0 + MLA_ROPE],
                           jnp.zeros((D, LANES - MLA_QK), F32), wi[:, k0 + MLA_ROPE:]], axis=1).astype(BF16)
    assert win.shape[1] == C_END
    wuq = jnp.pad(w_uq[0].reshape(Q_LORA, MLA_HEADS, MLA_QK), ((0, 0), (0, 0), (0, LANES - MLA_QK)))
    wuq = wuq.reshape(Q_LORA, MLA_HEADS * LANES).astype(BF16)
    wkv = w_ukv[0].reshape(KV_LORA, MLA_HEADS, MLA_NOPE + MLA_V)
    wk = jnp.pad(wkv[:, :, :MLA_NOPE], ((0, 0), (0, 0), (0, LANES - MLA_NOPE))).reshape(KV_LORA, MLA_HEADS * LANES)
    wv = jnp.pad(wkv[:, :, MLA_NOPE:], ((0, 0), (0, 0), (0, LANES - MLA_V))).reshape(KV_LORA, MLA_HEADS * LANES)
    wukv = jnp.concatenate([wk, wv], axis=1).astype(BF16)
    rc, rsa, rsb = _rope_tables(N)

    q_m, k_m, v_m, nq, nk, nv = _proj(x, ctx, mod, norm_attn_g, win, q_a_norm_g, kv_a_norm_g, wuq, wukv,
                                      rc, rsa, rsb)
    o_mla = _mla(q_m, k_m, v_m)
    o_na = _natten(nq, nk, nv, _na_bias(na_rel_bias[0]))

    n_r = N_GROUPS + N_EXPERTS
    wr = jnp.pad(jnp.concatenate([w_router_group[0], w_router_expert[0]], axis=1), ((0, 0), (0, LANES - n_r)))
    br = jnp.pad(jnp.concatenate([b_router_group[0], b_router_expert[0]]), (0, LANES - n_r)).reshape(1, LANES)
    T = B * N
    x1, h2, comb = _outproj(o_mla.reshape(T, -1), o_na.reshape(T, -1), x.reshape(T, D), mod,
                            w_out[0].astype(BF16), norm_ffn_g, wr, br, N)
    xs, cs, pos, counts = _moe_sort(h2, comb)
    blk, grp, valid = _chunk_schedule(counts[:, 0, :N_GROUPS])
    ys = _moe_experts(blk, grp, valid, xs, cs, w_gate[0].astype(BF16), w_up[0].astype(BF16),
                      w_down[0].astype(BF16))
    out = _moe_final(ys, pos, x1, mod, final_norm_g.reshape(1, D), N)
    return out.reshape(B, N, D)
```

```python
import functools

import jax
import jax.numpy as jnp
from jax import lax
from jax.experimental import pallas as pl
from jax.experimental.pallas import tpu as pltpu

D_MODEL = 1024
GRID_W = 64
CTX_LEN = 256
MLA_HEADS = 8
MLA_NOPE = 64
MLA_ROPE = 32
MLA_QK = MLA_NOPE + MLA_ROPE
MLA_V = 64
Q_LORA = 384
KV_LORA = 256
NA_HEADS = 8
NA_DIM = 64
NA_WIN_H = 8
NA_WIN_W = 16
N_GROUPS = 4
EXPERTS_PER_GROUP = 4
N_EXPERTS = N_GROUPS * EXPERTS_PER_GROUP
D_EXPERT = 512
ROPE_THETA = 10000.0
NORM_EPS = 1e-6
MASK_VALUE = -1e30
LOG2E = 1.4426950408889634

LANES = 128
HEAD_PAIRS = 4
TOK_BLOCK = 256
ROWS_PER_BLOCK = TOK_BLOCK // GRID_W
VMEM_LIMIT = 48 * 1024 * 1024

F32 = jnp.float32
BF16 = jnp.bfloat16


def _rms(x, g):
    return x * lax.rsqrt(jnp.mean(x * x, axis=-1, keepdims=True) + NORM_EPS) * g


def _adaln_kernel(c_ref, w_ref, b_ref, o_ref):
    c = c_ref[...]
    s = c * jax.nn.sigmoid(c)
    o_ref[...] = jnp.dot(s, w_ref[...], precision=lax.Precision.HIGHEST,
                         preferred_element_type=F32) + b_ref[...]


def _adaln(cond8, w_mod, b_mod):
    n_out = w_mod.shape[1]
    bn = 1024
    return pl.pallas_call(
        _adaln_kernel,
        grid=(n_out // bn,),
        in_specs=[pl.BlockSpec((8, D_MODEL), lambda n: (0, 0)),
                  pl.BlockSpec((D_MODEL, bn), lambda n: (0, n)),
                  pl.BlockSpec((1, bn), lambda n: (0, n))],
        out_specs=pl.BlockSpec((8, bn), lambda n: (0, n)),
        out_shape=jax.ShapeDtypeStruct((8, n_out), F32),
        name="adaln",
    )(cond8, w_mod, b_mod.reshape(1, n_out))


NA_VARIANTS = ((-1, False), (0, False), (1, False), (0, True), (1, True), (-1, True), (0, None))
V_PREV, V_CUR, V_NEXT, V_CUR_ALL, V_NEXT_ALL, V_PREV_ALL, V_NONE = range(7)
N_DR = 2 * NA_WIN_H - 1
N_DC = 2 * NA_WIN_W - 1


def _na_bias_kernel(rel_ref, o_ref):
    h = pl.program_id(0)
    kc = lax.broadcasted_iota(jnp.int32, (GRID_W, TOK_BLOCK), 0)
    qq = lax.broadcasted_iota(jnp.int32, (GRID_W, TOK_BLOCK), 1)
    qc = qq % GRID_W
    qblk = qq // GRID_W
    col_start = jnp.clip(qc - NA_WIN_W // 2, 0, GRID_W - NA_WIN_W)
    col_in = (kc >= col_start) & (kc < col_start + NA_WIN_W)
    dc = kc - qc + (NA_WIN_W - 1)
    tables = []
    for d in range(N_DR):
        t = jnp.full((GRID_W, TOK_BLOCK), MASK_VALUE, F32)
        base = (h * N_DR + d) * N_DC
        for m in range(N_DC):
            t = jnp.where(dc == m, rel_ref[base + m] * LOG2E, t)
        tables.append(jnp.where(col_in, t, MASK_VALUE))
    for v, (boff, all_valid) in enumerate(NA_VARIANTS):
        for cb in range(ROWS_PER_BLOCK):
            tile = jnp.full((GRID_W, TOK_BLOCK), MASK_VALUE, F32)
            if all_valid is not None:
                for a in range(ROWS_PER_BLOCK):
                    dr = ROWS_PER_BLOCK * boff + cb - a
                    if all_valid or (-(NA_WIN_H // 2) <= dr < NA_WIN_H - NA_WIN_H // 2):
                        tile = jnp.where(qblk == a, tables[dr + NA_WIN_H - 1], tile)
            o_ref[v, 0, cb * GRID_W:(cb + 1) * GRID_W, :] = tile


def _na_bias(rel_bias):
    return pl.pallas_call(
        _na_bias_kernel,
        grid=(NA_HEADS,),
        in_specs=[pl.BlockSpec(memory_space=pltpu.SMEM)],
        out_specs=pl.BlockSpec((len(NA_VARIANTS), 1, TOK_BLOCK, TOK_BLOCK), lambda h: (0, h, 0, 0)),
        out_shape=jax.ShapeDtypeStruct((len(NA_VARIANTS), NA_HEADS, TOK_BLOCK, TOK_BLOCK), F32),
        name="na_bias",
    )(rel_bias.reshape(-1))


C_QLAT = 0
C_NAQ = Q_LORA
C_KVLAT = C_NAQ + NA_HEADS * NA_DIM
C_KROPE = C_KVLAT + KV_LORA
C_NAK = C_KROPE + LANES
C_NAV = C_NAK + NA_HEADS * NA_DIM
C_END = C_NAV + NA_HEADS * NA_DIM
MLA_SCALE = MLA_QK ** -0.5 * LOG2E
NA_SCALE = NA_DIM ** -0.5 * LOG2E
ONES_LANE = MLA_V


def _rope(x, c, sa, sb):
    return x * c + pltpu.roll(x, LANES - MLA_ROPE // 2, 1) * sa + pltpu.roll(x, MLA_ROPE // 2, 1) * sb


def _proj_kernel(x_ref, ctx_ref, mod_ref, g_ref, win_ref, qg_ref, kvg_ref, wuq_ref, wukv_ref,
                 c_ref, sa_ref, sb_ref,
                 q_out, k_out, v_out, nq_out, nk_out, nv_out, h_scr):
    j = pl.program_id(1)
    shift = mod_ref[0, 0:1, :]
    scale = mod_ref[0, 1:2, :]

    def norm_mod(xt):
        h_scr[...] = (_rms(xt, g_ref[...]) * (1.0 + scale) + shift).astype(BF16)

    @pl.when(j == 0)
    def _():
        norm_mod(ctx_ref[0])

    @pl.when(j > 0)
    def _():
        norm_mod(x_ref[0])

    proj = jnp.dot(h_scr[...], win_ref[...], preferred_element_type=F32)
    c, sa, sb = c_ref[...], sa_ref[...], sb_ref[...]

    qn = _rms(proj[:, C_QLAT:C_NAQ], qg_ref[...]).astype(BF16)
    q = jnp.dot(qn, wuq_ref[...], preferred_element_type=F32) * MLA_SCALE
    for h in range(MLA_HEADS):
        q_out[0, h] = _rope(q[:, h * LANES:(h + 1) * LANES], c, sa, sb).astype(BF16)

    kvn = _rms(proj[:, C_KVLAT:C_KROPE], kvg_ref[...]).astype(BF16)
    kv = jnp.dot(kvn, wukv_ref[...], preferred_element_type=F32)
    kr = _rope(proj[:, C_KROPE:C_NAK], c, sa, sb)
    v0 = MLA_HEADS * LANES
    lane = lax.broadcasted_iota(jnp.int32, (TOK_BLOCK, LANES), 1)
    for h in range(MLA_HEADS):
        k_out[0, h] = (kv[:, h * LANES:(h + 1) * LANES] + kr).astype(BF16)
        v = jnp.where(lane == ONES_LANE, 1.0, kv[:, v0 + h * LANES:v0 + (h + 1) * LANES])
        v_out[0, h] = jnp.transpose(v).astype(BF16)
    for p in range(HEAD_PAIRS):
        nq_out[0, p] = (proj[:, C_NAQ + p * LANES:C_NAQ + (p + 1) * LANES] * NA_SCALE).astype(BF16)
        nk_out[0, p] = proj[:, C_NAK + p * LANES:C_NAK + (p + 1) * LANES].astype(BF16)
        nv_pair = proj[:, C_NAV + p * LANES:C_NAV + (p + 1) * LANES]
        for half in range(2):
            nv = pltpu.roll(nv_pair, NA_DIM, 1) if half else nv_pair
            nv = jnp.where(lane < NA_DIM, nv, jnp.where(lane == ONES_LANE, 1.0, 0.0))
            nv_out[0, 2 * p + half] = jnp.transpose(nv).astype(BF16)


def _proj(x, ctx, mod, g, win, qg, kvg, wuq, wukv, rc, rsa, rsb):
    B, N, D = x.shape
    nb = N // TOK_BLOCK
    ntot = N + CTX_LEN
    full = lambda shape: pl.BlockSpec(shape, lambda b, j: (0,) * len(shape))
    lat = lambda b, j: (b, 0, jnp.maximum(j - 1, 0), 0)
    alltok = lambda b, j: (b, 0, j, 0)
    return pl.pallas_call(
        _proj_kernel,
        grid=(B, nb + 1),
        in_specs=[pl.BlockSpec((1, TOK_BLOCK, D), lambda b, j: (b, jnp.maximum(j - 1, 0), 0)),
                  pl.BlockSpec((1, CTX_LEN, D), lambda b, j: (b, 0, 0)),
                  pl.BlockSpec((1, 6, D), lambda b, j: (jnp.where(j == 0, B, b), 0, 0)),
                  full((1, D)), full(win.shape), full((1, Q_LORA)), full((1, KV_LORA)),
                  full(wuq.shape), full(wukv.shape),
                  pl.BlockSpec((TOK_BLOCK, LANES), lambda b, j: (j, 0)),
                  pl.BlockSpec((TOK_BLOCK, LANES), lambda b, j: (j, 0)),
                  pl.BlockSpec((TOK_BLOCK, LANES), lambda b, j: (j, 0))],
        out_specs=[pl.BlockSpec((1, MLA_HEADS, TOK_BLOCK, LANES), lat),
                   pl.BlockSpec((1, MLA_HEADS, TOK_BLOCK, LANES), alltok),
                   pl.BlockSpec((1, MLA_HEADS, LANES, TOK_BLOCK), lambda b, j: (b, 0, 0, j)),
                   pl.BlockSpec((1, HEAD_PAIRS, TOK_BLOCK, LANES), lat),
                   pl.BlockSpec((1, HEAD_PAIRS, TOK_BLOCK, LANES), alltok),
                   pl.BlockSpec((1, NA_HEADS, LANES, TOK_BLOCK), lambda b, j: (b, 0, 0, j))],
        out_shape=[jax.ShapeDtypeStruct((B, MLA_HEADS, N, LANES), BF16),
                   jax.ShapeDtypeStruct((B, MLA_HEADS, ntot, LANES), BF16),
                   jax.ShapeDtypeStruct((B, MLA_HEADS, LANES, ntot), BF16),
                   jax.ShapeDtypeStruct((B, HEAD_PAIRS, N, LANES), BF16),
                   jax.ShapeDtypeStruct((B, HEAD_PAIRS, ntot, LANES), BF16),
                   jax.ShapeDtypeStruct((B, NA_HEADS, LANES, ntot), BF16)],
        scratch_shapes=[pltpu.VMEM((TOK_BLOCK, D), BF16)],
        compiler_params=pltpu.CompilerParams(vmem_limit_bytes=VMEM_LIMIT),
        name="proj",
    )(x, ctx, mod, g, win, qg, kvg, wuq, wukv, rc, rsa, rsb)


MLA_TQ = 1024
MLA_TK = 768


def _mla_kernel(q_ref, k_ref, v_ref, o_ref, m_scr, alpha_scr, acc_scr, p_scr, s_scr, smax_scr):
    n_chunks = k_ref.shape[2] // MLA_TK
    m_scr[...] = jnp.full(m_scr.shape, -jnp.inf, F32)
    acc_scr[...] = jnp.zeros(acc_scr.shape, F32)

    def scores(ci, hh):
        off = pl.multiple_of(ci * MLA_TK, MLA_TK)
        k = k_ref[0, hh, pl.ds(off, MLA_TK), :]
        s = lax.dot_general(k, q_ref[0, hh], (((1,), (1,)), ((), ())), preferred_element_type=F32)
        s_scr[hh] = s
        smax_scr[hh] = jnp.max(s, axis=0, keepdims=True)

    def softmax(hh):
        m_prev = m_scr[hh]
        m_next = jnp.maximum(m_prev, smax_scr[hh])
        p_scr[hh] = jnp.exp2((s_scr[hh] - m_next).astype(BF16))
        alpha_scr[hh] = jnp.exp2(m_prev - m_next)
        m_scr[hh] = m_next

    def values(ci, hh):
        off = pl.multiple_of(ci * MLA_TK, MLA_TK)
        vt = v_ref[0, hh, :, pl.ds(off, MLA_TK)]
        acc_scr[hh] = alpha_scr[hh] * acc_scr[hh] + jnp.dot(vt, p_scr[hh], preferred_element_type=F32)

    for hh in range(2):
        scores(0, hh)
    for hh in range(2):
        softmax(hh)
        scores(1, hh)

    def body(ci, carry):
        for hh in range(2):
            values(ci - 1, hh)
            softmax(hh)
            scores(ci + 1, hh)
        return carry

    lax.fori_loop(1, n_chunks - 1, body, 0)
    for hh in range(2):
        values(n_chunks - 2, hh)
        softmax(hh)
    for hh in range(2):
        values(n_chunks - 1, hh)
    outs = []
    for hh in range(2):
        acc = acc_scr[hh]
        outs.append(jnp.transpose(acc / acc[ONES_LANE:ONES_LANE + 1, :]))
    lane = lax.broadcasted_iota(jnp.int32, (MLA_TQ, LANES), 1)
    o_ref[0] = jnp.where(lane < MLA_V, outs[0], pltpu.roll(outs[1], MLA_V, 1)).astype(o_ref.dtype)


def _mla(q_m, k_m, v_m):
    B, _, N, _ = q_m.shape
    ntot = k_m.shape[2]
    return pl.pallas_call(
        _mla_kernel,
        grid=(B, HEAD_PAIRS, N // MLA_TQ),
        in_specs=[pl.BlockSpec((1, 2, MLA_TQ, LANES), lambda b, p, i: (b, p, i, 0)),
                  pl.BlockSpec((1, 2, ntot, LANES), lambda b, p, i: (b, p, 0, 0)),
                  pl.BlockSpec((1, 2, LANES, ntot), lambda b, p, i: (b, p, 0, 0))],
        out_specs=pl.BlockSpec((1, MLA_TQ, LANES), lambda b, p, i: (b, i, p)),
        out_shape=jax.ShapeDtypeStruct((B, N, MLA_HEADS * MLA_V), BF16),
        scratch_shapes=[pltpu.VMEM((2, 1, MLA_TQ), F32), pltpu.VMEM((2, 1, MLA_TQ), F32),
                        pltpu.VMEM((2, LANES, MLA_TQ), F32), pltpu.VMEM((2, MLA_TK, MLA_TQ), BF16),
                        pltpu.VMEM((2, MLA_TK, MLA_TQ), F32), pltpu.VMEM((2, 1, MLA_TQ), F32)],
        compiler_params=pltpu.CompilerParams(vmem_limit_bytes=VMEM_LIMIT),
        name="mla",
    )(q_m, k_m, v_m)


NA_UNIT_HEADS = 8
NA_UNITS_PER_BLOCK = NA_HEADS // NA_UNIT_HEADS
NA_KEYS = 4 * TOK_BLOCK


def _na_kernel(q_ref, kc_ref, kp_ref, k0_ref, kn_ref, bp_ref, b0_ref, bn_ref,
               vc_ref, vp_ref, v0_ref, vn_ref, o_ref, s_scr, smax_scr, p_scr):
    @pl.when(pl.program_id(1) == 0)
    def _():
        s_scr[...] = jnp.zeros(s_scr.shape, F32)
        smax_scr[...] = jnp.zeros(smax_scr.shape, F32)
        p_scr[...] = jnp.zeros(p_scr.shape, BF16)

    lane = lax.broadcasted_iota(jnp.int32, (TOK_BLOCK, LANES), 1)

    for pp in range(NA_UNIT_HEADS // 2):
        outs = []
        for half in range(2):
            hq = 2 * pp + half
            vt = jnp.concatenate([vc_ref[0, hq], vp_ref[0, hq], v0_ref[0, hq], vn_ref[0, hq]], axis=1)
            o = jnp.dot(vt, p_scr[hq], preferred_element_type=F32)
            outs.append(jnp.transpose(o / o[ONES_LANE:ONES_LANE + 1, :]))
        o_ref[0, :, pp * LANES:(pp + 1) * LANES] = jnp.where(
            lane < NA_DIM, outs[0], pltpu.roll(outs[1], NA_DIM, 1)).astype(o_ref.dtype)

    for hq in range(NA_UNIT_HEADS):
        p_scr[hq] = jnp.exp2((s_scr[hq] - smax_scr[hq]).astype(BF16))

    dn = (((1,), (1,)), ((), ()))
    for pp in range(NA_UNIT_HEADS // 2):
        keys = jnp.concatenate([kc_ref[0, pp], kp_ref[0, pp], k0_ref[0, pp], kn_ref[0, pp]], axis=0)
        for half in range(2):
            hq = 2 * pp + half
            in_half = (lane >= NA_DIM) if half else (lane < NA_DIM)
            q = jnp.where(in_half, q_ref[0, pp], jnp.zeros((), BF16))
            s = lax.dot_general(keys, q, dn, preferred_element_type=F32)
            parts = [s[:TOK_BLOCK],
                     s[TOK_BLOCK:2 * TOK_BLOCK] + bp_ref[0, hq],
                     s[2 * TOK_BLOCK:3 * TOK_BLOCK] + b0_ref[0, hq],
                     s[3 * TOK_BLOCK:] + bn_ref[0, hq]]
            for j, part in enumerate(parts):
                s_scr[hq, j * TOK_BLOCK:(j + 1) * TOK_BLOCK, :] = part
            smax_scr[hq] = jnp.max(jnp.maximum(jnp.maximum(parts[0], parts[1]),
                                               jnp.maximum(parts[2], parts[3])), axis=0, keepdims=True)


def _natten(nq, nk, nv, bias):
    B, _, N, _ = nq.shape
    nb = N // TOK_BLOCK
    last = nb - 1
    n_units = nb * NA_UNITS_PER_BLOCK
    pipeline_depth = 2

    def unit(t, lag):
        u = jnp.clip(t - lag, 0, n_units - 1)
        return u // NA_UNITS_PER_BLOCK, u % NA_UNITS_PER_BLOCK

    tok_blocks = (lambda i: 0, lambda i: jnp.maximum(i - 1, 0) + 1, lambda i: i + 1,
                  lambda i: jnp.minimum(i + 1, last) + 1)
    variants = (lambda i: jnp.where(i == 0, V_NONE, jnp.where(i == last, V_PREV_ALL, V_PREV)),
                lambda i: jnp.where((i == 0) | (i == last), V_CUR_ALL, V_CUR),
                lambda i: jnp.where(i == 0, V_NEXT_ALL, jnp.where(i == last, V_NONE, V_NEXT)))

    def q_map(b, t):
        i, g = unit(t, 0)
        return b, g, i, 0

    def key_map(tok):
        def f(b, t):
            i, g = unit(t, 0)
            return b, g, tok(i), 0
        return f

    def bias_map(var):
        def f(b, t):
            i, g = unit(t, 0)
            return var(i), g, 0, 0
        return f

    def value_map(tok):
        def f(b, t):
            i, g = unit(t, pipeline_depth)
            return b, g, 0, tok(i)
        return f

    def out_map(b, t):
        i, g = unit(t, pipeline_depth)
        return b, i, g

    pair_blk = (1, NA_UNIT_HEADS // 2, TOK_BLOCK, LANES)
    return pl.pallas_call(
        _na_kernel,
        grid=(B, n_units + pipeline_depth),
        in_specs=([pl.BlockSpec(pair_blk, q_map)]
                  + [pl.BlockSpec(pair_blk, key_map(tok)) for tok in tok_blocks]
                  + [pl.BlockSpec((1, NA_UNIT_HEADS, TOK_BLOCK, TOK_BLOCK), bias_map(var)) for var in variants]
                  + [pl.BlockSpec((1, NA_UNIT_HEADS, LANES, TOK_BLOCK), value_map(tok)) for tok in tok_blocks]),
        out_specs=pl.BlockSpec((1, TOK_BLOCK, NA_UNIT_HEADS * NA_DIM), out_map),
        out_shape=jax.ShapeDtypeStruct((B, N, NA_HEADS * NA_DIM), BF16),
        scratch_shapes=[pltpu.VMEM((NA_UNIT_HEADS, NA_KEYS, TOK_BLOCK), F32),
                        pltpu.VMEM((NA_UNIT_HEADS, 1, TOK_BLOCK), F32),
                        pltpu.VMEM((NA_UNIT_HEADS, NA_KEYS, TOK_BLOCK), BF16)],
        compiler_params=pltpu.CompilerParams(vmem_limit_bytes=VMEM_LIMIT),
        name="natten",
    )(nq, nk, nk, nk, nk, bias, bias, bias, nv, nv, nv, nv)


OUT_TM = 512
R_G1, R_SH2, R_SC2, R_G2 = 2, 3, 4, 5
GROUP_LANE = N_EXPERTS


def _first_max_lane(x, valid, lane):
    xm = jnp.where(valid, x, -jnp.inf)
    mx = jnp.max(xm, axis=1, keepdims=True)
    idx = jnp.min(jnp.where(valid & (xm == mx), lane, LANES), axis=1, keepdims=True)
    return mx, idx


def _outproj_kernel(om_ref, on_ref, x_ref, mod_ref, wo_ref, g_ref, wr_ref, br_ref,
                    x1_out, h2_out, comb_out):
    half = wo_ref.shape[0] // 2
    mixed = (jnp.dot(om_ref[...], wo_ref[:half, :], preferred_element_type=F32)
             + jnp.dot(on_ref[...], wo_ref[half:, :], preferred_element_type=F32))
    x1 = x_ref[...] + mod_ref[0, R_G1:R_G1 + 1, :] * mixed
    x1_out[...] = x1
    h2 = _rms(x1, g_ref[...]) * (1.0 + mod_ref[0, R_SC2:R_SC2 + 1, :]) + mod_ref[0, R_SH2:R_SH2 + 1, :]
    h2_out[...] = h2.astype(BF16)

    logits = jnp.dot(h2, wr_ref[...], precision=lax.Precision.HIGHEST,
                     preferred_element_type=F32) + br_ref[...]
    lane = lax.broadcasted_iota(jnp.int32, logits.shape, 1)
    is_g = lane < N_GROUPS
    g_max, g_sel = _first_max_lane(logits, is_g, lane)
    g_w = 1.0 / jnp.sum(jnp.where(is_g, jnp.exp(logits - g_max), 0.0), axis=1, keepdims=True)
    e_lo = N_GROUPS + g_sel * EXPERTS_PER_GROUP
    in_grp = (lane >= e_lo) & (lane < e_lo + EXPERTS_PER_GROUP)
    m1, i1 = _first_max_lane(logits, in_grp, lane)
    m2, i2 = _first_max_lane(logits, in_grp & (lane != i1), lane)
    e2 = jnp.exp(m2 - m1)
    w1 = 1.0 / (1.0 + e2)
    w2 = e2 / (1.0 + e2)
    comb = jnp.where(lane == i1, g_w * w1, jnp.where(lane == i2, g_w * w2, 0.0))
    comb_out[...] = jnp.where(lane == GROUP_LANE, g_sel.astype(F32), pltpu.roll(comb, LANES - N_GROUPS, 1))


def _outproj(o_mla, o_na, x2d, mod, wo, g, wr, br, n_per_batch):
    T, D = x2d.shape
    per_b = n_per_batch // OUT_TM
    tok = lambda w: pl.BlockSpec((OUT_TM, w), lambda t: (t, 0))
    full = lambda shape: pl.BlockSpec(shape, lambda t: (0,) * len(shape))
    return pl.pallas_call(
        _outproj_kernel,
        grid=(T // OUT_TM,),
        in_specs=[tok(o_mla.shape[1]), tok(o_na.shape[1]), tok(D),
                  pl.BlockSpec((1, 6, D), lambda t: (t // per_b, 0, 0)),
                  full(wo.shape), full((1, D)), full(wr.shape), full(br.shape)],
        out_specs=[tok(D), tok(D), tok(LANES)],
        out_shape=[jax.ShapeDtypeStruct((T, D), F32), jax.ShapeDtypeStruct((T, D), BF16),
                   jax.ShapeDtypeStruct((T, LANES), F32)],
        compiler_params=pltpu.CompilerParams(vmem_limit_bytes=VMEM_LIMIT),
        name="outproj",
    )(o_mla, o_na, x2d, mod, wo, g, wr, br)


MOE_TT = 1024
MOE_CH = 128
MOE_NBLK = MOE_TT // MOE_CH + (N_GROUPS - 1)
MOE_TTP = MOE_NBLK * MOE_CH


def _moe_sort_kernel(h_ref, comb_ref, xs_out, cs_out, pos_out, cnt_out, tri_scr):
    @pl.when(pl.program_id(0) == 0)
    def _():
        r = lax.broadcasted_iota(jnp.int32, tri_scr.shape, 0)
        c = lax.broadcasted_iota(jnp.int32, tri_scr.shape, 1)
        tri_scr[...] = jnp.where(c < r, 1.0, 0.0).astype(BF16)

    comb = comb_ref[...]
    lane = lax.broadcasted_iota(jnp.int32, comb.shape, 1)
    lane_f = lane.astype(F32)
    gid = jnp.sum(jnp.where(lane == GROUP_LANE, comb, 0.0), axis=1, keepdims=True)
    onehot = jnp.where((lane_f == gid) & (lane < N_GROUPS), 1.0, 0.0)
    ahead = jnp.dot(tri_scr[...], onehot.astype(BF16), preferred_element_type=F32)
    n = jnp.sum(onehot, axis=0, keepdims=True)
    padded = jnp.ceil(n * (1.0 / MOE_CH)) * MOE_CH
    start = jnp.zeros_like(padded)
    for k in range(1, N_GROUPS):
        start = start + jnp.where(lane[:1] >= k, pltpu.roll(padded, k, 1), 0.0)
    pos = jnp.sum(onehot * (ahead + start), axis=1, keepdims=True)
    pos_rep = jnp.broadcast_to(pos, comb.shape)
    pos_out[...] = pos_rep
    cnt_out[0] = jnp.broadcast_to(n, (8, LANES))

    pos_row = jnp.transpose(pos_rep)[0:1, :]
    row = lax.broadcasted_iota(jnp.int32, (MOE_TTP, MOE_TT), 0).astype(F32)
    perm = jnp.where(row == pos_row, 1.0, 0.0).astype(BF16)
    xs_out[...] = jnp.dot(perm, h_ref[...], preferred_element_type=F32).astype(BF16)
    hi = comb.astype(BF16)
    lo = (comb - hi.astype(F32)).astype(BF16)
    cs_out[...] = (jnp.dot(perm, hi, preferred_element_type=F32)
                   + jnp.dot(perm, lo, preferred_element_type=F32))


def _moe_sort(h2, comb):
    T, D = h2.shape
    nt = T // MOE_TT
    return pl.pallas_call(
        _moe_sort_kernel,
        grid=(nt,),
        in_specs=[pl.BlockSpec((MOE_TT, D), lambda t: (t, 0)),
                  pl.BlockSpec((MOE_TT, LANES), lambda t: (t, 0))],
        out_specs=[pl.BlockSpec((MOE_TTP, D), lambda t: (t, 0)),
                   pl.BlockSpec((MOE_TTP, LANES), lambda t: (t, 0)),
                   pl.BlockSpec((MOE_TT, LANES), lambda t: (t, 0)),
                   pl.BlockSpec((1, 8, LANES), lambda t: (t, 0, 0))],
        out_shape=[jax.ShapeDtypeStruct((nt * MOE_TTP, D), BF16),
                   jax.ShapeDtypeStruct((nt * MOE_TTP, LANES), F32),
                   jax.ShapeDtypeStruct((T, LANES), F32),
                   jax.ShapeDtypeStruct((nt, 8, LANES), F32)],
        scratch_shapes=[pltpu.VMEM((MOE_TT, MOE_TT), BF16)],
        compiler_params=pltpu.CompilerParams(vmem_limit_bytes=VMEM_LIMIT),
        name="moe_sort",
    )(h2, comb)


def _moe_experts_kernel(blk_ref, grp_ref, valid_ref, x_ref, c_ref, wg_ref, wu_ref, wd_ref, y_ref):
    s = pl.program_id(0)

    @pl.when(valid_ref[s] == 0)
    def _():
        y_ref[...] = jnp.zeros(y_ref.shape, y_ref.dtype)

    @pl.when(valid_ref[s] != 0)
    def _():
        x = x_ref[...]
        comb = c_ref[...]
        lane = lax.broadcasted_iota(jnp.int32, comb.shape, 1)
        e0 = grp_ref[s] * EXPERTS_PER_GROUP
        y = None
        for e in range(EXPERTS_PER_GROUP):
            gate = jnp.dot(x, wg_ref[e], preferred_element_type=F32)
            up = jnp.dot(x, wu_ref[e], preferred_element_type=F32)
            w = jnp.sum(jnp.where(lane == e0 + e, comb, 0.0), axis=1, keepdims=True)
            hid = (gate * jax.nn.sigmoid(gate) * up * w).astype(BF16)
            part = jnp.dot(hid, wd_ref[e], preferred_element_type=F32)
            y = part if y is None else y + part
        y_ref[...] = y.astype(y_ref.dtype)


def _moe_experts(blk, grp, valid, xs, cs, wg, wu, wd):
    D = xs.shape[1]
    n_slots = blk.shape[0]
    return pl.pallas_call(
        _moe_experts_kernel,
        grid_spec=pltpu.PrefetchScalarGridSpec(
            num_scalar_prefetch=3,
            grid=(n_slots,),
            in_specs=[pl.BlockSpec((MOE_CH, D), lambda s, blk, grp, valid: (blk[s], 0)),
                      pl.BlockSpec((MOE_CH, LANES), lambda s, blk, grp, valid: (blk[s], 0)),
                      pl.BlockSpec((EXPERTS_PER_GROUP, D, D_EXPERT), lambda s, blk, grp, valid: (grp[s], 0, 0)),
                      pl.BlockSpec((EXPERTS_PER_GROUP, D, D_EXPERT), lambda s, blk, grp, valid: (grp[s], 0, 0)),
                      pl.BlockSpec((EXPERTS_PER_GROUP, D_EXPERT, D), lambda s, blk, grp, valid: (grp[s], 0, 0))],
            out_specs=pl.BlockSpec((MOE_CH, D), lambda s, blk, grp, valid: (blk[s], 0)),
        ),
        out_shape=jax.ShapeDtypeStruct(xs.shape, BF16),
        compiler_params=pltpu.CompilerParams(vmem_limit_bytes=VMEM_LIMIT),
        name="moe_experts",
    )(blk, grp, valid, xs, cs, wg, wu, wd)


def _moe_final_kernel(ys_ref, pos_ref, x1_ref, mod_ref, fg_ref, o_ref):
    col = lax.broadcasted_iota(jnp.int32, (MOE_TT, MOE_TTP), 1).astype(F32)
    unperm = jnp.where(col == pltpu.repeat(pos_ref[...], MOE_NBLK, axis=1), 1.0, 0.0).astype(BF16)
    y = jnp.dot(unperm, ys_ref[...], preferred_element_type=F32)
    o_ref[...] = _rms(x1_ref[...] + mod_ref[0, R_G2:R_G2 + 1, :] * y, fg_ref[...])


def _moe_final(ys, pos, x1, mod, fg, n_per_batch):
    T, D = x1.shape
    per_b = n_per_batch // MOE_TT
    return pl.pallas_call(
        _moe_final_kernel,
        grid=(T // MOE_TT,),
        in_specs=[pl.BlockSpec((MOE_TTP, D), lambda t: (t, 0)),
                  pl.BlockSpec((MOE_TT, LANES), lambda t: (t, 0)),
                  pl.BlockSpec((MOE_TT, D), lambda t: (t, 0)),
                  pl.BlockSpec((1, 6, D), lambda t: (t // per_b, 0, 0)),
                  pl.BlockSpec((1, D), lambda t: (0, 0))],
        out_specs=pl.BlockSpec((MOE_TT, D), lambda t: (t, 0)),
        out_shape=jax.ShapeDtypeStruct((T, D), F32),
        compiler_params=pltpu.CompilerParams(vmem_limit_bytes=VMEM_LIMIT),
        name="moe_final",
    )(ys, pos, x1, mod, fg)


def _chunk_schedule(counts):
    nt = counts.shape[0]
    nchunks = jnp.ceil(counts * (1.0 / MOE_CH)).astype(jnp.int32)
    ends = jnp.cumsum(nchunks, axis=1)
    j = jnp.arange(MOE_NBLK, dtype=jnp.int32)
    grp = jnp.sum(j[None, :, None] >= ends[:, None, :], axis=2).astype(jnp.int32)
    valid = grp < N_GROUPS
    grp = jnp.minimum(grp, N_GROUPS - 1).reshape(-1)
    blk = (jnp.arange(nt, dtype=jnp.int32)[:, None] * MOE_NBLK + j[None, :]).reshape(-1)
    order = jnp.argsort(grp, stable=True)
    return blk[order], grp[order], valid.reshape(-1).astype(jnp.int32)[order]


def _rope_tables(n):
    pos = jnp.arange(n)
    half = MLA_ROPE // 2
    inv_freq = ROPE_THETA ** (-jnp.arange(0, half, 2, dtype=F32) / half)
    ang = jnp.concatenate([(pos // GRID_W).astype(F32)[:, None] * inv_freq,
                           (pos % GRID_W).astype(F32)[:, None] * inv_freq], axis=-1)
    cos, sin = jnp.cos(ang), jnp.sin(ang)
    one = jnp.ones((n, MLA_NOPE), F32)
    zero = jnp.zeros((n, MLA_NOPE), F32)
    pad = jnp.zeros((n, LANES - MLA_QK), F32)
    z16 = jnp.zeros((n, half), F32)
    c = jnp.concatenate([one, cos, cos, pad], axis=1)
    sa = jnp.concatenate([zero, -sin, z16, pad], axis=1)
    sb = jnp.concatenate([zero, z16, sin, pad], axis=1)
    ident = jnp.concatenate([jnp.ones((CTX_LEN, MLA_QK), F32), jnp.zeros((CTX_LEN, LANES - MLA_QK), F32)], axis=1)
    zc = jnp.zeros((CTX_LEN, LANES), F32)
    return (jnp.concatenate([ident, c], axis=0), jnp.concatenate([zc, sa], axis=0),
            jnp.concatenate([zc, sb], axis=0))


def kernel(x, c, ctx, c_ctx, w_mod, b_mod, norm_attn_g, norm_ffn_g, w_in, q_a_norm_g, kv_a_norm_g, w_uq, w_ukv,
           na_rel_bias, w_out, w_router_group, b_router_group, w_router_expert, b_router_expert, w_gate, w_up,
           w_down, final_norm_g):
    B, N, D = x.shape
    assert w_mod.shape[0] == 1, "single-layer kernel"
    assert (B, N, D) == (2, 8192, D_MODEL) and ctx.shape == (B, CTX_LEN, D)

    cond8 = jnp.concatenate([c, c_ctx[None, :], jnp.zeros((8 - B - 1, D), F32)], axis=0)
    mod = _adaln(cond8, w_mod[0], b_mod[0]).reshape(8, 6, D)

    wi = w_in[0]
    na_w = NA_HEADS * NA_DIM
    k0 = Q_LORA + na_w + KV_LORA
    win = jnp.concatenate([wi[:, :k0], jnp.zeros((D, MLA_NOPE), F32), wi[:, k0:k0 + MLA_ROPE],
                           jnp.zeros((D, LANES - MLA_QK), F32), wi[:, k0 + MLA_ROPE:]], axis=1).astype(BF16)
    assert win.shape[1] == C_END
    wuq = jnp.pad(w_uq[0].reshape(Q_LORA, MLA_HEADS, MLA_QK), ((0, 0), (0, 0), (0, LANES - MLA_QK)))
    wuq = wuq.reshape(Q_LORA, MLA_HEADS * LANES).astype(BF16)
    wkv = w_ukv[0].reshape(KV_LORA, MLA_HEADS, MLA_NOPE + MLA_V)
    wk = jnp.pad(wkv[:, :, :MLA_NOPE], ((0, 0), (0, 0), (0, LANES - MLA_NOPE))).reshape(KV_LORA, MLA_HEADS * LANES)
    wv = jnp.pad(wkv[:, :, MLA_NOPE:], ((0, 0), (0, 0), (0, LANES - MLA_V))).reshape(KV_LORA, MLA_HEADS * LANES)
    wukv = jnp.concatenate([wk, wv], axis=1).astype(BF16)
    rc, rsa, rsb = _rope_tables(N)

    q_m, k_m, v_m, nq, nk, nv = _proj(x, ctx, mod, norm_attn_g, win, q_a_norm_g, kv_a_norm_g, wuq, wukv,
                                      rc, rsa, rsb)
    o_mla = _mla(q_m, k_m, v_m)
    o_na = _natten(nq, nk, nv, _na_bias(na_rel_bias[0]))

    n_r = N_GROUPS + N_EXPERTS
    wr = jnp.pad(jnp.concatenate([w_router_group[0], w_router_expert[0]], axis=1), ((0, 0), (0, LANES - n_r)))
    br = jnp.pad(jnp.concatenate([b_router_group[0], b_router_expert[0]]), (0, LANES - n_r)).reshape(1, LANES)
    T = B * N
    x1, h2, comb = _outproj(o_mla.reshape(T, -1), o_na.reshape(T, -1), x.reshape(T, D), mod,
                            w_out[0].astype(BF16), norm_ffn_g, wr, br, N)
    xs, cs, pos, counts = _moe_sort(h2, comb)
    blk, grp, valid = _chunk_schedule(counts[:, 0, :N_GROUPS])
    ys = _moe_experts(blk, grp, valid, xs, cs, w_gate[0].astype(BF16), w_up[0].astype(BF16),
                      w_down[0].astype(BF16))
    out = _moe_final(ys, pos, x1, mod, final_norm_g.reshape(1, D), N)
    return out.reshape(B, N, D)
```

```python
import functools

import jax
import jax.numpy as jnp
import numpy as np
from jax import lax
from jax.experimental import pallas as pl
from jax.experimental.pallas import tpu as pltpu

D_MODEL = 1024
GRID_W = 64
CTX_LEN = 256
MLA_HEADS = 8
MLA_NOPE = 64
MLA_ROPE = 32
MLA_QK = MLA_NOPE + MLA_ROPE
MLA_V = 64
Q_LORA = 384
KV_LORA = 256
NA_HEADS = 8
NA_DIM = 64
NA_WIN_H = 8
NA_WIN_W = 16
N_GROUPS = 4
EXPERTS_PER_GROUP = 4
N_EXPERTS = N_GROUPS * EXPERTS_PER_GROUP
D_EXPERT = 512
ROPE_THETA = 10000.0
NORM_EPS = 1e-6
MASK_VALUE = -1e30
LOG2E = 1.4426950408889634

LANES = 128
HEAD_PAIRS = 4
TOK_BLOCK = 256
ROWS_PER_BLOCK = TOK_BLOCK // GRID_W
VMEM_LIMIT = 48 * 1024 * 1024

F32 = jnp.float32
BF16 = jnp.bfloat16


def _rms(x, g):
    return x * lax.rsqrt(jnp.mean(x * x, axis=-1, keepdims=True) + NORM_EPS) * g


def _adaln_kernel(c_ref, w_ref, b_ref, o_ref):
    c = c_ref[...]
    s = c * jax.nn.sigmoid(c)
    o_ref[...] = jnp.dot(s, w_ref[...], precision=lax.Precision.HIGHEST,
                         preferred_element_type=F32) + b_ref[...]


def _adaln(cond8, w_mod, b_mod):
    n_out = w_mod.shape[1]
    bn = 1024
    return pl.pallas_call(
        _adaln_kernel,
        grid=(n_out // bn,),
        in_specs=[pl.BlockSpec((8, D_MODEL), lambda n: (0, 0)),
                  pl.BlockSpec((D_MODEL, bn), lambda n: (0, n)),
                  pl.BlockSpec((1, bn), lambda n: (0, n))],
        out_specs=pl.BlockSpec((8, bn), lambda n: (0, n)),
        out_shape=jax.ShapeDtypeStruct((8, n_out), F32),
        name="adaln",
    )(cond8, w_mod, b_mod.reshape(1, n_out))


NA_VARIANTS = ((-1, False), (0, False), (1, False), (0, True), (1, True), (-1, True), (0, None))
V_PREV, V_CUR, V_NEXT, V_CUR_ALL, V_NEXT_ALL, V_PREV_ALL, V_NONE = range(7)
N_DR = 2 * NA_WIN_H - 1
N_DC = 2 * NA_WIN_W - 1


def _na_bias_kernel(rel_ref, o_ref):
    h = pl.program_id(0)
    kc = lax.broadcasted_iota(jnp.int32, (GRID_W, TOK_BLOCK), 0)
    qq = lax.broadcasted_iota(jnp.int32, (GRID_W, TOK_BLOCK), 1)
    qc = qq % GRID_W
    qblk = qq // GRID_W
    col_start = jnp.clip(qc - NA_WIN_W // 2, 0, GRID_W - NA_WIN_W)
    col_in = (kc >= col_start) & (kc < col_start + NA_WIN_W)
    dc = kc - qc + (NA_WIN_W - 1)
    tables = []
    for d in range(N_DR):
        t = jnp.full((GRID_W, TOK_BLOCK), MASK_VALUE, F32)
        base = (h * N_DR + d) * N_DC
        for m in range(N_DC):
            t = jnp.where(dc == m, rel_ref[base + m] * LOG2E, t)
        tables.append(jnp.where(col_in, t, MASK_VALUE))
    for v, (boff, all_valid) in enumerate(NA_VARIANTS):
        for cb in range(ROWS_PER_BLOCK):
            tile = jnp.full((GRID_W, TOK_BLOCK), MASK_VALUE, F32)
            if all_valid is not None:
                for a in range(ROWS_PER_BLOCK):
                    dr = ROWS_PER_BLOCK * boff + cb - a
                    if all_valid or (-(NA_WIN_H // 2) <= dr < NA_WIN_H - NA_WIN_H // 2):
                        tile = jnp.where(qblk == a, tables[dr + NA_WIN_H - 1], tile)
            o_ref[v, 0, cb * GRID_W:(cb + 1) * GRID_W, :] = tile


def _na_bias(rel_bias):
    return pl.pallas_call(
        _na_bias_kernel,
        grid=(NA_HEADS,),
        in_specs=[pl.BlockSpec(memory_space=pltpu.SMEM)],
        out_specs=pl.BlockSpec((len(NA_VARIANTS), 1, TOK_BLOCK, TOK_BLOCK), lambda h: (0, h, 0, 0)),
        out_shape=jax.ShapeDtypeStruct((len(NA_VARIANTS), NA_HEADS, TOK_BLOCK, TOK_BLOCK), F32),
        name="na_bias",
    )(rel_bias.reshape(-1))


C_QLAT = 0
C_NAQ = Q_LORA
C_KVLAT = C_NAQ + NA_HEADS * NA_DIM
C_KROPE = C_KVLAT + KV_LORA
C_NAK = C_KROPE + LANES
C_NAV = C_NAK + NA_HEADS * NA_DIM
C_END = C_NAV + NA_HEADS * NA_DIM
MLA_SCALE = MLA_QK ** -0.5 * LOG2E
NA_SCALE = NA_DIM ** -0.5 * LOG2E
ONES_LANE = MLA_V


def _rope(x, c, sa, sb):
    return x * c + pltpu.roll(x, LANES - MLA_ROPE // 2, 1) * sa + pltpu.roll(x, MLA_ROPE // 2, 1) * sb


def _proj_kernel(x_ref, ctx_ref, mod_ref, g_ref, win_ref, qg_ref, kvg_ref, wuq_ref, wukv_ref,
                 c_ref, sa_ref, sb_ref,
                 q_out, k_out, v_out, nq_out, nk_out, nv_out, h_scr, proj_scr):
    j = pl.program_id(1)

    @pl.when(j == 0)
    def _():
        h_scr[...] = jnp.zeros(h_scr.shape, h_scr.dtype)
        proj_scr[...] = jnp.zeros(proj_scr.shape, proj_scr.dtype)

    proj = proj_scr
    c, sa, sb = c_ref[...], sa_ref[...], sb_ref[...]

    qn = _rms(proj[:, C_QLAT:C_NAQ], qg_ref[...]).astype(BF16)
    q = jnp.dot(qn, wuq_ref[...], preferred_element_type=F32) * MLA_SCALE
    for h in range(MLA_HEADS):
        q_out[0, h] = _rope(q[:, h * LANES:(h + 1) * LANES], c, sa, sb).astype(BF16)

    kvn = _rms(proj[:, C_KVLAT:C_KROPE], kvg_ref[...]).astype(BF16)
    kv = jnp.dot(kvn, wukv_ref[...], preferred_element_type=F32)
    kr = _rope(proj[:, C_KROPE:C_NAK], c, sa, sb)
    v0 = MLA_HEADS * LANES
    lane = lax.broadcasted_iota(jnp.int32, (TOK_BLOCK, LANES), 1)
    for h in range(MLA_HEADS):
        k_out[0, h] = (kv[:, h * LANES:(h + 1) * LANES] + kr).astype(BF16)
        v = jnp.where(lane == ONES_LANE, 1.0, kv[:, v0 + h * LANES:v0 + (h + 1) * LANES])
        v_out[0, h] = jnp.transpose(v).astype(BF16)
    for p in range(HEAD_PAIRS):
        nq_out[0, p] = (proj[:, C_NAQ + p * LANES:C_NAQ + (p + 1) * LANES] * NA_SCALE).astype(BF16)
        nk_out[0, p] = proj[:, C_NAK + p * LANES:C_NAK + (p + 1) * LANES].astype(BF16)
        nv_pair = proj[:, C_NAV + p * LANES:C_NAV + (p + 1) * LANES]
        for half in range(2):
            nv = pltpu.roll(nv_pair, NA_DIM, 1) if half else nv_pair
            nv = jnp.where(lane < NA_DIM, nv, jnp.where(lane == ONES_LANE, 1.0, 0.0))
            nv_out[0, 2 * p + half] = jnp.transpose(nv).astype(BF16)

    proj_scr[...] = jnp.dot(h_scr[...], win_ref[...], preferred_element_type=F32)
    xt = jnp.where(j == 0, ctx_ref[0], x_ref[0])
    h_scr[...] = (_rms(xt, g_ref[...]) * (1.0 + mod_ref[0, 1:2, :]) + mod_ref[0, 0:1, :]).astype(BF16)


def _proj(x, ctx, mod, g, win, qg, kvg, wuq, wukv, rc, rsa, rsb):
    B, N, D = x.shape
    nb = N // TOK_BLOCK
    ntot = N + CTX_LEN
    full = lambda shape: pl.BlockSpec(shape, lambda b, j: (0,) * len(shape))
    depth = 2
    tile_in = lambda j: jnp.minimum(j, nb)
    tile_out = lambda j: jnp.clip(j - depth, 0, nb)
    lat = lambda b, j: (b, 0, jnp.maximum(tile_out(j) - 1, 0), 0)
    alltok = lambda b, j: (b, 0, tile_out(j), 0)
    feat = lambda b, j: (b, 0, 0, tile_out(j))
    rope = lambda b, j: (tile_out(j), 0)
    return pl.pallas_call(
        _proj_kernel,
        grid=(B, nb + 1 + depth),
        in_specs=[pl.BlockSpec((1, TOK_BLOCK, D), lambda b, j: (b, jnp.maximum(tile_in(j) - 1, 0), 0)),
                  pl.BlockSpec((1, CTX_LEN, D), lambda b, j: (b, 0, 0)),
                  pl.BlockSpec((1, 6, D), lambda b, j: (jnp.where(j == 0, B, b), 0, 0)),
                  full((1, D)), full(win.shape), full((1, Q_LORA)), full((1, KV_LORA)),
                  full(wuq.shape), full(wukv.shape),
                  pl.BlockSpec((TOK_BLOCK, LANES), rope),
                  pl.BlockSpec((TOK_BLOCK, LANES), rope),
                  pl.BlockSpec((TOK_BLOCK, LANES), rope)],
        out_specs=[pl.BlockSpec((1, MLA_HEADS, TOK_BLOCK, LANES), lat),
                   pl.BlockSpec((1, MLA_HEADS, TOK_BLOCK, LANES), alltok),
                   pl.BlockSpec((1, MLA_HEADS, LANES, TOK_BLOCK), feat),
                   pl.BlockSpec((1, HEAD_PAIRS, TOK_BLOCK, LANES), lat),
                   pl.BlockSpec((1, HEAD_PAIRS, TOK_BLOCK, LANES), alltok),
                   pl.BlockSpec((1, NA_HEADS, LANES, TOK_BLOCK), feat)],
        out_shape=[jax.ShapeDtypeStruct((B, MLA_HEADS, N, LANES), BF16),
                   jax.ShapeDtypeStruct((B, MLA_HEADS, ntot, LANES), BF16),
                   jax.ShapeDtypeStruct((B, MLA_HEADS, LANES, ntot), BF16),
                   jax.ShapeDtypeStruct((B, HEAD_PAIRS, N, LANES), BF16),
                   jax.ShapeDtypeStruct((B, HEAD_PAIRS, ntot, LANES), BF16),
                   jax.ShapeDtypeStruct((B, NA_HEADS, LANES, ntot), BF16)],
        scratch_shapes=[pltpu.VMEM((TOK_BLOCK, D), BF16), pltpu.VMEM((TOK_BLOCK, C_END), F32)],
        compiler_params=pltpu.CompilerParams(vmem_limit_bytes=VMEM_LIMIT),
        name="proj",
    )(x, ctx, mod, g, win, qg, kvg, wuq, wukv, rc, rsa, rsb)


MLA_TQ = 1024
MLA_TK = 768


def _mla_kernel(q_ref, k_ref, v_ref, o_ref, m_scr, alpha_scr, acc_scr, p_scr, s_scr, smax_scr):
    n_chunks = k_ref.shape[2] // MLA_TK
    m_scr[...] = jnp.full(m_scr.shape, -jnp.inf, F32)
    acc_scr[...] = jnp.zeros(acc_scr.shape, F32)

    def scores(ci, hh):
        off = pl.multiple_of(ci * MLA_TK, MLA_TK)
        k = k_ref[0, hh, pl.ds(off, MLA_TK), :]
        s = lax.dot_general(k, q_ref[0, hh], (((1,), (1,)), ((), ())), preferred_element_type=F32)
        s_scr[hh] = s
        smax_scr[hh] = jnp.max(s, axis=0, keepdims=True)

    def softmax(hh):
        m_prev = m_scr[hh]
        m_next = jnp.maximum(m_prev, smax_scr[hh])
        p_scr[hh] = jnp.exp2((s_scr[hh] - m_next).astype(BF16))
        alpha_scr[hh] = jnp.exp2(m_prev - m_next)
        m_scr[hh] = m_next

    def values(ci, hh):
        off = pl.multiple_of(ci * MLA_TK, MLA_TK)
        vt = v_ref[0, hh, :, pl.ds(off, MLA_TK)]
        acc_scr[hh] = alpha_scr[hh] * acc_scr[hh] + jnp.dot(vt, p_scr[hh], preferred_element_type=F32)

    for hh in range(2):
        scores(0, hh)
    for hh in range(2):
        softmax(hh)
        scores(1, hh)

    def body(ci, carry):
        for hh in range(2):
            values(ci - 1, hh)
            softmax(hh)
            scores(ci + 1, hh)
        return carry

    lax.fori_loop(1, n_chunks - 1, body, 0)
    for hh in range(2):
        values(n_chunks - 2, hh)
        softmax(hh)
    for hh in range(2):
        values(n_chunks - 1, hh)
    outs = []
    for hh in range(2):
        acc = acc_scr[hh]
        outs.append(jnp.transpose(acc / acc[ONES_LANE:ONES_LANE + 1, :]))
    lane = lax.broadcasted_iota(jnp.int32, (MLA_TQ, LANES), 1)
    o_ref[0] = jnp.where(lane < MLA_V, outs[0], pltpu.roll(outs[1], MLA_V, 1)).astype(o_ref.dtype)


def _mla(q_m, k_m, v_m):
    B, _, N, _ = q_m.shape
    ntot = k_m.shape[2]
    return pl.pallas_call(
        _mla_kernel,
        grid=(B, HEAD_PAIRS, N // MLA_TQ),
        in_specs=[pl.BlockSpec((1, 2, MLA_TQ, LANES), lambda b, p, i: (b, p, i, 0)),
                  pl.BlockSpec((1, 2, ntot, LANES), lambda b, p, i: (b, p, 0, 0)),
                  pl.BlockSpec((1, 2, LANES, ntot), lambda b, p, i: (b, p, 0, 0))],
        out_specs=pl.BlockSpec((1, MLA_TQ, LANES), lambda b, p, i: (b, i, p)),
        out_shape=jax.ShapeDtypeStruct((B, N, MLA_HEADS * MLA_V), BF16),
        scratch_shapes=[pltpu.VMEM((2, 1, MLA_TQ), F32), pltpu.VMEM((2, 1, MLA_TQ), F32),
                        pltpu.VMEM((2, LANES, MLA_TQ), F32), pltpu.VMEM((2, MLA_TK, MLA_TQ), BF16),
                        pltpu.VMEM((2, MLA_TK, MLA_TQ), F32), pltpu.VMEM((2, 1, MLA_TQ), F32)],
        compiler_params=pltpu.CompilerParams(vmem_limit_bytes=VMEM_LIMIT),
        name="mla",
    )(q_m, k_m, v_m)


NA_UNIT_HEADS = 8
NA_UNITS_PER_BLOCK = NA_HEADS // NA_UNIT_HEADS
NA_KEYS = 4 * TOK_BLOCK


def _na_kernel(q_ref, kc_ref, kp_ref, k0_ref, kn_ref, bp_ref, b0_ref, bn_ref,
               vc_ref, vp_ref, v0_ref, vn_ref, o_ref, s_scr, smax_scr, p_scr):
    @pl.when(pl.program_id(1) == 0)
    def _():
        s_scr[...] = jnp.zeros(s_scr.shape, F32)
        smax_scr[...] = jnp.zeros(smax_scr.shape, F32)
        p_scr[...] = jnp.zeros(p_scr.shape, BF16)

    lane = lax.broadcasted_iota(jnp.int32, (TOK_BLOCK, LANES), 1)

    for pp in range(NA_UNIT_HEADS // 2):
        outs = []
        for half in range(2):
            hq = 2 * pp + half
            vt = jnp.concatenate([vc_ref[0, hq], vp_ref[0, hq], v0_ref[0, hq], vn_ref[0, hq]], axis=1)
            o = jnp.dot(vt, p_scr[hq], preferred_element_type=F32)
            outs.append(jnp.transpose(o / o[ONES_LANE:ONES_LANE + 1, :]))
        o_ref[0, :, pp * LANES:(pp + 1) * LANES] = jnp.where(
            lane < NA_DIM, outs[0], pltpu.roll(outs[1], NA_DIM, 1)).astype(o_ref.dtype)

    for hq in range(NA_UNIT_HEADS):
        p_scr[hq] = jnp.exp2((s_scr[hq] - smax_scr[hq]).astype(BF16))

    dn = (((1,), (1,)), ((), ()))
    for pp in range(NA_UNIT_HEADS // 2):
        keys = jnp.concatenate([kc_ref[0, pp], kp_ref[0, pp], k0_ref[0, pp], kn_ref[0, pp]], axis=0)
        for half in range(2):
            hq = 2 * pp + half
            in_half = (lane >= NA_DIM) if half else (lane < NA_DIM)
            q = jnp.where(in_half, q_ref[0, pp], jnp.zeros((), BF16))
            s = lax.dot_general(keys, q, dn, preferred_element_type=F32)
            parts = [s[:TOK_BLOCK],
                     s[TOK_BLOCK:2 * TOK_BLOCK] + bp_ref[0, hq],
                     s[2 * TOK_BLOCK:3 * TOK_BLOCK] + b0_ref[0, hq],
                     s[3 * TOK_BLOCK:] + bn_ref[0, hq]]
            for j, part in enumerate(parts):
                s_scr[hq, j * TOK_BLOCK:(j + 1) * TOK_BLOCK, :] = part
            smax_scr[hq] = jnp.max(jnp.maximum(jnp.maximum(parts[0], parts[1]),
                                               jnp.maximum(parts[2], parts[3])), axis=0, keepdims=True)


def _natten(nq, nk, nv, bias):
    B, _, N, _ = nq.shape
    nb = N // TOK_BLOCK
    last = nb - 1
    n_units = nb * NA_UNITS_PER_BLOCK
    pipeline_depth = 2

    def unit(t, lag):
        u = jnp.clip(t - lag, 0, n_units - 1)
        return u // NA_UNITS_PER_BLOCK, u % NA_UNITS_PER_BLOCK

    tok_blocks = (lambda i: 0, lambda i: jnp.maximum(i - 1, 0) + 1, lambda i: i + 1,
                  lambda i: jnp.minimum(i + 1, last) + 1)
    variants = (lambda i: jnp.where(i == 0, V_NONE, jnp.where(i == last, V_PREV_ALL, V_PREV)),
                lambda i: jnp.where((i == 0) | (i == last), V_CUR_ALL, V_CUR),
                lambda i: jnp.where(i == 0, V_NEXT_ALL, jnp.where(i == last, V_NONE, V_NEXT)))

    def q_map(b, t):
        i, g = unit(t, 0)
        return b, g, i, 0

    def key_map(tok):
        def f(b, t):
            i, g = unit(t, 0)
            return b, g, tok(i), 0
        return f

    def bias_map(var):
        def f(b, t):
            i, g = unit(t, 0)
            return var(i), g, 0, 0
        return f

    def value_map(tok):
        def f(b, t):
            i, g = unit(t, pipeline_depth)
            return b, g, 0, tok(i)
        return f

    def out_map(b, t):
        i, g = unit(t, pipeline_depth)
        return b, i, g

    pair_blk = (1, NA_UNIT_HEADS // 2, TOK_BLOCK, LANES)
    return pl.pallas_call(
        _na_kernel,
        grid=(B, n_units + pipeline_depth),
        in_specs=([pl.BlockSpec(pair_blk, q_map)]
                  + [pl.BlockSpec(pair_blk, key_map(tok)) for tok in tok_blocks]
                  + [pl.BlockSpec((1, NA_UNIT_HEADS, TOK_BLOCK, TOK_BLOCK), bias_map(var)) for var in variants]
                  + [pl.BlockSpec((1, NA_UNIT_HEADS, LANES, TOK_BLOCK), value_map(tok)) for tok in tok_blocks]),
        out_specs=pl.BlockSpec((1, TOK_BLOCK, NA_UNIT_HEADS * NA_DIM), out_map),
        out_shape=jax.ShapeDtypeStruct((B, N, NA_HEADS * NA_DIM), BF16),
        scratch_shapes=[pltpu.VMEM((NA_UNIT_HEADS, NA_KEYS, TOK_BLOCK), F32),
                        pltpu.VMEM((NA_UNIT_HEADS, 1, TOK_BLOCK), F32),
                        pltpu.VMEM((NA_UNIT_HEADS, NA_KEYS, TOK_BLOCK), BF16)],
        compiler_params=pltpu.CompilerParams(vmem_limit_bytes=VMEM_LIMIT),
        name="natten",
    )(nq, nk, nk, nk, nk, bias, bias, bias, nv, nv, nv, nv)


OUT_TM = 512
R_G1, R_SH2, R_SC2, R_G2 = 2, 3, 4, 5
GROUP_LANE = N_EXPERTS


def _first_max_lane(x, valid, lane):
    xm = jnp.where(valid, x, -jnp.inf)
    mx = jnp.max(xm, axis=1, keepdims=True)
    idx = jnp.min(jnp.where(valid & (xm == mx), lane, LANES), axis=1, keepdims=True)
    return mx, idx


def _outproj_kernel(om_ref, on_ref, x_ref, mod_ref, wo_ref, g_ref, wr_ref, br_ref,
                    x1_out, h2_out, comb_out, h2_scr):
    @pl.when(pl.program_id(0) == 0)
    def _():
        h2_scr[...] = jnp.zeros(h2_scr.shape, h2_scr.dtype)

    h2_prev = h2_scr[...]
    h2_hi = h2_prev.astype(BF16)
    h2_lo = (h2_prev - h2_hi.astype(F32)).astype(BF16)
    both = jnp.dot(h2_hi, wr_ref[...], preferred_element_type=F32)
    logits = (both[:, :LANES] + both[:, LANES:]
              + jnp.dot(h2_lo, wr_ref[:, :LANES], preferred_element_type=F32) + br_ref[...])
    lane = lax.broadcasted_iota(jnp.int32, logits.shape, 1)
    is_g = lane < N_GROUPS
    g_max, g_sel = _first_max_lane(logits, is_g, lane)
    g_w = 1.0 / jnp.sum(jnp.where(is_g, jnp.exp(logits - g_max), 0.0), axis=1, keepdims=True)
    e_lo = N_GROUPS + g_sel * EXPERTS_PER_GROUP
    in_grp = (lane >= e_lo) & (lane < e_lo + EXPERTS_PER_GROUP)
    m1, i1 = _first_max_lane(logits, in_grp, lane)
    m2, i2 = _first_max_lane(logits, in_grp & (lane != i1), lane)
    e2 = jnp.exp(m2 - m1)
    w1 = 1.0 / (1.0 + e2)
    w2 = e2 / (1.0 + e2)
    comb = jnp.where(lane == i1, g_w * w1, jnp.where(lane == i2, g_w * w2, 0.0))
    comb_out[...] = jnp.where(lane == GROUP_LANE, g_sel.astype(F32), pltpu.roll(comb, LANES - N_GROUPS, 1))

    half = wo_ref.shape[0] // 2
    mixed = (jnp.dot(om_ref[...], wo_ref[:half, :], preferred_element_type=F32)
             + jnp.dot(on_ref[...], wo_ref[half:, :], preferred_element_type=F32))
    x1 = x_ref[...] + mod_ref[0, R_G1:R_G1 + 1, :] * mixed
    x1_out[...] = x1
    h2 = _rms(x1, g_ref[...]) * (1.0 + mod_ref[0, R_SC2:R_SC2 + 1, :]) + mod_ref[0, R_SH2:R_SH2 + 1, :]
    h2_out[...] = h2.astype(BF16)
    h2_scr[...] = h2


def _outproj(o_mla, o_na, x2d, mod, wo, g, wr, br, n_per_batch):
    T, D = x2d.shape
    per_b = n_per_batch // OUT_TM
    nt = T // OUT_TM
    cur = lambda t: jnp.minimum(t, nt - 1)
    tok = lambda w: pl.BlockSpec((OUT_TM, w), lambda t: (cur(t), 0))
    full = lambda shape: pl.BlockSpec(shape, lambda t: (0,) * len(shape))
    return pl.pallas_call(
        _outproj_kernel,
        grid=(nt + 1,),
        in_specs=[tok(o_mla.shape[1]), tok(o_na.shape[1]), tok(D),
                  pl.BlockSpec((1, 6, D), lambda t: (cur(t) // per_b, 0, 0)),
                  full(wo.shape), full((1, D)), full(wr.shape), full(br.shape)],
        out_specs=[tok(D), tok(D), pl.BlockSpec((OUT_TM, LANES), lambda t: (jnp.maximum(t - 1, 0), 0))],
        out_shape=[jax.ShapeDtypeStruct((T, D), F32), jax.ShapeDtypeStruct((T, D), BF16),
                   jax.ShapeDtypeStruct((T, LANES), F32)],
        scratch_shapes=[pltpu.VMEM((OUT_TM, D), F32)],
        compiler_params=pltpu.CompilerParams(vmem_limit_bytes=VMEM_LIMIT),
        name="outproj",
    )(o_mla, o_na, x2d, mod, wo, g, wr, br)


MOE_TT = 1024
MOE_CH = 128
MOE_NBLK = MOE_TT // MOE_CH + (N_GROUPS - 1)
MOE_TTP = MOE_NBLK * MOE_CH


def _moe_sort_kernel(h_ref, comb_ref, xs_out, cs_out, pos_out, cnt_out, tri_scr):
    @pl.when(pl.program_id(0) == 0)
    def _():
        r = lax.broadcasted_iota(jnp.int32, tri_scr.shape, 0)
        c = lax.broadcasted_iota(jnp.int32, tri_scr.shape, 1)
        tri_scr[...] = jnp.where(c < r, 1.0, 0.0).astype(BF16)

    comb = comb_ref[...]
    lane = lax.broadcasted_iota(jnp.int32, comb.shape, 1)
    lane_f = lane.astype(F32)
    gid = jnp.sum(jnp.where(lane == GROUP_LANE, comb, 0.0), axis=1, keepdims=True)
    onehot = jnp.where((lane_f == gid) & (lane < N_GROUPS), 1.0, 0.0)
    ahead = jnp.dot(tri_scr[...], onehot.astype(BF16), preferred_element_type=F32)
    n = jnp.sum(onehot, axis=0, keepdims=True)
    padded = jnp.ceil(n * (1.0 / MOE_CH)) * MOE_CH
    start = jnp.zeros_like(padded)
    for k in range(1, N_GROUPS):
        start = start + jnp.where(lane[:1] >= k, pltpu.roll(padded, k, 1), 0.0)
    pos = jnp.sum(onehot * (ahead + start), axis=1, keepdims=True)
    pos_rep = jnp.broadcast_to(pos, comb.shape)
    pos_out[...] = pos_rep
    cnt_out[0] = jnp.broadcast_to(n, (8, LANES))

    pos_row = jnp.transpose(pos_rep)[0:1, :]
    row = lax.broadcasted_iota(jnp.int32, (MOE_TTP, MOE_TT), 0).astype(F32)
    perm = jnp.where(row == pos_row, 1.0, 0.0).astype(BF16)
    xs_out[...] = jnp.dot(perm, h_ref[...], preferred_element_type=F32).astype(BF16)
    hi = comb.astype(BF16)
    lo = (comb - hi.astype(F32)).astype(BF16)
    cs_out[...] = (jnp.dot(perm, hi, preferred_element_type=F32)
                   + jnp.dot(perm, lo, preferred_element_type=F32))


def _moe_sort(h2, comb):
    T, D = h2.shape
    nt = T // MOE_TT
    return pl.pallas_call(
        _moe_sort_kernel,
        grid=(nt,),
        in_specs=[pl.BlockSpec((MOE_TT, D), lambda t: (t, 0)),
                  pl.BlockSpec((MOE_TT, LANES), lambda t: (t, 0))],
        out_specs=[pl.BlockSpec((MOE_TTP, D), lambda t: (t, 0)),
                   pl.BlockSpec((MOE_TTP, LANES), lambda t: (t, 0)),
                   pl.BlockSpec((MOE_TT, LANES), lambda t: (t, 0)),
                   pl.BlockSpec((1, 8, LANES), lambda t: (t, 0, 0))],
        out_shape=[jax.ShapeDtypeStruct((nt * MOE_TTP, D), BF16),
                   jax.ShapeDtypeStruct((nt * MOE_TTP, LANES), F32),
                   jax.ShapeDtypeStruct((T, LANES), F32),
                   jax.ShapeDtypeStruct((nt, 8, LANES), F32)],
        scratch_shapes=[pltpu.VMEM((MOE_TT, MOE_TT), BF16)],
        compiler_params=pltpu.CompilerParams(vmem_limit_bytes=VMEM_LIMIT),
        name="moe_sort",
    )(h2, comb)


def _moe_experts_kernel(blk_ref, grp_ref, nvalid_ref, x_ref, c_ref, wg_ref, wu_ref, wd_ref, y_ref, hid_scr):
    s = pl.program_id(0)
    n_slots = pl.num_programs(0) - 1

    @pl.when(s == 0)
    def _():
        hid_scr[...] = jnp.zeros(hid_scr.shape, hid_scr.dtype)

    @pl.when(s > nvalid_ref[0])
    def _():
        y_ref[...] = jnp.zeros(y_ref.shape, y_ref.dtype)

    @pl.when(s <= nvalid_ref[0])
    def _():
        y = None
        for e in range(EXPERTS_PER_GROUP):
            part = jnp.dot(hid_scr[e], wd_ref[e], preferred_element_type=F32)
            y = part if y is None else y + part
        y_ref[...] = y.astype(y_ref.dtype)

        x = x_ref[...]
        comb = c_ref[...]
        lane = lax.broadcasted_iota(jnp.int32, comb.shape, 1)
        e0 = grp_ref[jnp.minimum(s, n_slots - 1)] * EXPERTS_PER_GROUP
        for e in range(EXPERTS_PER_GROUP):
            gate = jnp.dot(x, wg_ref[e], preferred_element_type=F32)
            up = jnp.dot(x, wu_ref[e], preferred_element_type=F32)
            w = jnp.sum(jnp.where(lane == e0 + e, comb, 0.0), axis=1, keepdims=True)
            hid_scr[e] = (gate * jax.nn.sigmoid(gate) * up * w).astype(BF16)


def _moe_experts(blk, grp, nvalid, xs, cs, wg, wu, wd):
    D = xs.shape[1]
    n_slots = blk.shape[0]

    def cur(arr):
        return lambda s, blk, grp, nvalid: (arr(blk, grp)[jnp.minimum(s, n_slots - 1)],)

    def prev(arr):
        return lambda s, blk, grp, nvalid: (arr(blk, grp)[jnp.maximum(s - 1, 0)],)

    blk_of = lambda blk, grp: blk
    grp_of = lambda blk, grp: grp
    rows = lambda f: (lambda *a: f(*a) + (0,))
    wts = lambda f: (lambda *a: f(*a) + (0, 0))
    return pl.pallas_call(
        _moe_experts_kernel,
        grid_spec=pltpu.PrefetchScalarGridSpec(
            num_scalar_prefetch=3,
            grid=(n_slots + 1,),
            in_specs=[pl.BlockSpec((MOE_CH, D), rows(cur(blk_of))),
                      pl.BlockSpec((MOE_CH, LANES), rows(cur(blk_of))),
                      pl.BlockSpec((EXPERTS_PER_GROUP, D, D_EXPERT), wts(cur(grp_of))),
                      pl.BlockSpec((EXPERTS_PER_GROUP, D, D_EXPERT), wts(cur(grp_of))),
                      pl.BlockSpec((EXPERTS_PER_GROUP, D_EXPERT, D), wts(prev(grp_of)))],
            out_specs=pl.BlockSpec((MOE_CH, D), rows(prev(blk_of))),
            scratch_shapes=[pltpu.VMEM((EXPERTS_PER_GROUP, MOE_CH, D_EXPERT), BF16)],
        ),
        out_shape=jax.ShapeDtypeStruct(xs.shape, BF16),
        compiler_params=pltpu.CompilerParams(vmem_limit_bytes=VMEM_LIMIT),
        name="moe_experts",
    )(blk, grp, nvalid, xs, cs, wg, wu, wd)


def _moe_final_kernel(ys_ref, pos_ref, x1_ref, mod_ref, fg_ref, o_ref):
    col = lax.broadcasted_iota(jnp.int32, (MOE_TT, MOE_TTP), 1).astype(F32)
    unperm = jnp.where(col == pltpu.repeat(pos_ref[...], MOE_NBLK, axis=1), 1.0, 0.0).astype(BF16)
    y = jnp.dot(unperm, ys_ref[...], preferred_element_type=F32)
    o_ref[...] = _rms(x1_ref[...] + mod_ref[0, R_G2:R_G2 + 1, :] * y, fg_ref[...])


def _moe_final(ys, pos, x1, mod, fg, n_per_batch):
    T, D = x1.shape
    per_b = n_per_batch // MOE_TT
    return pl.pallas_call(
        _moe_final_kernel,
        grid=(T // MOE_TT,),
        in_specs=[pl.BlockSpec((MOE_TTP, D), lambda t: (t, 0)),
                  pl.BlockSpec((MOE_TT, LANES), lambda t: (t, 0)),
                  pl.BlockSpec((MOE_TT, D), lambda t: (t, 0)),
                  pl.BlockSpec((1, 6, D), lambda t: (t // per_b, 0, 0)),
                  pl.BlockSpec((1, D), lambda t: (0, 0))],
        out_specs=pl.BlockSpec((MOE_TT, D), lambda t: (t, 0)),
        out_shape=jax.ShapeDtypeStruct((T, D), F32),
        compiler_params=pltpu.CompilerParams(vmem_limit_bytes=VMEM_LIMIT),
        name="moe_final",
    )(ys, pos, x1, mod, fg)


def _chunk_schedule(counts):
    nt = counts.shape[0]
    nchunks = jnp.ceil(counts * (1.0 / MOE_CH)).astype(jnp.int32)
    ends = jnp.cumsum(nchunks, axis=1)
    j = jnp.arange(MOE_NBLK, dtype=jnp.int32)
    grp = jnp.sum(j[None, :, None] >= ends[:, None, :], axis=2).astype(jnp.int32).reshape(-1)
    blk = (jnp.arange(nt, dtype=jnp.int32)[:, None] * MOE_NBLK + j[None, :]).reshape(-1)
    order = jnp.argsort(grp, stable=True)
    nvalid = jnp.sum(grp < N_GROUPS).astype(jnp.int32).reshape(1)
    return blk[order], jnp.minimum(grp, N_GROUPS - 1)[order], nvalid


def _rope_tables(n):
    pos = np.arange(n)
    half = MLA_ROPE // 2
    inv_freq = (ROPE_THETA ** (-np.arange(0, half, 2, dtype=np.float32) / half)).astype(np.float32)
    ang = np.concatenate([(pos // GRID_W).astype(np.float32)[:, None] * inv_freq,
                          (pos % GRID_W).astype(np.float32)[:, None] * inv_freq], axis=-1)
    cos, sin = np.cos(ang).astype(np.float32), np.sin(ang).astype(np.float32)
    one = np.ones((n, MLA_NOPE), np.float32)
    zero = np.zeros((n, MLA_NOPE), np.float32)
    pad = np.zeros((n, LANES - MLA_QK), np.float32)
    z16 = np.zeros((n, half), np.float32)
    c = np.concatenate([one, cos, cos, pad], axis=1)
    sa = np.concatenate([zero, -sin, z16, pad], axis=1)
    sb = np.concatenate([zero, z16, sin, pad], axis=1)
    ident = np.concatenate([np.ones((CTX_LEN, MLA_QK), np.float32),
                            np.zeros((CTX_LEN, LANES - MLA_QK), np.float32)], axis=1)
    zc = np.zeros((CTX_LEN, LANES), np.float32)
    return (np.concatenate([ident, c], axis=0), np.concatenate([zc, sa], axis=0),
            np.concatenate([zc, sb], axis=0))


def kernel(x, c, ctx, c_ctx, w_mod, b_mod, norm_attn_g, norm_ffn_g, w_in, q_a_norm_g, kv_a_norm_g, w_uq, w_ukv,
           na_rel_bias, w_out, w_router_group, b_router_group, w_router_expert, b_router_expert, w_gate, w_up,
           w_down, final_norm_g):
    B, N, D = x.shape
    assert w_mod.shape[0] == 1, "single-layer kernel"
    assert (B, N, D) == (2, 8192, D_MODEL) and ctx.shape == (B, CTX_LEN, D)

    cond8 = jnp.concatenate([c, c_ctx[None, :], jnp.zeros((8 - B - 1, D), F32)], axis=0)
    mod = _adaln(cond8, w_mod[0], b_mod[0]).reshape(8, 6, D)

    wi = w_in[0]
    na_w = NA_HEADS * NA_DIM
    k0 = Q_LORA + na_w + KV_LORA
    win = jnp.concatenate([wi[:, :k0], jnp.zeros((D, MLA_NOPE), F32), wi[:, k0:k0 + MLA_ROPE],
                           jnp.zeros((D, LANES - MLA_QK), F32), wi[:, k0 + MLA_ROPE:]], axis=1).astype(BF16)
    assert win.shape[1] == C_END
    wuq = jnp.pad(w_uq[0].reshape(Q_LORA, MLA_HEADS, MLA_QK), ((0, 0), (0, 0), (0, LANES - MLA_QK)))
    wuq = wuq.reshape(Q_LORA, MLA_HEADS * LANES).astype(BF16)
    wkv = w_ukv[0].reshape(KV_LORA, MLA_HEADS, MLA_NOPE + MLA_V)
    wk = jnp.pad(wkv[:, :, :MLA_NOPE], ((0, 0), (0, 0), (0, LANES - MLA_NOPE))).reshape(KV_LORA, MLA_HEADS * LANES)
    wv = jnp.pad(wkv[:, :, MLA_NOPE:], ((0, 0), (0, 0), (0, LANES - MLA_V))).reshape(KV_LORA, MLA_HEADS * LANES)
    wukv = jnp.concatenate([wk, wv], axis=1).astype(BF16)
    rc, rsa, rsb = _rope_tables(N)

    q_m, k_m, v_m, nq, nk, nv = _proj(x, ctx, mod, norm_attn_g, win, q_a_norm_g, kv_a_norm_g, wuq, wukv,
                                      rc, rsa, rsb)
    o_mla = _mla(q_m, k_m, v_m)
    o_na = _natten(nq, nk, nv, _na_bias(na_rel_bias[0]))

    n_r = N_GROUPS + N_EXPERTS
    wr = jnp.pad(jnp.concatenate([w_router_group[0], w_router_expert[0]], axis=1), ((0, 0), (0, LANES - n_r)))
    wr_hi = lax.bitcast_convert_type(lax.bitcast_convert_type(wr, jnp.uint32) & jnp.uint32(0xFFFF0000), F32)
    wr = jnp.concatenate([wr_hi.astype(BF16), (wr - wr_hi).astype(BF16)], axis=1)
    br = jnp.pad(jnp.concatenate([b_router_group[0], b_router_expert[0]]), (0, LANES - n_r)).reshape(1, LANES)
    T = B * N
    x1, h2, comb = _outproj(o_mla.reshape(T, -1), o_na.reshape(T, -1), x.reshape(T, D), mod,
                            w_out[0].astype(BF16), norm_ffn_g, wr, br, N)
    xs, cs, pos, counts = _moe_sort(h2, comb)
    blk, grp, nvalid = _chunk_schedule(counts[:, 0, :N_GROUPS])
    ys = _moe_experts(blk, grp, nvalid, xs, cs, w_gate[0].astype(BF16), w_up[0].astype(BF16),
                      w_down[0].astype(BF16))
    out = _moe_final(ys, pos, x1, mod, final_norm_g.reshape(1, D), N)
    return out.reshape(B, N, D)
```

```python
import functools

import jax
import jax.numpy as jnp
import numpy as np
from jax import lax
from jax.experimental import pallas as pl
from jax.experimental.pallas import tpu as pltpu

D_MODEL = 1024
GRID_W = 64
CTX_LEN = 256
MLA_HEADS = 8
MLA_NOPE = 64
MLA_ROPE = 32
MLA_QK = MLA_NOPE + MLA_ROPE
MLA_V = 64
Q_LORA = 384
KV_LORA = 256
NA_HEADS = 8
NA_DIM = 64
NA_WIN_H = 8
NA_WIN_W = 16
N_GROUPS = 4
EXPERTS_PER_GROUP = 4
N_EXPERTS = N_GROUPS * EXPERTS_PER_GROUP
D_EXPERT = 512
ROPE_THETA = 10000.0
NORM_EPS = 1e-6
MASK_VALUE = -1e30
LOG2E = 1.4426950408889634

LANES = 128
MXU_COLS = 256
HEAD_PAIRS = 4
TOK_BLOCK = 256
ROWS_PER_BLOCK = TOK_BLOCK // GRID_W
VMEM_LIMIT = 48 * 1024 * 1024

F32 = jnp.float32
BF16 = jnp.bfloat16


def _rms(x, g):
    return x * lax.rsqrt(jnp.mean(x * x, axis=-1, keepdims=True) + NORM_EPS) * g


def _adaln_kernel(c_ref, w_ref, b_ref, o_ref):
    c = c_ref[...]
    s = c * jax.nn.sigmoid(c)
    o_ref[...] = jnp.dot(s, w_ref[...], precision=lax.Precision.HIGHEST,
                         preferred_element_type=F32) + b_ref[...]


def _adaln(cond8, w_mod, b_mod):
    n_out = w_mod.shape[1]
    bn = 1024
    return pl.pallas_call(
        _adaln_kernel,
        grid=(n_out // bn,),
        in_specs=[pl.BlockSpec((8, D_MODEL), lambda n: (0, 0)),
                  pl.BlockSpec((D_MODEL, bn), lambda n: (0, n)),
                  pl.BlockSpec((1, bn), lambda n: (0, n))],
        out_specs=pl.BlockSpec((8, bn), lambda n: (0, n)),
        out_shape=jax.ShapeDtypeStruct((8, n_out), F32),
        name="adaln",
    )(cond8, w_mod, b_mod.reshape(1, n_out))


NA_VARIANTS = ((-1, False), (0, False), (1, False), (0, True), (1, True), (-1, True), (0, None))
V_PREV, V_CUR, V_NEXT, V_CUR_ALL, V_NEXT_ALL, V_PREV_ALL, V_NONE = range(7)
N_DR = 2 * NA_WIN_H - 1
N_DC = 2 * NA_WIN_W - 1


def _na_bias_kernel(rel_ref, o_ref):
    h = pl.program_id(0)
    kc = lax.broadcasted_iota(jnp.int32, (GRID_W, TOK_BLOCK), 0)
    qq = lax.broadcasted_iota(jnp.int32, (GRID_W, TOK_BLOCK), 1)
    qc = qq % GRID_W
    qblk = qq // GRID_W
    col_start = jnp.clip(qc - NA_WIN_W // 2, 0, GRID_W - NA_WIN_W)
    col_in = (kc >= col_start) & (kc < col_start + NA_WIN_W)
    dc = kc - qc + (NA_WIN_W - 1)
    tables = []
    for d in range(N_DR):
        t = jnp.full((GRID_W, TOK_BLOCK), MASK_VALUE, F32)
        base = (h * N_DR + d) * N_DC
        for m in range(N_DC):
            t = jnp.where(dc == m, rel_ref[base + m] * LOG2E, t)
        tables.append(jnp.where(col_in, t, MASK_VALUE))
    for v, (boff, all_valid) in enumerate(NA_VARIANTS):
        for cb in range(ROWS_PER_BLOCK):
            tile = jnp.full((GRID_W, TOK_BLOCK), MASK_VALUE, F32)
            if all_valid is not None:
                for a in range(ROWS_PER_BLOCK):
                    dr = ROWS_PER_BLOCK * boff + cb - a
                    if all_valid or (-(NA_WIN_H // 2) <= dr < NA_WIN_H - NA_WIN_H // 2):
                        tile = jnp.where(qblk == a, tables[dr + NA_WIN_H - 1], tile)
            o_ref[v, 0, cb * GRID_W:(cb + 1) * GRID_W, :] = tile


def _na_bias(rel_bias):
    return pl.pallas_call(
        _na_bias_kernel,
        grid=(NA_HEADS,),
        in_specs=[pl.BlockSpec(memory_space=pltpu.SMEM)],
        out_specs=pl.BlockSpec((len(NA_VARIANTS), 1, TOK_BLOCK, TOK_BLOCK), lambda h: (0, h, 0, 0)),
        out_shape=jax.ShapeDtypeStruct((len(NA_VARIANTS), NA_HEADS, TOK_BLOCK, TOK_BLOCK), F32),
        name="na_bias",
    )(rel_bias.reshape(-1))


C_QLAT = 0
C_NAQ = Q_LORA
C_KVLAT = C_NAQ + NA_HEADS * NA_DIM
C_KROPE = C_KVLAT + KV_LORA
C_NAK = C_KROPE + LANES
C_NAV = C_NAK + NA_HEADS * NA_DIM
C_END = C_NAV + NA_HEADS * NA_DIM
MLA_SCALE = MLA_QK ** -0.5 * LOG2E
NA_SCALE = NA_DIM ** -0.5 * LOG2E
ONES_LANE = MLA_V


def _rope(x, c, sa, sb):
    return x * c + pltpu.roll(x, LANES - MLA_ROPE // 2, 1) * sa + pltpu.roll(x, MLA_ROPE // 2, 1) * sb


def _proj_kernel(x_ref, ctx_ref, mod_ref, g_ref, win_ref, qg_ref, kvg_ref, wuq_ref, wukv_ref,
                 c_ref, sa_ref, sb_ref,
                 q_out, k_out, v_out, nq_out, nk_out, nv_out, h_scr, proj_scr):
    j = pl.program_id(1)

    @pl.when(j == 0)
    def _():
        h_scr[...] = jnp.zeros(h_scr.shape, h_scr.dtype)
        proj_scr[...] = jnp.zeros(proj_scr.shape, proj_scr.dtype)

    proj = proj_scr
    c, sa, sb = c_ref[...], sa_ref[...], sb_ref[...]

    qn = _rms(proj[:, C_QLAT:C_NAQ], qg_ref[...]).astype(BF16)
    q = jnp.dot(qn, wuq_ref[...], preferred_element_type=F32) * MLA_SCALE
    for h in range(MLA_HEADS):
        q_out[0, h] = _rope(q[:, h * LANES:(h + 1) * LANES], c, sa, sb).astype(BF16)

    kvn = _rms(proj[:, C_KVLAT:C_KROPE], kvg_ref[...]).astype(BF16)
    kv = jnp.dot(kvn, wukv_ref[...], preferred_element_type=F32)
    kr = _rope(proj[:, C_KROPE:C_NAK], c, sa, sb)
    v0 = MLA_HEADS * LANES
    lane = lax.broadcasted_iota(jnp.int32, (TOK_BLOCK, LANES), 1)
    for h in range(MLA_HEADS):
        k_out[0, h] = (kv[:, h * LANES:(h + 1) * LANES] + kr).astype(BF16)
        v = jnp.where(lane == ONES_LANE, 1.0, kv[:, v0 + h * LANES:v0 + (h + 1) * LANES])
        v_out[0, h] = jnp.transpose(v).astype(BF16)
    for p in range(HEAD_PAIRS):
        nq_out[0, p] = (proj[:, C_NAQ + p * LANES:C_NAQ + (p + 1) * LANES] * NA_SCALE).astype(BF16)
        nk_out[0, p] = proj[:, C_NAK + p * LANES:C_NAK + (p + 1) * LANES].astype(BF16)
        nv_pair = proj[:, C_NAV + p * LANES:C_NAV + (p + 1) * LANES]
        for half in range(2):
            nv = pltpu.roll(nv_pair, NA_DIM, 1) if half else nv_pair
            nv = jnp.where(lane < NA_DIM, nv, jnp.where(lane == ONES_LANE, 1.0, 0.0))
            nv_out[0, 2 * p + half] = jnp.transpose(nv).astype(BF16)

    proj_scr[...] = jnp.dot(h_scr[...], win_ref[...], preferred_element_type=F32)
    xt = jnp.where(j == 0, ctx_ref[0], x_ref[0])
    h_scr[...] = (_rms(xt, g_ref[...]) * (1.0 + mod_ref[0, 1:2, :]) + mod_ref[0, 0:1, :]).astype(BF16)


def _proj(x, ctx, mod, g, win, qg, kvg, wuq, wukv, rc, rsa, rsb):
    B, N, D = x.shape
    nb = N // TOK_BLOCK
    ntot = N + CTX_LEN
    full = lambda shape: pl.BlockSpec(shape, lambda b, j: (0,) * len(shape))
    depth = 2
    tile_in = lambda j: jnp.minimum(j, nb)
    tile_out = lambda j: jnp.clip(j - depth, 0, nb)
    lat = lambda b, j: (b, 0, jnp.maximum(tile_out(j) - 1, 0), 0)
    alltok = lambda b, j: (b, 0, tile_out(j), 0)
    feat = lambda b, j: (b, 0, 0, tile_out(j))
    rope = lambda b, j: (tile_out(j), 0)
    return pl.pallas_call(
        _proj_kernel,
        grid=(B, nb + 1 + depth),
        in_specs=[pl.BlockSpec((1, TOK_BLOCK, D), lambda b, j: (b, jnp.maximum(tile_in(j) - 1, 0), 0)),
                  pl.BlockSpec((1, CTX_LEN, D), lambda b, j: (b, 0, 0)),
                  pl.BlockSpec((1, 6, D), lambda b, j: (jnp.where(j == 0, B, b), 0, 0)),
                  full((1, D)), full(win.shape), full((1, Q_LORA)), full((1, KV_LORA)),
                  full(wuq.shape), full(wukv.shape),
                  pl.BlockSpec((TOK_BLOCK, LANES), rope),
                  pl.BlockSpec((TOK_BLOCK, LANES), rope),
                  pl.BlockSpec((TOK_BLOCK, LANES), rope)],
        out_specs=[pl.BlockSpec((1, MLA_HEADS, TOK_BLOCK, LANES), lat),
                   pl.BlockSpec((1, MLA_HEADS, TOK_BLOCK, LANES), alltok),
                   pl.BlockSpec((1, MLA_HEADS, LANES, TOK_BLOCK), feat),
                   pl.BlockSpec((1, HEAD_PAIRS, TOK_BLOCK, LANES), lat),
                   pl.BlockSpec((1, HEAD_PAIRS, TOK_BLOCK, LANES), alltok),
                   pl.BlockSpec((1, NA_HEADS, LANES, TOK_BLOCK), feat)],
        out_shape=[jax.ShapeDtypeStruct((B, MLA_HEADS, N, LANES), BF16),
                   jax.ShapeDtypeStruct((B, MLA_HEADS, ntot, LANES), BF16),
                   jax.ShapeDtypeStruct((B, MLA_HEADS, LANES, ntot), BF16),
                   jax.ShapeDtypeStruct((B, HEAD_PAIRS, N, LANES), BF16),
                   jax.ShapeDtypeStruct((B, HEAD_PAIRS, ntot, LANES), BF16),
                   jax.ShapeDtypeStruct((B, NA_HEADS, LANES, ntot), BF16)],
        scratch_shapes=[pltpu.VMEM((TOK_BLOCK, D), BF16), pltpu.VMEM((TOK_BLOCK, C_END), F32)],
        compiler_params=pltpu.CompilerParams(vmem_limit_bytes=VMEM_LIMIT),
        name="proj",
    )(x, ctx, mod, g, win, qg, kvg, wuq, wukv, rc, rsa, rsb)


MLA_TQ = 1024
MLA_TK = 768


def _mla_kernel(q_ref, k_ref, v_ref, o_ref, m_scr, alpha_scr, acc_scr, p_scr, s_scr, smax_scr):
    n_chunks = k_ref.shape[2] // MLA_TK
    m_scr[...] = jnp.full(m_scr.shape, -jnp.inf, F32)
    acc_scr[...] = jnp.zeros(acc_scr.shape, F32)

    n_qb = MLA_TQ // MXU_COLS

    def scores(ci, hh, n):
        off = pl.multiple_of(ci * MLA_TK, MLA_TK)
        k = k_ref[0, hh, pl.ds(off, MLA_TK), :]
        q = q_ref[0, hh, n * MXU_COLS:(n + 1) * MXU_COLS, :]
        s = lax.dot_general(k, q, (((1,), (1,)), ((), ())), preferred_element_type=F32)
        s_scr[hh, n] = s
        smax_scr[hh, n] = jnp.max(s, axis=0, keepdims=True)

    def softmax(hh, n):
        m_prev = m_scr[hh, n]
        m_next = jnp.maximum(m_prev, smax_scr[hh, n])
        p_scr[hh, n] = jnp.exp2((s_scr[hh, n] - m_next).astype(BF16))
        alpha_scr[hh, n] = jnp.exp2(m_prev - m_next)
        m_scr[hh, n] = m_next

    def values(ci, hh, n):
        off = pl.multiple_of(ci * MLA_TK, MLA_TK)
        vt = v_ref[0, hh, :, pl.ds(off, MLA_TK)]
        acc_scr[hh, n] = alpha_scr[hh, n] * acc_scr[hh, n] + jnp.dot(vt, p_scr[hh, n], preferred_element_type=F32)

    blocks = [(hh, n) for hh in range(2) for n in range(n_qb)]
    for hh, n in blocks:
        scores(0, hh, n)
    for hh, n in blocks:
        softmax(hh, n)
        scores(1, hh, n)

    def body(ci, carry):
        for hh, n in blocks:
            values(ci - 1, hh, n)
            softmax(hh, n)
            scores(ci + 1, hh, n)
        return carry

    lax.fori_loop(1, n_chunks - 1, body, 0)
    for hh, n in blocks:
        values(n_chunks - 2, hh, n)
        softmax(hh, n)
    for hh, n in blocks:
        values(n_chunks - 1, hh, n)
    outs = []
    for hh in range(2):
        acc = jnp.concatenate([acc_scr[hh, n] for n in range(n_qb)], axis=1)
        outs.append(jnp.transpose(acc / acc[ONES_LANE:ONES_LANE + 1, :]))
    lane = lax.broadcasted_iota(jnp.int32, (MLA_TQ, LANES), 1)
    o_ref[0] = jnp.where(lane < MLA_V, outs[0], pltpu.roll(outs[1], MLA_V, 1)).astype(o_ref.dtype)


def _mla(q_m, k_m, v_m):
    B, _, N, _ = q_m.shape
    ntot = k_m.shape[2]
    n_qb = MLA_TQ // MXU_COLS
    return pl.pallas_call(
        _mla_kernel,
        grid=(B, HEAD_PAIRS, N // MLA_TQ),
        in_specs=[pl.BlockSpec((1, 2, MLA_TQ, LANES), lambda b, p, i: (b, p, i, 0)),
                  pl.BlockSpec((1, 2, ntot, LANES), lambda b, p, i: (b, p, 0, 0)),
                  pl.BlockSpec((1, 2, LANES, ntot), lambda b, p, i: (b, p, 0, 0))],
        out_specs=pl.BlockSpec((1, MLA_TQ, LANES), lambda b, p, i: (b, i, p)),
        out_shape=jax.ShapeDtypeStruct((B, N, MLA_HEADS * MLA_V), BF16),
        scratch_shapes=[pltpu.VMEM((2, n_qb, 1, MXU_COLS), F32), pltpu.VMEM((2, n_qb, 1, MXU_COLS), F32),
                        pltpu.VMEM((2, n_qb, LANES, MXU_COLS), F32), pltpu.VMEM((2, n_qb, MLA_TK, MXU_COLS), BF16),
                        pltpu.VMEM((2, n_qb, MLA_TK, MXU_COLS), F32), pltpu.VMEM((2, n_qb, 1, MXU_COLS), F32)],
        compiler_params=pltpu.CompilerParams(vmem_limit_bytes=VMEM_LIMIT),
        name="mla",
    )(q_m, k_m, v_m)


NA_UNIT_HEADS = 8
NA_UNITS_PER_BLOCK = NA_HEADS // NA_UNIT_HEADS
NA_KEYS = 4 * TOK_BLOCK


def _na_kernel(q_ref, kc_ref, kp_ref, k0_ref, kn_ref, bp_ref, b0_ref, bn_ref,
               vc_ref, vp_ref, v0_ref, vn_ref, o_ref, s_scr, smax_scr, p_scr):
    @pl.when(pl.program_id(1) == 0)
    def _():
        s_scr[...] = jnp.zeros(s_scr.shape, F32)
        smax_scr[...] = jnp.zeros(smax_scr.shape, F32)
        p_scr[...] = jnp.zeros(p_scr.shape, BF16)

    lane = lax.broadcasted_iota(jnp.int32, (TOK_BLOCK, LANES), 1)

    for pp in range(NA_UNIT_HEADS // 2):
        outs = []
        for half in range(2):
            hq = 2 * pp + half
            vt = jnp.concatenate([vc_ref[0, hq], vp_ref[0, hq], v0_ref[0, hq], vn_ref[0, hq]], axis=1)
            o = jnp.dot(vt, p_scr[hq], preferred_element_type=F32)
            outs.append(jnp.transpose(o / o[ONES_LANE:ONES_LANE + 1, :]))
        o_ref[0, :, pp * LANES:(pp + 1) * LANES] = jnp.where(
            lane < NA_DIM, outs[0], pltpu.roll(outs[1], NA_DIM, 1)).astype(o_ref.dtype)

    for hq in range(NA_UNIT_HEADS):
        p_scr[hq] = jnp.exp2((s_scr[hq] - smax_scr[hq]).astype(BF16))

    dn = (((1,), (1,)), ((), ()))
    for pp in range(NA_UNIT_HEADS // 2):
        keys = jnp.concatenate([kc_ref[0, pp], kp_ref[0, pp], k0_ref[0, pp], kn_ref[0, pp]], axis=0)
        for half in range(2):
            hq = 2 * pp + half
            in_half = (lane >= NA_DIM) if half else (lane < NA_DIM)
            q = jnp.where(in_half, q_ref[0, pp], jnp.zeros((), BF16))
            s = lax.dot_general(keys, q, dn, preferred_element_type=F32)
            parts = [s[:TOK_BLOCK],
                     s[TOK_BLOCK:2 * TOK_BLOCK] + bp_ref[0, hq],
                     s[2 * TOK_BLOCK:3 * TOK_BLOCK] + b0_ref[0, hq],
                     s[3 * TOK_BLOCK:] + bn_ref[0, hq]]
            for j, part in enumerate(parts):
                s_scr[hq, j * TOK_BLOCK:(j + 1) * TOK_BLOCK, :] = part
            smax_scr[hq] = jnp.max(jnp.maximum(jnp.maximum(parts[0], parts[1]),
                                               jnp.maximum(parts[2], parts[3])), axis=0, keepdims=True)


def _natten(nq, nk, nv, bias):
    B, _, N, _ = nq.shape
    nb = N // TOK_BLOCK
    last = nb - 1
    n_units = nb * NA_UNITS_PER_BLOCK
    pipeline_depth = 2

    def unit(t, lag):
        u = jnp.clip(t - lag, 0, n_units - 1)
        return u // NA_UNITS_PER_BLOCK, u % NA_UNITS_PER_BLOCK

    tok_blocks = (lambda i: 0, lambda i: jnp.maximum(i - 1, 0) + 1, lambda i: i + 1,
                  lambda i: jnp.minimum(i + 1, last) + 1)
    variants = (lambda i: jnp.where(i == 0, V_NONE, jnp.where(i == last, V_PREV_ALL, V_PREV)),
                lambda i: jnp.where((i == 0) | (i == last), V_CUR_ALL, V_CUR),
                lambda i: jnp.where(i == 0, V_NEXT_ALL, jnp.where(i == last, V_NONE, V_NEXT)))

    def q_map(b, t):
        i, g = unit(t, 0)
        return b, g, i, 0

    def key_map(tok):
        def f(b, t):
            i, g = unit(t, 0)
            return b, g, tok(i), 0
        return f

    def bias_map(var):
        def f(b, t):
            i, g = unit(t, 0)
            return var(i), g, 0, 0
        return f

    def value_map(tok):
        def f(b, t):
            i, g = unit(t, pipeline_depth)
            return b, g, 0, tok(i)
        return f

    def out_map(b, t):
        i, g = unit(t, pipeline_depth)
        return b, i, g

    pair_blk = (1, NA_UNIT_HEADS // 2, TOK_BLOCK, LANES)
    return pl.pallas_call(
        _na_kernel,
        grid=(B, n_units + pipeline_depth),
        in_specs=([pl.BlockSpec(pair_blk, q_map)]
                  + [pl.BlockSpec(pair_blk, key_map(tok)) for tok in tok_blocks]
                  + [pl.BlockSpec((1, NA_UNIT_HEADS, TOK_BLOCK, TOK_BLOCK), bias_map(var)) for var in variants]
                  + [pl.BlockSpec((1, NA_UNIT_HEADS, LANES, TOK_BLOCK), value_map(tok)) for tok in tok_blocks]),
        out_specs=pl.BlockSpec((1, TOK_BLOCK, NA_UNIT_HEADS * NA_DIM), out_map),
        out_shape=jax.ShapeDtypeStruct((B, N, NA_HEADS * NA_DIM), BF16),
        scratch_shapes=[pltpu.VMEM((NA_UNIT_HEADS, NA_KEYS, TOK_BLOCK), F32),
                        pltpu.VMEM((NA_UNIT_HEADS, 1, TOK_BLOCK), F32),
                        pltpu.VMEM((NA_UNIT_HEADS, NA_KEYS, TOK_BLOCK), BF16)],
        compiler_params=pltpu.CompilerParams(vmem_limit_bytes=VMEM_LIMIT),
        name="natten",
    )(nq, nk, nk, nk, nk, bias, bias, bias, nv, nv, nv, nv)


OUT_TM = 512
R_G1, R_SH2, R_SC2, R_G2 = 2, 3, 4, 5
GROUP_LANE = N_EXPERTS


def _first_max_lane(x, valid, lane):
    xm = jnp.where(valid, x, -jnp.inf)
    mx = jnp.max(xm, axis=1, keepdims=True)
    idx = jnp.min(jnp.where(valid & (xm == mx), lane, LANES), axis=1, keepdims=True)
    return mx, idx


def _outproj_kernel(om_ref, on_ref, x_ref, mod_ref, wo_ref, g_ref, wr_ref, br_ref,
                    x1_out, h2_out, comb_out, h2_scr):
    @pl.when(pl.program_id(0) == 0)
    def _():
        h2_scr[...] = jnp.zeros(h2_scr.shape, h2_scr.dtype)

    h2_prev = h2_scr[...]
    h2_hi = h2_prev.astype(BF16)
    h2_lo = (h2_prev - h2_hi.astype(F32)).astype(BF16)
    both = jnp.dot(h2_hi, wr_ref[...], preferred_element_type=F32)
    logits = (both[:, :LANES] + both[:, LANES:]
              + jnp.dot(h2_lo, wr_ref[:, :LANES], preferred_element_type=F32) + br_ref[...])
    lane = lax.broadcasted_iota(jnp.int32, logits.shape, 1)
    is_g = lane < N_GROUPS
    g_max, g_sel = _first_max_lane(logits, is_g, lane)
    g_w = 1.0 / jnp.sum(jnp.where(is_g, jnp.exp(logits - g_max), 0.0), axis=1, keepdims=True)
    e_lo = N_GROUPS + g_sel * EXPERTS_PER_GROUP
    in_grp = (lane >= e_lo) & (lane < e_lo + EXPERTS_PER_GROUP)
    m1, i1 = _first_max_lane(logits, in_grp, lane)
    m2, i2 = _first_max_lane(logits, in_grp & (lane != i1), lane)
    e2 = jnp.exp(m2 - m1)
    w1 = 1.0 / (1.0 + e2)
    w2 = e2 / (1.0 + e2)
    comb = jnp.where(lane == i1, g_w * w1, jnp.where(lane == i2, g_w * w2, 0.0))
    comb_out[...] = jnp.where(lane == GROUP_LANE, g_sel.astype(F32), pltpu.roll(comb, LANES - N_GROUPS, 1))

    half = wo_ref.shape[0] // 2
    mixed = (jnp.dot(om_ref[...], wo_ref[:half, :], preferred_element_type=F32)
             + jnp.dot(on_ref[...], wo_ref[half:, :], preferred_element_type=F32))
    x1 = x_ref[...] + mod_ref[0, R_G1:R_G1 + 1, :] * mixed
    x1_out[...] = x1
    h2 = _rms(x1, g_ref[...]) * (1.0 + mod_ref[0, R_SC2:R_SC2 + 1, :]) + mod_ref[0, R_SH2:R_SH2 + 1, :]
    h2_out[...] = h2.astype(BF16)
    h2_scr[...] = h2


def _outproj(o_mla, o_na, x2d, mod, wo, g, wr, br, n_per_batch):
    T, D = x2d.shape
    per_b = n_per_batch // OUT_TM
    nt = T // OUT_TM
    cur = lambda t: jnp.minimum(t, nt - 1)
    tok = lambda w: pl.BlockSpec((OUT_TM, w), lambda t: (cur(t), 0))
    full = lambda shape: pl.BlockSpec(shape, lambda t: (0,) * len(shape))
    return pl.pallas_call(
        _outproj_kernel,
        grid=(nt + 1,),
        in_specs=[tok(o_mla.shape[1]), tok(o_na.shape[1]), tok(D),
                  pl.BlockSpec((1, 6, D), lambda t: (cur(t) // per_b, 0, 0)),
                  full(wo.shape), full((1, D)), full(wr.shape), full(br.shape)],
        out_specs=[tok(D), tok(D), pl.BlockSpec((OUT_TM, LANES), lambda t: (jnp.maximum(t - 1, 0), 0))],
        out_shape=[jax.ShapeDtypeStruct((T, D), F32), jax.ShapeDtypeStruct((T, D), BF16),
                   jax.ShapeDtypeStruct((T, LANES), F32)],
        scratch_shapes=[pltpu.VMEM((OUT_TM, D), F32)],
        compiler_params=pltpu.CompilerParams(vmem_limit_bytes=VMEM_LIMIT),
        name="outproj",
    )(o_mla, o_na, x2d, mod, wo, g, wr, br)


MOE_TT = 1024
MOE_CH = 128
MOE_NBLK = MOE_TT // MOE_CH + (N_GROUPS - 1)
MOE_TTP = MOE_NBLK * MOE_CH


def _moe_sort_kernel(h_ref, comb_ref, xs_out, cs_out, pos_out, cnt_out, tri_scr):
    @pl.when(pl.program_id(0) == 0)
    def _():
        r = lax.broadcasted_iota(jnp.int32, tri_scr.shape, 0)
        c = lax.broadcasted_iota(jnp.int32, tri_scr.shape, 1)
        tri_scr[...] = jnp.where(c < r, 1.0, 0.0).astype(BF16)

    comb = comb_ref[...]
    lane = lax.broadcasted_iota(jnp.int32, comb.shape, 1)
    lane_f = lane.astype(F32)
    gid = jnp.sum(jnp.where(lane == GROUP_LANE, comb, 0.0), axis=1, keepdims=True)
    onehot = jnp.where((lane_f == gid) & (lane < N_GROUPS), 1.0, 0.0)
    ahead = jnp.dot(tri_scr[...], onehot.astype(BF16), preferred_element_type=F32)
    n = jnp.sum(onehot, axis=0, keepdims=True)
    padded = jnp.ceil(n * (1.0 / MOE_CH)) * MOE_CH
    start = jnp.zeros_like(padded)
    for k in range(1, N_GROUPS):
        start = start + jnp.where(lane[:1] >= k, pltpu.roll(padded, k, 1), 0.0)
    pos = jnp.sum(onehot * (ahead + start), axis=1, keepdims=True)
    pos_rep = jnp.broadcast_to(pos, comb.shape)
    pos_out[...] = pos_rep
    cnt_out[0] = jnp.broadcast_to(n, (8, LANES))

    pos_row = jnp.transpose(pos_rep)[0:1, :]
    row = lax.broadcasted_iota(jnp.int32, (MOE_TTP, MOE_TT), 0).astype(F32)
    perm = jnp.where(row == pos_row, 1.0, 0.0).astype(BF16)
    xs_out[...] = jnp.dot(perm, h_ref[...], preferred_element_type=F32).astype(BF16)
    hi = comb.astype(BF16)
    lo = (comb - hi.astype(F32)).astype(BF16)
    cs_out[...] = (jnp.dot(perm, hi, preferred_element_type=F32)
                   + jnp.dot(perm, lo, preferred_element_type=F32))


def _moe_sort(h2, comb):
    T, D = h2.shape
    nt = T // MOE_TT
    return pl.pallas_call(
        _moe_sort_kernel,
        grid=(nt,),
        in_specs=[pl.BlockSpec((MOE_TT, D), lambda t: (t, 0)),
                  pl.BlockSpec((MOE_TT, LANES), lambda t: (t, 0))],
        out_specs=[pl.BlockSpec((MOE_TTP, D), lambda t: (t, 0)),
                   pl.BlockSpec((MOE_TTP, LANES), lambda t: (t, 0)),
                   pl.BlockSpec((MOE_TT, LANES), lambda t: (t, 0)),
                   pl.BlockSpec((1, 8, LANES), lambda t: (t, 0, 0))],
        out_shape=[jax.ShapeDtypeStruct((nt * MOE_TTP, D), BF16),
                   jax.ShapeDtypeStruct((nt * MOE_TTP, LANES), F32),
                   jax.ShapeDtypeStruct((T, LANES), F32),
                   jax.ShapeDtypeStruct((nt, 8, LANES), F32)],
        scratch_shapes=[pltpu.VMEM((MOE_TT, MOE_TT), BF16)],
        compiler_params=pltpu.CompilerParams(vmem_limit_bytes=VMEM_LIMIT),
        name="moe_sort",
    )(h2, comb)


def _moe_experts_kernel(blk_ref, grp_ref, nvalid_ref, x_ref, c_ref, wg_ref, wu_ref, wd_ref, y_ref, hid_scr):
    s = pl.program_id(0)
    n_slots = pl.num_programs(0) - 1

    @pl.when(s == 0)
    def _():
        hid_scr[...] = jnp.zeros(hid_scr.shape, hid_scr.dtype)

    @pl.when(s > nvalid_ref[0])
    def _():
        y_ref[...] = jnp.zeros(y_ref.shape, y_ref.dtype)

    @pl.when(s <= nvalid_ref[0])
    def _():
        y = None
        for e in range(EXPERTS_PER_GROUP):
            part = jnp.dot(hid_scr[e], wd_ref[e], preferred_element_type=F32)
            y = part if y is None else y + part
        y_ref[...] = y.astype(y_ref.dtype)

        x = x_ref[...]
        comb = c_ref[...]
        lane = lax.broadcasted_iota(jnp.int32, comb.shape, 1)
        e0 = grp_ref[jnp.minimum(s, n_slots - 1)] * EXPERTS_PER_GROUP
        for e in range(EXPERTS_PER_GROUP):
            gate = jnp.dot(x, wg_ref[e], preferred_element_type=F32)
            up = jnp.dot(x, wu_ref[e], preferred_element_type=F32)
            w = jnp.sum(jnp.where(lane == e0 + e, comb, 0.0), axis=1, keepdims=True)
            hid_scr[e] = (gate * jax.nn.sigmoid(gate) * up * w).astype(BF16)


def _moe_experts(blk, grp, nvalid, xs, cs, wg, wu, wd):
    D = xs.shape[1]
    n_slots = blk.shape[0]

    def cur(arr):
        return lambda s, blk, grp, nvalid: (arr(blk, grp)[jnp.minimum(s, n_slots - 1)],)

    def prev(arr):
        return lambda s, blk, grp, nvalid: (arr(blk, grp)[jnp.maximum(s - 1, 0)],)

    blk_of = lambda blk, grp: blk
    grp_of = lambda blk, grp: grp
    rows = lambda f: (lambda *a: f(*a) + (0,))
    wts = lambda f: (lambda *a: f(*a) + (0, 0))
    return pl.pallas_call(
        _moe_experts_kernel,
        grid_spec=pltpu.PrefetchScalarGridSpec(
            num_scalar_prefetch=3,
            grid=(n_slots + 1,),
            in_specs=[pl.BlockSpec((MOE_CH, D), rows(cur(blk_of))),
                      pl.BlockSpec((MOE_CH, LANES), rows(cur(blk_of))),
                      pl.BlockSpec((EXPERTS_PER_GROUP, D, D_EXPERT), wts(cur(grp_of))),
                      pl.BlockSpec((EXPERTS_PER_GROUP, D, D_EXPERT), wts(cur(grp_of))),
                      pl.BlockSpec((EXPERTS_PER_GROUP, D_EXPERT, D), wts(prev(grp_of)))],
            out_specs=pl.BlockSpec((MOE_CH, D), rows(prev(blk_of))),
            scratch_shapes=[pltpu.VMEM((EXPERTS_PER_GROUP, MOE_CH, D_EXPERT), BF16)],
        ),
        out_shape=jax.ShapeDtypeStruct(xs.shape, BF16),
        compiler_params=pltpu.CompilerParams(vmem_limit_bytes=VMEM_LIMIT),
        name="moe_experts",
    )(blk, grp, nvalid, xs, cs, wg, wu, wd)


def _moe_final_kernel(ys_ref, pos_ref, x1_ref, mod_ref, fg_ref, o_ref):
    col = lax.broadcasted_iota(jnp.int32, (MOE_TT, MOE_TTP), 1).astype(F32)
    unperm = jnp.where(col == pltpu.repeat(pos_ref[...], MOE_NBLK, axis=1), 1.0, 0.0).astype(BF16)
    y = jnp.dot(unperm, ys_ref[...], preferred_element_type=F32)
    o_ref[...] = _rms(x1_ref[...] + mod_ref[0, R_G2:R_G2 + 1, :] * y, fg_ref[...])


def _moe_final(ys, pos, x1, mod, fg, n_per_batch):
    T, D = x1.shape
    per_b = n_per_batch // MOE_TT
    return pl.pallas_call(
        _moe_final_kernel,
        grid=(T // MOE_TT,),
        in_specs=[pl.BlockSpec((MOE_TTP, D), lambda t: (t, 0)),
                  pl.BlockSpec((MOE_TT, LANES), lambda t: (t, 0)),
                  pl.BlockSpec((MOE_TT, D), lambda t: (t, 0)),
                  pl.BlockSpec((1, 6, D), lambda t: (t // per_b, 0, 0)),
                  pl.BlockSpec((1, D), lambda t: (0, 0))],
        out_specs=pl.BlockSpec((MOE_TT, D), lambda t: (t, 0)),
        out_shape=jax.ShapeDtypeStruct((T, D), F32),
        compiler_params=pltpu.CompilerParams(vmem_limit_bytes=VMEM_LIMIT),
        name="moe_final",
    )(ys, pos, x1, mod, fg)


def _chunk_schedule(counts):
    nt = counts.shape[0]
    nchunks = jnp.ceil(counts * (1.0 / MOE_CH)).astype(jnp.int32)
    ends = jnp.cumsum(nchunks, axis=1)
    j = jnp.arange(MOE_NBLK, dtype=jnp.int32)
    grp = jnp.sum(j[None, :, None] >= ends[:, None, :], axis=2).astype(jnp.int32).reshape(-1)
    blk = (jnp.arange(nt, dtype=jnp.int32)[:, None] * MOE_NBLK + j[None, :]).reshape(-1)
    order = jnp.argsort(grp, stable=True)
    nvalid = jnp.sum(grp < N_GROUPS).astype(jnp.int32).reshape(1)
    return blk[order], jnp.minimum(grp, N_GROUPS - 1)[order], nvalid


def _rope_tables(n):
    pos = np.arange(n)
    half = MLA_ROPE // 2
    inv_freq = (ROPE_THETA ** (-np.arange(0, half, 2, dtype=np.float32) / half)).astype(np.float32)
    ang = np.concatenate([(pos // GRID_W).astype(np.float32)[:, None] * inv_freq,
                          (pos % GRID_W).astype(np.float32)[:, None] * inv_freq], axis=-1)
    cos, sin = np.cos(ang).astype(np.float32), np.sin(ang).astype(np.float32)
    one = np.ones((n, MLA_NOPE), np.float32)
    zero = np.zeros((n, MLA_NOPE), np.float32)
    pad = np.zeros((n, LANES - MLA_QK), np.float32)
    z16 = np.zeros((n, half), np.float32)
    c = np.concatenate([one, cos, cos, pad], axis=1)
    sa = np.concatenate([zero, -sin, z16, pad], axis=1)
    sb = np.concatenate([zero, z16, sin, pad], axis=1)
    ident = np.concatenate([np.ones((CTX_LEN, MLA_QK), np.float32),
                            np.zeros((CTX_LEN, LANES - MLA_QK), np.float32)], axis=1)
    zc = np.zeros((CTX_LEN, LANES), np.float32)
    return (np.concatenate([ident, c], axis=0), np.concatenate([zc, sa], axis=0),
            np.concatenate([zc, sb], axis=0))


def kernel(x, c, ctx, c_ctx, w_mod, b_mod, norm_attn_g, norm_ffn_g, w_in, q_a_norm_g, kv_a_norm_g, w_uq, w_ukv,
           na_rel_bias, w_out, w_router_group, b_router_group, w_router_expert, b_router_expert, w_gate, w_up,
           w_down, final_norm_g):
    B, N, D = x.shape
    assert w_mod.shape[0] == 1, "single-layer kernel"
    assert (B, N, D) == (2, 8192, D_MODEL) and ctx.shape == (B, CTX_LEN, D)

    cond8 = jnp.concatenate([c, c_ctx[None, :], jnp.zeros((8 - B - 1, D), F32)], axis=0)
    mod = _adaln(cond8, w_mod[0], b_mod[0]).reshape(8, 6, D)

    wi = w_in[0]
    na_w = NA_HEADS * NA_DIM
    k0 = Q_LORA + na_w + KV_LORA
    win = jnp.concatenate([wi[:, :k0], jnp.zeros((D, MLA_NOPE), F32), wi[:, k0:k0 + MLA_ROPE],
                           jnp.zeros((D, LANES - MLA_QK), F32), wi[:, k0 + MLA_ROPE:]], axis=1).astype(BF16)
    assert win.shape[1] == C_END
    wuq = jnp.pad(w_uq[0].reshape(Q_LORA, MLA_HEADS, MLA_QK), ((0, 0), (0, 0), (0, LANES - MLA_QK)))
    wuq = wuq.reshape(Q_LORA, MLA_HEADS * LANES).astype(BF16)
    wkv = w_ukv[0].reshape(KV_LORA, MLA_HEADS, MLA_NOPE + MLA_V)
    wk = jnp.pad(wkv[:, :, :MLA_NOPE], ((0, 0), (0, 0), (0, LANES - MLA_NOPE))).reshape(KV_LORA, MLA_HEADS * LANES)
    wv = jnp.pad(wkv[:, :, MLA_NOPE:], ((0, 0), (0, 0), (0, LANES - MLA_V))).reshape(KV_LORA, MLA_HEADS * LANES)
    wukv = jnp.concatenate([wk, wv], axis=1).astype(BF16)
    rc, rsa, rsb = _rope_tables(N)

    q_m, k_m, v_m, nq, nk, nv = _proj(x, ctx, mod, norm_attn_g, win, q_a_norm_g, kv_a_norm_g, wuq, wukv,
                                      rc, rsa, rsb)
    o_mla = _mla(q_m, k_m, v_m)
    o_na = _natten(nq, nk, nv, _na_bias(na_rel_bias[0]))

    n_r = N_GROUPS + N_EXPERTS
    wr = jnp.pad(jnp.concatenate([w_router_group[0], w_router_expert[0]], axis=1), ((0, 0), (0, LANES - n_r)))
    wr_hi = lax.bitcast_convert_type(lax.bitcast_convert_type(wr, jnp.uint32) & jnp.uint32(0xFFFF0000), F32)
    wr = jnp.concatenate([wr_hi.astype(BF16), (wr - wr_hi).astype(BF16)], axis=1)
    br = jnp.pad(jnp.concatenate([b_router_group[0], b_router_expert[0]]), (0, LANES - n_r)).reshape(1, LANES)
    T = B * N
    x1, h2, comb = _outproj(o_mla.reshape(T, -1), o_na.reshape(T, -1), x.reshape(T, D), mod,
                            w_out[0].astype(BF16), norm_ffn_g, wr, br, N)
    xs, cs, pos, counts = _moe_sort(h2, comb)
    blk, grp, nvalid = _chunk_schedule(counts[:, 0, :N_GROUPS])
    ys = _moe_experts(blk, grp, nvalid, xs, cs, w_gate[0].astype(BF16), w_up[0].astype(BF16),
                      w_down[0].astype(BF16))
    out = _moe_final(ys, pos, x1, mod, final_norm_g.reshape(1, D), N)
    return out.reshape(B, N, D)
```

```python
import functools

import jax
import jax.numpy as jnp
import numpy as np
from jax import lax
from jax.experimental import pallas as pl
from jax.experimental.pallas import tpu as pltpu

D_MODEL = 1024
GRID_W = 64
CTX_LEN = 256
MLA_HEADS = 8
MLA_NOPE = 64
MLA_ROPE = 32
MLA_QK = MLA_NOPE + MLA_ROPE
MLA_V = 64
Q_LORA = 384
KV_LORA = 256
NA_HEADS = 8
NA_DIM = 64
NA_WIN_H = 8
NA_WIN_W = 16
N_GROUPS = 4
EXPERTS_PER_GROUP = 4
N_EXPERTS = N_GROUPS * EXPERTS_PER_GROUP
D_EXPERT = 512
ROPE_THETA = 10000.0
NORM_EPS = 1e-6
MASK_VALUE = -1e30
LOG2E = 1.4426950408889634

LANES = 128
BF16_ROWS = 16
HEAD_PAIRS = 4
TOK_BLOCK = 256
ROWS_PER_BLOCK = TOK_BLOCK // GRID_W
VMEM_LIMIT = 48 * 1024 * 1024

F32 = jnp.float32
BF16 = jnp.bfloat16


def _rms(x, g):
    return x * lax.rsqrt(jnp.mean(x * x, axis=-1, keepdims=True) + NORM_EPS) * g


def _adaln_kernel(c_ref, w_ref, b_ref, o_ref):
    c = c_ref[...]
    s = c * jax.nn.sigmoid(c)
    o_ref[...] = jnp.dot(s, w_ref[...], precision=lax.Precision.HIGHEST,
                         preferred_element_type=F32) + b_ref[...]


def _adaln(cond8, w_mod, b_mod):
    n_out = w_mod.shape[1]
    bn = 1024
    return pl.pallas_call(
        _adaln_kernel,
        grid=(n_out // bn,),
        in_specs=[pl.BlockSpec((8, D_MODEL), lambda n: (0, 0)),
                  pl.BlockSpec((D_MODEL, bn), lambda n: (0, n)),
                  pl.BlockSpec((1, bn), lambda n: (0, n))],
        out_specs=pl.BlockSpec((8, bn), lambda n: (0, n)),
        out_shape=jax.ShapeDtypeStruct((8, n_out), F32),
        name="adaln",
    )(cond8, w_mod, b_mod.reshape(1, n_out))


NA_VARIANTS = ((-1, False), (0, False), (1, False), (0, True), (1, True), (-1, True), (0, None))
V_PREV, V_CUR, V_NEXT, V_CUR_ALL, V_NEXT_ALL, V_PREV_ALL, V_NONE = range(7)
N_DR = 2 * NA_WIN_H - 1
N_DC = 2 * NA_WIN_W - 1


def _na_bias_kernel(rel_ref, o_ref):
    h = pl.program_id(0)
    kc = lax.broadcasted_iota(jnp.int32, (GRID_W, TOK_BLOCK), 0)
    qq = lax.broadcasted_iota(jnp.int32, (GRID_W, TOK_BLOCK), 1)
    qc = qq % GRID_W
    qblk = qq // GRID_W
    col_start = jnp.clip(qc - NA_WIN_W // 2, 0, GRID_W - NA_WIN_W)
    col_in = (kc >= col_start) & (kc < col_start + NA_WIN_W)
    dc = kc - qc + (NA_WIN_W - 1)
    tables = []
    for d in range(N_DR):
        t = jnp.full((GRID_W, TOK_BLOCK), MASK_VALUE, F32)
        base = (h * N_DR + d) * N_DC
        for m in range(N_DC):
            t = jnp.where(dc == m, rel_ref[base + m] * LOG2E, t)
        tables.append(jnp.where(col_in, t, MASK_VALUE))
    for v, (boff, all_valid) in enumerate(NA_VARIANTS):
        for cb in range(ROWS_PER_BLOCK):
            tile = jnp.full((GRID_W, TOK_BLOCK), MASK_VALUE, F32)
            if all_valid is not None:
                for a in range(ROWS_PER_BLOCK):
                    dr = ROWS_PER_BLOCK * boff + cb - a
                    if all_valid or (-(NA_WIN_H // 2) <= dr < NA_WIN_H - NA_WIN_H // 2):
                        tile = jnp.where(qblk == a, tables[dr + NA_WIN_H - 1], tile)
            o_ref[v, 0, cb * GRID_W:(cb + 1) * GRID_W, :] = tile


def _na_bias(rel_bias):
    return pl.pallas_call(
        _na_bias_kernel,
        grid=(NA_HEADS,),
        in_specs=[pl.BlockSpec(memory_space=pltpu.SMEM)],
        out_specs=pl.BlockSpec((len(NA_VARIANTS), 1, TOK_BLOCK, TOK_BLOCK), lambda h: (0, h, 0, 0)),
        out_shape=jax.ShapeDtypeStruct((len(NA_VARIANTS), NA_HEADS, TOK_BLOCK, TOK_BLOCK), F32),
        name="na_bias",
    )(rel_bias.reshape(-1))


C_QLAT = 0
C_NAQ = Q_LORA
C_KVLAT = C_NAQ + NA_HEADS * NA_DIM
C_KROPE = C_KVLAT + KV_LORA
C_NAK = C_KROPE + LANES
C_NAV = C_NAK + NA_HEADS * NA_DIM
C_END = C_NAV + NA_HEADS * NA_DIM
MLA_SCALE = MLA_QK ** -0.5 * LOG2E
NA_SCALE = NA_DIM ** -0.5 * LOG2E
ONES_LANE = MLA_V
V_ROWS = MLA_V + BF16_ROWS


def _rope(x, c, sa, sb):
    return x * c + pltpu.roll(x, LANES - MLA_ROPE // 2, 1) * sa + pltpu.roll(x, MLA_ROPE // 2, 1) * sb


def _proj_kernel(x_ref, ctx_ref, mod_ref, g_ref, win_ref, qg_ref, kvg_ref, wuq_ref, wukv_ref,
                 c_ref, sa_ref, sb_ref,
                 q_out, k_out, v_out, nq_out, nk_out, nv_out, h_scr, proj_scr):
    j = pl.program_id(1)

    @pl.when(j == 0)
    def _():
        h_scr[...] = jnp.zeros(h_scr.shape, h_scr.dtype)
        proj_scr[...] = jnp.zeros(proj_scr.shape, proj_scr.dtype)

    proj = proj_scr
    c, sa, sb = c_ref[...], sa_ref[...], sb_ref[...]

    qn = _rms(proj[:, C_QLAT:C_NAQ], qg_ref[...]).astype(BF16)
    q = jnp.dot(qn, wuq_ref[...], preferred_element_type=F32) * MLA_SCALE
    for h in range(MLA_HEADS):
        q_out[0, h] = _rope(q[:, h * LANES:(h + 1) * LANES], c, sa, sb).astype(BF16)

    kvn = _rms(proj[:, C_KVLAT:C_KROPE], kvg_ref[...]).astype(BF16)
    kv = jnp.dot(kvn, wukv_ref[...], preferred_element_type=F32)
    kr = _rope(proj[:, C_KROPE:C_NAK], c, sa, sb)
    v0 = MLA_HEADS * LANES
    lane = lax.broadcasted_iota(jnp.int32, (TOK_BLOCK, LANES), 1)
    for h in range(MLA_HEADS):
        k_out[0, h] = (kv[:, h * LANES:(h + 1) * LANES] + kr).astype(BF16)
        v = jnp.where(lane == ONES_LANE, 1.0, kv[:, v0 + h * LANES:v0 + (h + 1) * LANES])
        v_out[0, h] = jnp.transpose(v).astype(BF16)
    for p in range(HEAD_PAIRS):
        nq_out[0, p] = (proj[:, C_NAQ + p * LANES:C_NAQ + (p + 1) * LANES] * NA_SCALE).astype(BF16)
        nk_out[0, p] = proj[:, C_NAK + p * LANES:C_NAK + (p + 1) * LANES].astype(BF16)
        nv_pair = proj[:, C_NAV + p * LANES:C_NAV + (p + 1) * LANES]
        for half in range(2):
            nv = pltpu.roll(nv_pair, NA_DIM, 1) if half else nv_pair
            nv = jnp.where(lane < NA_DIM, nv, jnp.where(lane == ONES_LANE, 1.0, 0.0))
            nv_out[0, 2 * p + half] = jnp.transpose(nv).astype(BF16)

    proj_scr[...] = jnp.dot(h_scr[...], win_ref[...], preferred_element_type=F32)
    xt = jnp.where(j == 0, ctx_ref[0], x_ref[0])
    h_scr[...] = (_rms(xt, g_ref[...]) * (1.0 + mod_ref[0, 1:2, :]) + mod_ref[0, 0:1, :]).astype(BF16)


def _proj(x, ctx, mod, g, win, qg, kvg, wuq, wukv, rc, rsa, rsb):
    B, N, D = x.shape
    nb = N // TOK_BLOCK
    ntot = N + CTX_LEN
    full = lambda shape: pl.BlockSpec(shape, lambda b, j: (0,) * len(shape))
    depth = 2
    tile_in = lambda j: jnp.minimum(j, nb)
    tile_out = lambda j: jnp.clip(j - depth, 0, nb)
    lat = lambda b, j: (b, 0, jnp.maximum(tile_out(j) - 1, 0), 0)
    alltok = lambda b, j: (b, 0, tile_out(j), 0)
    feat = lambda b, j: (b, 0, 0, tile_out(j))
    rope = lambda b, j: (tile_out(j), 0)
    return pl.pallas_call(
        _proj_kernel,
        grid=(B, nb + 1 + depth),
        in_specs=[pl.BlockSpec((1, TOK_BLOCK, D), lambda b, j: (b, jnp.maximum(tile_in(j) - 1, 0), 0)),
                  pl.BlockSpec((1, CTX_LEN, D), lambda b, j: (b, 0, 0)),
                  pl.BlockSpec((1, 6, D), lambda b, j: (jnp.where(j == 0, B, b), 0, 0)),
                  full((1, D)), full(win.shape), full((1, Q_LORA)), full((1, KV_LORA)),
                  full(wuq.shape), full(wukv.shape),
                  pl.BlockSpec((TOK_BLOCK, LANES), rope),
                  pl.BlockSpec((TOK_BLOCK, LANES), rope),
                  pl.BlockSpec((TOK_BLOCK, LANES), rope)],
        out_specs=[pl.BlockSpec((1, MLA_HEADS, TOK_BLOCK, LANES), lat),
                   pl.BlockSpec((1, MLA_HEADS, TOK_BLOCK, LANES), alltok),
                   pl.BlockSpec((1, MLA_HEADS, LANES, TOK_BLOCK), feat),
                   pl.BlockSpec((1, HEAD_PAIRS, TOK_BLOCK, LANES), lat),
                   pl.BlockSpec((1, HEAD_PAIRS, TOK_BLOCK, LANES), alltok),
                   pl.BlockSpec((1, NA_HEADS, LANES, TOK_BLOCK), feat)],
        out_shape=[jax.ShapeDtypeStruct((B, MLA_HEADS, N, LANES), BF16),
                   jax.ShapeDtypeStruct((B, MLA_HEADS, ntot, LANES), BF16),
                   jax.ShapeDtypeStruct((B, MLA_HEADS, LANES, ntot), BF16),
                   jax.ShapeDtypeStruct((B, HEAD_PAIRS, N, LANES), BF16),
                   jax.ShapeDtypeStruct((B, HEAD_PAIRS, ntot, LANES), BF16),
                   jax.ShapeDtypeStruct((B, NA_HEADS, LANES, ntot), BF16)],
        scratch_shapes=[pltpu.VMEM((TOK_BLOCK, D), BF16), pltpu.VMEM((TOK_BLOCK, C_END), F32)],
        compiler_params=pltpu.CompilerParams(vmem_limit_bytes=VMEM_LIMIT),
        name="proj",
    )(x, ctx, mod, g, win, qg, kvg, wuq, wukv, rc, rsa, rsb)


MLA_TQ = 1024
MLA_TK = 768


def _mla_kernel(q_ref, k_ref, v_ref, o_ref, m_scr, alpha_scr, acc_scr, p_scr, s_scr, smax_scr):
    n_chunks = k_ref.shape[2] // MLA_TK
    m_scr[...] = jnp.full(m_scr.shape, -jnp.inf, F32)
    acc_scr[...] = jnp.zeros(acc_scr.shape, F32)

    def scores(ci, hh):
        off = pl.multiple_of(ci * MLA_TK, MLA_TK)
        k = k_ref[0, hh, pl.ds(off, MLA_TK), :]
        s = lax.dot_general(k, q_ref[0, hh], (((1,), (1,)), ((), ())), preferred_element_type=F32)
        s_scr[hh] = s
        smax_scr[hh] = jnp.max(s, axis=0, keepdims=True)

    def softmax(hh):
        m_prev = m_scr[hh]
        m_next = jnp.maximum(m_prev, smax_scr[hh])
        p_scr[hh] = jnp.exp2((s_scr[hh] - m_next).astype(BF16))
        alpha_scr[hh] = jnp.exp2(m_prev - m_next)
        m_scr[hh] = m_next

    def values(ci, hh):
        off = pl.multiple_of(ci * MLA_TK, MLA_TK)
        vt = v_ref[0, hh, :V_ROWS, pl.ds(off, MLA_TK)]
        acc_scr[hh] = alpha_scr[hh] * acc_scr[hh] + jnp.dot(vt, p_scr[hh], preferred_element_type=F32)

    for hh in range(2):
        scores(0, hh)
    for hh in range(2):
        softmax(hh)
        scores(1, hh)

    def body(ci, carry):
        for hh in range(2):
            values(ci - 1, hh)
            softmax(hh)
            scores(ci + 1, hh)
        return carry

    lax.fori_loop(1, n_chunks - 1, body, 0)
    for hh in range(2):
        values(n_chunks - 2, hh)
        softmax(hh)
    for hh in range(2):
        values(n_chunks - 1, hh)
    outs = []
    for hh in range(2):
        acc = acc_scr[hh]
        o = acc / acc[ONES_LANE:ONES_LANE + 1, :]
        outs.append(jnp.transpose(jnp.concatenate([o, jnp.zeros((LANES - V_ROWS, MLA_TQ), F32)], axis=0)))
    lane = lax.broadcasted_iota(jnp.int32, (MLA_TQ, LANES), 1)
    o_ref[0] = jnp.where(lane < MLA_V, outs[0], pltpu.roll(outs[1], MLA_V, 1)).astype(o_ref.dtype)


def _mla(q_m, k_m, v_m):
    B, _, N, _ = q_m.shape
    ntot = k_m.shape[2]
    return pl.pallas_call(
        _mla_kernel,
        grid=(B, HEAD_PAIRS, N // MLA_TQ),
        in_specs=[pl.BlockSpec((1, 2, MLA_TQ, LANES), lambda b, p, i: (b, p, i, 0)),
                  pl.BlockSpec((1, 2, ntot, LANES), lambda b, p, i: (b, p, 0, 0)),
                  pl.BlockSpec((1, 2, LANES, ntot), lambda b, p, i: (b, p, 0, 0))],
        out_specs=pl.BlockSpec((1, MLA_TQ, LANES), lambda b, p, i: (b, i, p)),
        out_shape=jax.ShapeDtypeStruct((B, N, MLA_HEADS * MLA_V), BF16),
        scratch_shapes=[pltpu.VMEM((2, 1, MLA_TQ), F32), pltpu.VMEM((2, 1, MLA_TQ), F32),
                        pltpu.VMEM((2, V_ROWS, MLA_TQ), F32), pltpu.VMEM((2, MLA_TK, MLA_TQ), BF16),
                        pltpu.VMEM((2, MLA_TK, MLA_TQ), F32), pltpu.VMEM((2, 1, MLA_TQ), F32)],
        compiler_params=pltpu.CompilerParams(vmem_limit_bytes=VMEM_LIMIT),
        name="mla",
    )(q_m, k_m, v_m)


NA_UNIT_HEADS = 8
NA_UNITS_PER_BLOCK = NA_HEADS // NA_UNIT_HEADS
NA_KEYS = 4 * TOK_BLOCK


def _na_kernel(q_ref, kc_ref, kp_ref, k0_ref, kn_ref, bp_ref, b0_ref, bn_ref,
               vc_ref, vp_ref, v0_ref, vn_ref, o_ref, s_scr, smax_scr, p_scr):
    @pl.when(pl.program_id(1) == 0)
    def _():
        s_scr[...] = jnp.zeros(s_scr.shape, F32)
        smax_scr[...] = jnp.zeros(smax_scr.shape, F32)
        p_scr[...] = jnp.zeros(p_scr.shape, BF16)

    lane = lax.broadcasted_iota(jnp.int32, (TOK_BLOCK, LANES), 1)

    for pp in range(NA_UNIT_HEADS // 2):
        outs = []
        for half in range(2):
            hq = 2 * pp + half
            vt = jnp.concatenate([vc_ref[0, hq, :V_ROWS], vp_ref[0, hq, :V_ROWS], v0_ref[0, hq, :V_ROWS],
                                  vn_ref[0, hq, :V_ROWS]], axis=1)
            o = jnp.dot(vt, p_scr[hq], preferred_element_type=F32)
            o = o / o[ONES_LANE:ONES_LANE + 1, :]
            outs.append(jnp.transpose(jnp.concatenate([o, jnp.zeros((LANES - V_ROWS, TOK_BLOCK), F32)], axis=0)))
        o_ref[0, :, pp * LANES:(pp + 1) * LANES] = jnp.where(
            lane < NA_DIM, outs[0], pltpu.roll(outs[1], NA_DIM, 1)).astype(o_ref.dtype)

    for hq in range(NA_UNIT_HEADS):
        p_scr[hq] = jnp.exp2((s_scr[hq] - smax_scr[hq]).astype(BF16))

    dn = (((1,), (1,)), ((), ()))
    for pp in range(NA_UNIT_HEADS // 2):
        keys = jnp.concatenate([kc_ref[0, pp], kp_ref[0, pp], k0_ref[0, pp], kn_ref[0, pp]], axis=0)
        for half in range(2):
            hq = 2 * pp + half
            in_half = (lane >= NA_DIM) if half else (lane < NA_DIM)
            q = jnp.where(in_half, q_ref[0, pp], jnp.zeros((), BF16))
            s = lax.dot_general(keys, q, dn, preferred_element_type=F32)
            parts = [s[:TOK_BLOCK],
                     s[TOK_BLOCK:2 * TOK_BLOCK] + bp_ref[0, hq],
                     s[2 * TOK_BLOCK:3 * TOK_BLOCK] + b0_ref[0, hq],
                     s[3 * TOK_BLOCK:] + bn_ref[0, hq]]
            for j, part in enumerate(parts):
                s_scr[hq, j * TOK_BLOCK:(j + 1) * TOK_BLOCK, :] = part
            smax_scr[hq] = jnp.max(jnp.maximum(jnp.maximum(parts[0], parts[1]),
                                               jnp.maximum(parts[2], parts[3])), axis=0, keepdims=True)


def _natten(nq, nk, nv, bias):
    B, _, N, _ = nq.shape
    nb = N // TOK_BLOCK
    last = nb - 1
    n_units = nb * NA_UNITS_PER_BLOCK
    pipeline_depth = 2

    def unit(t, lag):
        u = jnp.clip(t - lag, 0, n_units - 1)
        return u // NA_UNITS_PER_BLOCK, u % NA_UNITS_PER_BLOCK

    tok_blocks = (lambda i: 0, lambda i: jnp.maximum(i - 1, 0) + 1, lambda i: i + 1,
                  lambda i: jnp.minimum(i + 1, last) + 1)
    variants = (lambda i: jnp.where(i == 0, V_NONE, jnp.where(i == last, V_PREV_ALL, V_PREV)),
                lambda i: jnp.where((i == 0) | (i == last), V_CUR_ALL, V_CUR),
                lambda i: jnp.where(i == 0, V_NEXT_ALL, jnp.where(i == last, V_NONE, V_NEXT)))

    def q_map(b, t):
        i, g = unit(t, 0)
        return b, g, i, 0

    def key_map(tok):
        def f(b, t):
            i, g = unit(t, 0)
            return b, g, tok(i), 0
        return f

    def bias_map(var):
        def f(b, t):
            i, g = unit(t, 0)
            return var(i), g, 0, 0
        return f

    def value_map(tok):
        def f(b, t):
            i, g = unit(t, pipeline_depth)
            return b, g, 0, tok(i)
        return f

    def out_map(b, t):
        i, g = unit(t, pipeline_depth)
        return b, i, g

    pair_blk = (1, NA_UNIT_HEADS // 2, TOK_BLOCK, LANES)
    return pl.pallas_call(
        _na_kernel,
        grid=(B, n_units + pipeline_depth),
        in_specs=([pl.BlockSpec(pair_blk, q_map)]
                  + [pl.BlockSpec(pair_blk, key_map(tok)) for tok in tok_blocks]
                  + [pl.BlockSpec((1, NA_UNIT_HEADS, TOK_BLOCK, TOK_BLOCK), bias_map(var)) for var in variants]
                  + [pl.BlockSpec((1, NA_UNIT_HEADS, LANES, TOK_BLOCK), value_map(tok)) for tok in tok_blocks]),
        out_specs=pl.BlockSpec((1, TOK_BLOCK, NA_UNIT_HEADS * NA_DIM), out_map),
        out_shape=jax.ShapeDtypeStruct((B, N, NA_HEADS * NA_DIM), BF16),
        scratch_shapes=[pltpu.VMEM((NA_UNIT_HEADS, NA_KEYS, TOK_BLOCK), F32),
                        pltpu.VMEM((NA_UNIT_HEADS, 1, TOK_BLOCK), F32),
                        pltpu.VMEM((NA_UNIT_HEADS, NA_KEYS, TOK_BLOCK), BF16)],
        compiler_params=pltpu.CompilerParams(vmem_limit_bytes=VMEM_LIMIT),
        name="natten",
    )(nq, nk, nk, nk, nk, bias, bias, bias, nv, nv, nv, nv)


OUT_TM = 512
R_G1, R_SH2, R_SC2, R_G2 = 2, 3, 4, 5
GROUP_LANE = N_EXPERTS


def _first_max_lane(x, valid, lane):
    xm = jnp.where(valid, x, -jnp.inf)
    mx = jnp.max(xm, axis=1, keepdims=True)
    idx = jnp.min(jnp.where(valid & (xm == mx), lane, LANES), axis=1, keepdims=True)
    return mx, idx


def _outproj_kernel(om_ref, on_ref, x_ref, mod_ref, wo_ref, g_ref, wr_ref, br_ref,
                    x1_out, h2_out, comb_out, h2_scr):
    @pl.when(pl.program_id(0) == 0)
    def _():
        h2_scr[...] = jnp.zeros(h2_scr.shape, h2_scr.dtype)

    h2_prev = h2_scr[...]
    h2_hi = h2_prev.astype(BF16)
    h2_lo = (h2_prev - h2_hi.astype(F32)).astype(BF16)
    both = jnp.dot(h2_hi, wr_ref[...], preferred_element_type=F32)
    logits = (both[:, :LANES] + both[:, LANES:]
              + jnp.dot(h2_lo, wr_ref[:, :LANES], preferred_element_type=F32) + br_ref[...])
    lane = lax.broadcasted_iota(jnp.int32, logits.shape, 1)
    is_g = lane < N_GROUPS
    g_max, g_sel = _first_max_lane(logits, is_g, lane)
    g_w = 1.0 / jnp.sum(jnp.where(is_g, jnp.exp(logits - g_max), 0.0), axis=1, keepdims=True)
    e_lo = N_GROUPS + g_sel * EXPERTS_PER_GROUP
    in_grp = (lane >= e_lo) & (lane < e_lo + EXPERTS_PER_GROUP)
    m1, i1 = _first_max_lane(logits, in_grp, lane)
    m2, i2 = _first_max_lane(logits, in_grp & (lane != i1), lane)
    e2 = jnp.exp(m2 - m1)
    w1 = 1.0 / (1.0 + e2)
    w2 = e2 / (1.0 + e2)
    comb = jnp.where(lane == i1, g_w * w1, jnp.where(lane == i2, g_w * w2, 0.0))
    comb_out[...] = jnp.where(lane == GROUP_LANE, g_sel.astype(F32), pltpu.roll(comb, LANES - N_GROUPS, 1))

    half = wo_ref.shape[0] // 2
    mixed = (jnp.dot(om_ref[...], wo_ref[:half, :], preferred_element_type=F32)
             + jnp.dot(on_ref[...], wo_ref[half:, :], preferred_element_type=F32))
    x1 = x_ref[...] + mod_ref[0, R_G1:R_G1 + 1, :] * mixed
    x1_out[...] = x1
    h2 = _rms(x1, g_ref[...]) * (1.0 + mod_ref[0, R_SC2:R_SC2 + 1, :]) + mod_ref[0, R_SH2:R_SH2 + 1, :]
    h2_out[...] = h2.astype(BF16)
    h2_scr[...] = h2


def _outproj(o_mla, o_na, x2d, mod, wo, g, wr, br, n_per_batch):
    T, D = x2d.shape
    per_b = n_per_batch // OUT_TM
    nt = T // OUT_TM
    cur = lambda t: jnp.minimum(t, nt - 1)
    tok = lambda w: pl.BlockSpec((OUT_TM, w), lambda t: (cur(t), 0))
    full = lambda shape: pl.BlockSpec(shape, lambda t: (0,) * len(shape))
    return pl.pallas_call(
        _outproj_kernel,
        grid=(nt + 1,),
        in_specs=[tok(o_mla.shape[1]), tok(o_na.shape[1]), tok(D),
                  pl.BlockSpec((1, 6, D), lambda t: (cur(t) // per_b, 0, 0)),
                  full(wo.shape), full((1, D)), full(wr.shape), full(br.shape)],
        out_specs=[tok(D), tok(D), pl.BlockSpec((OUT_TM, LANES), lambda t: (jnp.maximum(t - 1, 0), 0))],
        out_shape=[jax.ShapeDtypeStruct((T, D), F32), jax.ShapeDtypeStruct((T, D), BF16),
                   jax.ShapeDtypeStruct((T, LANES), F32)],
        scratch_shapes=[pltpu.VMEM((OUT_TM, D), F32)],
        compiler_params=pltpu.CompilerParams(vmem_limit_bytes=VMEM_LIMIT),
        name="outproj",
    )(o_mla, o_na, x2d, mod, wo, g, wr, br)


MOE_TT = 1024
MOE_CH = 128
MOE_NBLK = MOE_TT // MOE_CH + (N_GROUPS - 1)
MOE_TTP = MOE_NBLK * MOE_CH


def _moe_sort_kernel(h_ref, comb_ref, xs_out, cs_out, pos_out, cnt_out, tri_scr):
    @pl.when(pl.program_id(0) == 0)
    def _():
        r = lax.broadcasted_iota(jnp.int32, tri_scr.shape, 0)
        c = lax.broadcasted_iota(jnp.int32, tri_scr.shape, 1)
        tri_scr[...] = jnp.where(c < r, 1.0, 0.0).astype(BF16)

    comb = comb_ref[...]
    lane = lax.broadcasted_iota(jnp.int32, comb.shape, 1)
    lane_f = lane.astype(F32)
    gid = jnp.sum(jnp.where(lane == GROUP_LANE, comb, 0.0), axis=1, keepdims=True)
    onehot = jnp.where((lane_f == gid) & (lane < N_GROUPS), 1.0, 0.0)
    ahead = jnp.dot(tri_scr[...], onehot.astype(BF16), preferred_element_type=F32)
    n = jnp.sum(onehot, axis=0, keepdims=True)
    padded = jnp.ceil(n * (1.0 / MOE_CH)) * MOE_CH
    start = jnp.zeros_like(padded)
    for k in range(1, N_GROUPS):
        start = start + jnp.where(lane[:1] >= k, pltpu.roll(padded, k, 1), 0.0)
    pos = jnp.sum(onehot * (ahead + start), axis=1, keepdims=True)
    pos_rep = jnp.broadcast_to(pos, comb.shape)
    pos_out[...] = pos_rep
    cnt_out[0] = jnp.broadcast_to(n, (8, LANES))

    pos_row = jnp.transpose(pos_rep)[0:1, :]
    row = lax.broadcasted_iota(jnp.int32, (MOE_TTP, MOE_TT), 0).astype(F32)
    perm = jnp.where(row == pos_row, 1.0, 0.0).astype(BF16)
    xs_out[...] = jnp.dot(perm, h_ref[...], preferred_element_type=F32).astype(BF16)
    hi = comb.astype(BF16)
    lo = (comb - hi.astype(F32)).astype(BF16)
    cs_out[...] = (jnp.dot(perm, hi, preferred_element_type=F32)
                   + jnp.dot(perm, lo, preferred_element_type=F32))


def _moe_sort(h2, comb):
    T, D = h2.shape
    nt = T // MOE_TT
    return pl.pallas_call(
        _moe_sort_kernel,
        grid=(nt,),
        in_specs=[pl.BlockSpec((MOE_TT, D), lambda t: (t, 0)),
                  pl.BlockSpec((MOE_TT, LANES), lambda t: (t, 0))],
        out_specs=[pl.BlockSpec((MOE_TTP, D), lambda t: (t, 0)),
                   pl.BlockSpec((MOE_TTP, LANES), lambda t: (t, 0)),
                   pl.BlockSpec((MOE_TT, LANES), lambda t: (t, 0)),
                   pl.BlockSpec((1, 8, LANES), lambda t: (t, 0, 0))],
        out_shape=[jax.ShapeDtypeStruct((nt * MOE_TTP, D), BF16),
                   jax.ShapeDtypeStruct((nt * MOE_TTP, LANES), F32),
                   jax.ShapeDtypeStruct((T, LANES), F32),
                   jax.ShapeDtypeStruct((nt, 8, LANES), F32)],
        scratch_shapes=[pltpu.VMEM((MOE_TT, MOE_TT), BF16)],
        compiler_params=pltpu.CompilerParams(vmem_limit_bytes=VMEM_LIMIT),
        name="moe_sort",
    )(h2, comb)


def _moe_experts_kernel(blk_ref, grp_ref, nvalid_ref, x_ref, c_ref, wg_ref, wu_ref, wd_ref, y_ref, hid_scr):
    s = pl.program_id(0)
    n_slots = pl.num_programs(0) - 1

    @pl.when(s == 0)
    def _():
        hid_scr[...] = jnp.zeros(hid_scr.shape, hid_scr.dtype)

    @pl.when(s > nvalid_ref[0])
    def _():
        y_ref[...] = jnp.zeros(y_ref.shape, y_ref.dtype)

    @pl.when(s <= nvalid_ref[0])
    def _():
        y = None
        for e in range(EXPERTS_PER_GROUP):
            part = jnp.dot(hid_scr[e], wd_ref[e], preferred_element_type=F32)
            y = part if y is None else y + part
        y_ref[...] = y.astype(y_ref.dtype)

        x = x_ref[...]
        comb = c_ref[...]
        lane = lax.broadcasted_iota(jnp.int32, comb.shape, 1)
        e0 = grp_ref[jnp.minimum(s, n_slots - 1)] * EXPERTS_PER_GROUP
        for e in range(EXPERTS_PER_GROUP):
            gate = jnp.dot(x, wg_ref[e], preferred_element_type=F32)
            up = jnp.dot(x, wu_ref[e], preferred_element_type=F32)
            w = jnp.sum(jnp.where(lane == e0 + e, comb, 0.0), axis=1, keepdims=True)
            hid_scr[e] = (gate * jax.nn.sigmoid(gate) * up * w).astype(BF16)


def _moe_experts(blk, grp, nvalid, xs, cs, wg, wu, wd):
    D = xs.shape[1]
    n_slots = blk.shape[0]

    def cur(arr):
        return lambda s, blk, grp, nvalid: (arr(blk, grp)[jnp.minimum(s, n_slots - 1)],)

    def prev(arr):
        return lambda s, blk, grp, nvalid: (arr(blk, grp)[jnp.maximum(s - 1, 0)],)

    blk_of = lambda blk, grp: blk
    grp_of = lambda blk, grp: grp
    rows = lambda f: (lambda *a: f(*a) + (0,))
    wts = lambda f: (lambda *a: f(*a) + (0, 0))
    return pl.pallas_call(
        _moe_experts_kernel,
        grid_spec=pltpu.PrefetchScalarGridSpec(
            num_scalar_prefetch=3,
            grid=(n_slots + 1,),
            in_specs=[pl.BlockSpec((MOE_CH, D), rows(cur(blk_of))),
                      pl.BlockSpec((MOE_CH, LANES), rows(cur(blk_of))),
                      pl.BlockSpec((EXPERTS_PER_GROUP, D, D_EXPERT), wts(cur(grp_of))),
                      pl.BlockSpec((EXPERTS_PER_GROUP, D, D_EXPERT), wts(cur(grp_of))),
                      pl.BlockSpec((EXPERTS_PER_GROUP, D_EXPERT, D), wts(prev(grp_of)))],
            out_specs=pl.BlockSpec((MOE_CH, D), rows(prev(blk_of))),
            scratch_shapes=[pltpu.VMEM((EXPERTS_PER_GROUP, MOE_CH, D_EXPERT), BF16)],
        ),
        out_shape=jax.ShapeDtypeStruct(xs.shape, BF16),
        compiler_params=pltpu.CompilerParams(vmem_limit_bytes=VMEM_LIMIT),
        name="moe_experts",
    )(blk, grp, nvalid, xs, cs, wg, wu, wd)


def _moe_final_kernel(ys_ref, pos_ref, x1_ref, mod_ref, fg_ref, o_ref):
    col = lax.broadcasted_iota(jnp.int32, (MOE_TT, MOE_TTP), 1).astype(F32)
    unperm = jnp.where(col == pltpu.repeat(pos_ref[...], MOE_NBLK, axis=1), 1.0, 0.0).astype(BF16)
    y = jnp.dot(unperm, ys_ref[...], preferred_element_type=F32)
    o_ref[...] = _rms(x1_ref[...] + mod_ref[0, R_G2:R_G2 + 1, :] * y, fg_ref[...])


def _moe_final(ys, pos, x1, mod, fg, n_per_batch):
    T, D = x1.shape
    per_b = n_per_batch // MOE_TT
    return pl.pallas_call(
        _moe_final_kernel,
        grid=(T // MOE_TT,),
        in_specs=[pl.BlockSpec((MOE_TTP, D), lambda t: (t, 0)),
                  pl.BlockSpec((MOE_TT, LANES), lambda t: (t, 0)),
                  pl.BlockSpec((MOE_TT, D), lambda t: (t, 0)),
                  pl.BlockSpec((1, 6, D), lambda t: (t // per_b, 0, 0)),
                  pl.BlockSpec((1, D), lambda t: (0, 0))],
        out_specs=pl.BlockSpec((MOE_TT, D), lambda t: (t, 0)),
        out_shape=jax.ShapeDtypeStruct((T, D), F32),
        compiler_params=pltpu.CompilerParams(vmem_limit_bytes=VMEM_LIMIT),
        name="moe_final",
    )(ys, pos, x1, mod, fg)


def _chunk_schedule(counts):
    nt = counts.shape[0]
    nchunks = jnp.ceil(counts * (1.0 / MOE_CH)).astype(jnp.int32)
    ends = jnp.cumsum(nchunks, axis=1)
    j = jnp.arange(MOE_NBLK, dtype=jnp.int32)
    grp = jnp.sum(j[None, :, None] >= ends[:, None, :], axis=2).astype(jnp.int32).reshape(-1)
    blk = (jnp.arange(nt, dtype=jnp.int32)[:, None] * MOE_NBLK + j[None, :]).reshape(-1)
    order = jnp.argsort(grp, stable=True)
    nvalid = jnp.sum(grp < N_GROUPS).astype(jnp.int32).reshape(1)
    return blk[order], jnp.minimum(grp, N_GROUPS - 1)[order], nvalid


def _rope_tables(n):
    pos = np.arange(n)
    half = MLA_ROPE // 2
    inv_freq = (ROPE_THETA ** (-np.arange(0, half, 2, dtype=np.float32) / half)).astype(np.float32)
    ang = np.concatenate([(pos // GRID_W).astype(np.float32)[:, None] * inv_freq,
                          (pos % GRID_W).astype(np.float32)[:, None] * inv_freq], axis=-1)
    cos, sin = np.cos(ang).astype(np.float32), np.sin(ang).astype(np.float32)
    one = np.ones((n, MLA_NOPE), np.float32)
    zero = np.zeros((n, MLA_NOPE), np.float32)
    pad = np.zeros((n, LANES - MLA_QK), np.float32)
    z16 = np.zeros((n, half), np.float32)
    c = np.concatenate([one, cos, cos, pad], axis=1)
    sa = np.concatenate([zero, -sin, z16, pad], axis=1)
    sb = np.concatenate([zero, z16, sin, pad], axis=1)
    ident = np.concatenate([np.ones((CTX_LEN, MLA_QK), np.float32),
                            np.zeros((CTX_LEN, LANES - MLA_QK), np.float32)], axis=1)
    zc = np.zeros((CTX_LEN, LANES), np.float32)
    return (np.concatenate([ident, c], axis=0), np.concatenate([zc, sa], axis=0),
            np.concatenate([zc, sb], axis=0))


def kernel(x, c, ctx, c_ctx, w_mod, b_mod, norm_attn_g, norm_ffn_g, w_in, q_a_norm_g, kv_a_norm_g, w_uq, w_ukv,
           na_rel_bias, w_out, w_router_group, b_router_group, w_router_expert, b_router_expert, w_gate, w_up,
           w_down, final_norm_g):
    B, N, D = x.shape
    assert w_mod.shape[0] == 1, "single-layer kernel"
    assert (B, N, D) == (2, 8192, D_MODEL) and ctx.shape == (B, CTX_LEN, D)

    cond8 = jnp.concatenate([c, c_ctx[None, :], jnp.zeros((8 - B - 1, D), F32)], axis=0)
    mod = _adaln(cond8, w_mod[0], b_mod[0]).reshape(8, 6, D)

    wi = w_in[0]
    na_w = NA_HEADS * NA_DIM
    k0 = Q_LORA + na_w + KV_LORA
    win = jnp.concatenate([wi[:, :k0], jnp.zeros((D, MLA_NOPE), F32), wi[:, k0:k0 + MLA_ROPE],
                           jnp.zeros((D, LANES - MLA_QK), F32), wi[:, k0 + MLA_ROPE:]], axis=1).astype(BF16)
    assert win.shape[1] == C_END
    wuq = jnp.pad(w_uq[0].reshape(Q_LORA, MLA_HEADS, MLA_QK), ((0, 0), (0, 0), (0, LANES - MLA_QK)))
    wuq = wuq.reshape(Q_LORA, MLA_HEADS * LANES).astype(BF16)
    wkv = w_ukv[0].reshape(KV_LORA, MLA_HEADS, MLA_NOPE + MLA_V)
    wk = jnp.pad(wkv[:, :, :MLA_NOPE], ((0, 0), (0, 0), (0, LANES - MLA_NOPE))).reshape(KV_LORA, MLA_HEADS * LANES)
    wv = jnp.pad(wkv[:, :, MLA_NOPE:], ((0, 0), (0, 0), (0, LANES - MLA_V))).reshape(KV_LORA, MLA_HEADS * LANES)
    wukv = jnp.concatenate([wk, wv], axis=1).astype(BF16)
    rc, rsa, rsb = _rope_tables(N)

    q_m, k_m, v_m, nq, nk, nv = _proj(x, ctx, mod, norm_attn_g, win, q_a_norm_g, kv_a_norm_g, wuq, wukv,
                                      rc, rsa, rsb)
    o_mla = _mla(q_m, k_m, v_m)
    o_na = _natten(nq, nk, nv, _na_bias(na_rel_bias[0]))

    n_r = N_GROUPS + N_EXPERTS
    wr = jnp.pad(jnp.concatenate([w_router_group[0], w_router_expert[0]], axis=1), ((0, 0), (0, LANES - n_r)))
    wr_hi = lax.bitcast_convert_type(lax.bitcast_convert_type(wr, jnp.uint32) & jnp.uint32(0xFFFF0000), F32)
    wr = jnp.concatenate([wr_hi.astype(BF16), (wr - wr_hi).astype(BF16)], axis=1)
    br = jnp.pad(jnp.concatenate([b_router_group[0], b_router_expert[0]]), (0, LANES - n_r)).reshape(1, LANES)
    T = B * N
    x1, h2, comb = _outproj(o_mla.reshape(T, -1), o_na.reshape(T, -1), x.reshape(T, D), mod,
                            w_out[0].astype(BF16), norm_ffn_g, wr, br, N)
    xs, cs, pos, counts = _moe_sort(h2, comb)
    blk, grp, nvalid = _chunk_schedule(counts[:, 0, :N_GROUPS])
    ys = _moe_experts(blk, grp, nvalid, xs, cs, w_gate[0].astype(BF16), w_up[0].astype(BF16),
                      w_down[0].astype(BF16))
    out = _moe_final(ys, pos, x1, mod, final_norm_g.reshape(1, D), N)
    return out.reshape(B, N, D)
```

```python
import functools

import jax
import jax.numpy as jnp
import numpy as np
from jax import lax
from jax.experimental import pallas as pl
from jax.experimental.pallas import tpu as pltpu

D_MODEL = 1024
GRID_W = 64
CTX_LEN = 256
MLA_HEADS = 8
MLA_NOPE = 64
MLA_ROPE = 32
MLA_QK = MLA_NOPE + MLA_ROPE
MLA_V = 64
Q_LORA = 384
KV_LORA = 256
NA_HEADS = 8
NA_DIM = 64
NA_WIN_H = 8
NA_WIN_W = 16
N_GROUPS = 4
EXPERTS_PER_GROUP = 4
N_EXPERTS = N_GROUPS * EXPERTS_PER_GROUP
D_EXPERT = 512
ROPE_THETA = 10000.0
NORM_EPS = 1e-6
MASK_VALUE = -1e30
LOG2E = 1.4426950408889634

LANES = 128
HEAD_PAIRS = 4
TOK_BLOCK = 256
ROWS_PER_BLOCK = TOK_BLOCK // GRID_W
VMEM_LIMIT = 48 * 1024 * 1024

F32 = jnp.float32
BF16 = jnp.bfloat16


def _rms(x, g):
    return x * lax.rsqrt(jnp.mean(x * x, axis=-1, keepdims=True) + NORM_EPS) * g


def _adaln_kernel(c_ref, w_ref, b_ref, o_ref):
    c = c_ref[...]
    s = c * jax.nn.sigmoid(c)
    o_ref[...] = jnp.dot(s, w_ref[...], precision=lax.Precision.HIGHEST,
                         preferred_element_type=F32) + b_ref[...]


def _adaln(cond8, w_mod, b_mod):
    n_out = w_mod.shape[1]
    bn = 1024
    return pl.pallas_call(
        _adaln_kernel,
        grid=(n_out // bn,),
        in_specs=[pl.BlockSpec((8, D_MODEL), lambda n: (0, 0)),
                  pl.BlockSpec((D_MODEL, bn), lambda n: (0, n)),
                  pl.BlockSpec((1, bn), lambda n: (0, n))],
        out_specs=pl.BlockSpec((8, bn), lambda n: (0, n)),
        out_shape=jax.ShapeDtypeStruct((8, n_out), F32),
        name="adaln",
    )(cond8, w_mod, b_mod.reshape(1, n_out))


NA_VARIANTS = ((-1, False), (0, False), (1, False), (0, True), (1, True), (-1, True), (0, None))
V_PREV, V_CUR, V_NEXT, V_CUR_ALL, V_NEXT_ALL, V_PREV_ALL, V_NONE = range(7)
N_DR = 2 * NA_WIN_H - 1
N_DC = 2 * NA_WIN_W - 1


def _na_bias_kernel(rel_ref, o_ref):
    h = pl.program_id(0)
    kc = lax.broadcasted_iota(jnp.int32, (GRID_W, TOK_BLOCK), 0)
    qq = lax.broadcasted_iota(jnp.int32, (GRID_W, TOK_BLOCK), 1)
    qc = qq % GRID_W
    qblk = qq // GRID_W
    col_start = jnp.clip(qc - NA_WIN_W // 2, 0, GRID_W - NA_WIN_W)
    col_in = (kc >= col_start) & (kc < col_start + NA_WIN_W)
    dc = kc - qc + (NA_WIN_W - 1)
    tables = []
    for d in range(N_DR):
        t = jnp.full((GRID_W, TOK_BLOCK), MASK_VALUE, F32)
        base = (h * N_DR + d) * N_DC
        for m in range(N_DC):
            t = jnp.where(dc == m, rel_ref[base + m] * LOG2E, t)
        tables.append(jnp.where(col_in, t, MASK_VALUE))
    for v, (boff, all_valid) in enumerate(NA_VARIANTS):
        for cb in range(ROWS_PER_BLOCK):
            tile = jnp.full((GRID_W, TOK_BLOCK), MASK_VALUE, F32)
            if all_valid is not None:
                for a in range(ROWS_PER_BLOCK):
                    dr = ROWS_PER_BLOCK * boff + cb - a
                    if all_valid or (-(NA_WIN_H // 2) <= dr < NA_WIN_H - NA_WIN_H // 2):
                        tile = jnp.where(qblk == a, tables[dr + NA_WIN_H - 1], tile)
            o_ref[v, 0, cb * GRID_W:(cb + 1) * GRID_W, :] = tile


def _na_bias(rel_bias):
    return pl.pallas_call(
        _na_bias_kernel,
        grid=(NA_HEADS,),
        in_specs=[pl.BlockSpec(memory_space=pltpu.SMEM)],
        out_specs=pl.BlockSpec((len(NA_VARIANTS), 1, TOK_BLOCK, TOK_BLOCK), lambda h: (0, h, 0, 0)),
        out_shape=jax.ShapeDtypeStruct((len(NA_VARIANTS), NA_HEADS, TOK_BLOCK, TOK_BLOCK), F32),
        name="na_bias",
    )(rel_bias.reshape(-1))


C_QLAT = 0
C_NAQ = Q_LORA
C_KVLAT = C_NAQ + NA_HEADS * NA_DIM
C_KROPE = C_KVLAT + KV_LORA
C_NAK = C_KROPE + LANES
C_NAV = C_NAK + NA_HEADS * NA_DIM
C_END = C_NAV + NA_HEADS * NA_DIM
MLA_SCALE = MLA_QK ** -0.5 * LOG2E
NA_SCALE = NA_DIM ** -0.5 * LOG2E
ONES_LANE = MLA_V


def _rope(x, c, sa, sb):
    return x * c + pltpu.roll(x, LANES - MLA_ROPE // 2, 1) * sa + pltpu.roll(x, MLA_ROPE // 2, 1) * sb


def _proj_kernel(x_ref, ctx_ref, mod_ref, g_ref, win_ref, qg_ref, kvg_ref, wuq_ref, wukv_ref,
                 c_ref, sa_ref, sb_ref,
                 q_out, k_out, v_out, nq_out, nk_out, nv_out, h_scr, proj_scr):
    j = pl.program_id(1)

    @pl.when(j == 0)
    def _():
        h_scr[...] = jnp.zeros(h_scr.shape, h_scr.dtype)
        proj_scr[...] = jnp.zeros(proj_scr.shape, proj_scr.dtype)

    proj = proj_scr
    c, sa, sb = c_ref[...], sa_ref[...], sb_ref[...]

    qn = _rms(proj[:, C_QLAT:C_NAQ], qg_ref[...]).astype(BF16)
    q = jnp.dot(qn, wuq_ref[...], preferred_element_type=F32) * MLA_SCALE
    for h in range(MLA_HEADS):
        q_out[0, h] = _rope(q[:, h * LANES:(h + 1) * LANES], c, sa, sb).astype(BF16)

    kvn = _rms(proj[:, C_KVLAT:C_KROPE], kvg_ref[...]).astype(BF16)
    kv = jnp.dot(kvn, wukv_ref[...], preferred_element_type=F32)
    kr = _rope(proj[:, C_KROPE:C_NAK], c, sa, sb)
    v0 = MLA_HEADS * LANES
    lane = lax.broadcasted_iota(jnp.int32, (TOK_BLOCK, LANES), 1)
    for h in range(MLA_HEADS):
        k_out[0, h] = (kv[:, h * LANES:(h + 1) * LANES] + kr).astype(BF16)
        v = jnp.where(lane == ONES_LANE, 1.0, kv[:, v0 + h * LANES:v0 + (h + 1) * LANES])
        v_out[0, h] = jnp.transpose(v).astype(BF16)
    for p in range(HEAD_PAIRS):
        nq_out[0, p] = (proj[:, C_NAQ + p * LANES:C_NAQ + (p + 1) * LANES] * NA_SCALE).astype(BF16)
        nk_out[0, p] = proj[:, C_NAK + p * LANES:C_NAK + (p + 1) * LANES].astype(BF16)
        nv_pair = proj[:, C_NAV + p * LANES:C_NAV + (p + 1) * LANES]
        for half in range(2):
            nv = pltpu.roll(nv_pair, NA_DIM, 1) if half else nv_pair
            nv = jnp.where(lane < NA_DIM, nv, jnp.where(lane == ONES_LANE, 1.0, 0.0))
            nv_out[0, 2 * p + half] = jnp.transpose(nv).astype(BF16)

    proj_scr[...] = jnp.dot(h_scr[...], win_ref[...], preferred_element_type=F32)
    xt = jnp.where(j == 0, ctx_ref[0], x_ref[0])
    h_scr[...] = (_rms(xt, g_ref[...]) * (1.0 + mod_ref[0, 1:2, :]) + mod_ref[0, 0:1, :]).astype(BF16)


def _proj(x, ctx, mod, g, win, qg, kvg, wuq, wukv, rc, rsa, rsb):
    B, N, D = x.shape
    nb = N // TOK_BLOCK
    ntot = N + CTX_LEN
    full = lambda shape: pl.BlockSpec(shape, lambda b, j: (0,) * len(shape))
    depth = 2
    tile_in = lambda j: jnp.minimum(j, nb)
    tile_out = lambda j: jnp.clip(j - depth, 0, nb)
    lat = lambda b, j: (b, 0, jnp.maximum(tile_out(j) - 1, 0), 0)
    alltok = lambda b, j: (b, 0, tile_out(j), 0)
    feat = lambda b, j: (b, 0, 0, tile_out(j))
    rope = lambda b, j: (tile_out(j), 0)
    return pl.pallas_call(
        _proj_kernel,
        grid=(B, nb + 1 + depth),
        in_specs=[pl.BlockSpec((1, TOK_BLOCK, D), lambda b, j: (b, jnp.maximum(tile_in(j) - 1, 0), 0)),
                  pl.BlockSpec((1, CTX_LEN, D), lambda b, j: (b, 0, 0)),
                  pl.BlockSpec((1, 6, D), lambda b, j: (jnp.where(j == 0, B, b), 0, 0)),
                  full((1, D)), full(win.shape), full((1, Q_LORA)), full((1, KV_LORA)),
                  full(wuq.shape), full(wukv.shape),
                  pl.BlockSpec((TOK_BLOCK, LANES), rope),
                  pl.BlockSpec((TOK_BLOCK, LANES), rope),
                  pl.BlockSpec((TOK_BLOCK, LANES), rope)],
        out_specs=[pl.BlockSpec((1, MLA_HEADS, TOK_BLOCK, LANES), lat),
                   pl.BlockSpec((1, MLA_HEADS, TOK_BLOCK, LANES), alltok),
                   pl.BlockSpec((1, MLA_HEADS, LANES, TOK_BLOCK), feat),
                   pl.BlockSpec((1, HEAD_PAIRS, TOK_BLOCK, LANES), lat),
                   pl.BlockSpec((1, HEAD_PAIRS, TOK_BLOCK, LANES), alltok),
                   pl.BlockSpec((1, NA_HEADS, LANES, TOK_BLOCK), feat)],
        out_shape=[jax.ShapeDtypeStruct((B, MLA_HEADS, N, LANES), BF16),
                   jax.ShapeDtypeStruct((B, MLA_HEADS, ntot, LANES), BF16),
                   jax.ShapeDtypeStruct((B, MLA_HEADS, LANES, ntot), BF16),
                   jax.ShapeDtypeStruct((B, HEAD_PAIRS, N, LANES), BF16),
                   jax.ShapeDtypeStruct((B, HEAD_PAIRS, ntot, LANES), BF16),
                   jax.ShapeDtypeStruct((B, NA_HEADS, LANES, ntot), BF16)],
        scratch_shapes=[pltpu.VMEM((TOK_BLOCK, D), BF16), pltpu.VMEM((TOK_BLOCK, C_END), F32)],
        compiler_params=pltpu.CompilerParams(vmem_limit_bytes=VMEM_LIMIT),
        name="proj",
    )(x, ctx, mod, g, win, qg, kvg, wuq, wukv, rc, rsa, rsb)


MLA_TQ = 1024
MLA_TK = 768


def _mla_kernel(q_ref, k_ref, v_ref, wg_ref, wu_ref, wd_ref, o_ref, wg_out, wu_out, wd_out,
                m_scr, alpha_scr, acc_scr, p_scr, s_scr, smax_scr):
    wg_out[...] = wg_ref[...].astype(BF16)
    wu_out[...] = wu_ref[...].astype(BF16)
    wd_out[...] = wd_ref[...].astype(BF16)

    n_chunks = k_ref.shape[2] // MLA_TK
    m_scr[...] = jnp.full(m_scr.shape, -jnp.inf, F32)
    acc_scr[...] = jnp.zeros(acc_scr.shape, F32)

    def scores(ci, hh):
        off = pl.multiple_of(ci * MLA_TK, MLA_TK)
        k = k_ref[0, hh, pl.ds(off, MLA_TK), :]
        s = lax.dot_general(k, q_ref[0, hh], (((1,), (1,)), ((), ())), preferred_element_type=F32)
        s_scr[hh] = s
        smax_scr[hh] = jnp.max(s, axis=0, keepdims=True)

    def softmax(hh):
        m_prev = m_scr[hh]
        m_next = jnp.maximum(m_prev, smax_scr[hh])
        p_scr[hh] = jnp.exp2((s_scr[hh] - m_next).astype(BF16))
        alpha_scr[hh] = jnp.exp2(m_prev - m_next)
        m_scr[hh] = m_next

    def values(ci, hh):
        off = pl.multiple_of(ci * MLA_TK, MLA_TK)
        vt = v_ref[0, hh, :, pl.ds(off, MLA_TK)]
        acc_scr[hh] = alpha_scr[hh] * acc_scr[hh] + jnp.dot(vt, p_scr[hh], preferred_element_type=F32)

    for hh in range(2):
        scores(0, hh)
    for hh in range(2):
        softmax(hh)
        scores(1, hh)

    def body(ci, carry):
        for hh in range(2):
            values(ci - 1, hh)
            softmax(hh)
            scores(ci + 1, hh)
        return carry

    lax.fori_loop(1, n_chunks - 1, body, 0)
    for hh in range(2):
        values(n_chunks - 2, hh)
        softmax(hh)
    for hh in range(2):
        values(n_chunks - 1, hh)
    outs = []
    for hh in range(2):
        acc = acc_scr[hh]
        outs.append(jnp.transpose(acc / acc[ONES_LANE:ONES_LANE + 1, :]))
    lane = lax.broadcasted_iota(jnp.int32, (MLA_TQ, LANES), 1)
    o_ref[0] = jnp.where(lane < MLA_V, outs[0], pltpu.roll(outs[1], MLA_V, 1)).astype(o_ref.dtype)


def _mla(q_m, k_m, v_m, w_gate, w_up, w_down):
    B, _, N, _ = q_m.shape
    ntot = k_m.shape[2]
    n_q = N // MLA_TQ
    steps = B * HEAD_PAIRS * n_q
    flat = [w.reshape(-1, w.shape[-1]) for w in (w_gate, w_up, w_down)]
    wblk = [(w.shape[0] // steps, w.shape[1]) for w in flat]
    step = lambda b, p, i: ((b * HEAD_PAIRS + p) * n_q + i, 0)
    outs = pl.pallas_call(
        _mla_kernel,
        grid=(B, HEAD_PAIRS, n_q),
        in_specs=[pl.BlockSpec((1, 2, MLA_TQ, LANES), lambda b, p, i: (b, p, i, 0)),
                  pl.BlockSpec((1, 2, ntot, LANES), lambda b, p, i: (b, p, 0, 0)),
                  pl.BlockSpec((1, 2, LANES, ntot), lambda b, p, i: (b, p, 0, 0))]
        + [pl.BlockSpec(blk, step) for blk in wblk],
        out_specs=[pl.BlockSpec((1, MLA_TQ, LANES), lambda b, p, i: (b, i, p))]
        + [pl.BlockSpec(blk, step) for blk in wblk],
        out_shape=[jax.ShapeDtypeStruct((B, N, MLA_HEADS * MLA_V), BF16)]
        + [jax.ShapeDtypeStruct(w.shape, BF16) for w in flat],
        scratch_shapes=[pltpu.VMEM((2, 1, MLA_TQ), F32), pltpu.VMEM((2, 1, MLA_TQ), F32),
                        pltpu.VMEM((2, LANES, MLA_TQ), F32), pltpu.VMEM((2, MLA_TK, MLA_TQ), BF16),
                        pltpu.VMEM((2, MLA_TK, MLA_TQ), F32), pltpu.VMEM((2, 1, MLA_TQ), F32)],
        compiler_params=pltpu.CompilerParams(vmem_limit_bytes=VMEM_LIMIT),
        name="mla",
    )(q_m, k_m, v_m, *flat)
    return (outs[0],) + tuple(o.reshape(w.shape) for o, w in zip(outs[1:], (w_gate, w_up, w_down)))


NA_UNIT_HEADS = 8
NA_UNITS_PER_BLOCK = NA_HEADS // NA_UNIT_HEADS
NA_KEYS = 4 * TOK_BLOCK


def _na_kernel(q_ref, kc_ref, kp_ref, k0_ref, kn_ref, bp_ref, b0_ref, bn_ref,
               vc_ref, vp_ref, v0_ref, vn_ref, o_ref, s_scr, smax_scr, p_scr):
    @pl.when(pl.program_id(1) == 0)
    def _():
        s_scr[...] = jnp.zeros(s_scr.shape, F32)
        smax_scr[...] = jnp.zeros(smax_scr.shape, F32)
        p_scr[...] = jnp.zeros(p_scr.shape, BF16)

    lane = lax.broadcasted_iota(jnp.int32, (TOK_BLOCK, LANES), 1)

    for pp in range(NA_UNIT_HEADS // 2):
        outs = []
        for half in range(2):
            hq = 2 * pp + half
            vt = jnp.concatenate([vc_ref[0, hq], vp_ref[0, hq], v0_ref[0, hq], vn_ref[0, hq]], axis=1)
            o = jnp.dot(vt, p_scr[hq], preferred_element_type=F32)
            outs.append(jnp.transpose(o / o[ONES_LANE:ONES_LANE + 1, :]))
        o_ref[0, :, pp * LANES:(pp + 1) * LANES] = jnp.where(
            lane < NA_DIM, outs[0], pltpu.roll(outs[1], NA_DIM, 1)).astype(o_ref.dtype)

    for hq in range(NA_UNIT_HEADS):
        p_scr[hq] = jnp.exp2((s_scr[hq] - smax_scr[hq]).astype(BF16))

    dn = (((1,), (1,)), ((), ()))
    for pp in range(NA_UNIT_HEADS // 2):
        keys = jnp.concatenate([kc_ref[0, pp], kp_ref[0, pp], k0_ref[0, pp], kn_ref[0, pp]], axis=0)
        for half in range(2):
            hq = 2 * pp + half
            in_half = (lane >= NA_DIM) if half else (lane < NA_DIM)
            q = jnp.where(in_half, q_ref[0, pp], jnp.zeros((), BF16))
            s = lax.dot_general(keys, q, dn, preferred_element_type=F32)
            parts = [s[:TOK_BLOCK],
                     s[TOK_BLOCK:2 * TOK_BLOCK] + bp_ref[0, hq],
                     s[2 * TOK_BLOCK:3 * TOK_BLOCK] + b0_ref[0, hq],
                     s[3 * TOK_BLOCK:] + bn_ref[0, hq]]
            for j, part in enumerate(parts):
                s_scr[hq, j * TOK_BLOCK:(j + 1) * TOK_BLOCK, :] = part
            smax_scr[hq] = jnp.max(jnp.maximum(jnp.maximum(parts[0], parts[1]),
                                               jnp.maximum(parts[2], parts[3])), axis=0, keepdims=True)


def _natten(nq, nk, nv, bias):
    B, _, N, _ = nq.shape
    nb = N // TOK_BLOCK
    last = nb - 1
    n_units = nb * NA_UNITS_PER_BLOCK
    pipeline_depth = 2

    def unit(t, lag):
        u = jnp.clip(t - lag, 0, n_units - 1)
        return u // NA_UNITS_PER_BLOCK, u % NA_UNITS_PER_BLOCK

    tok_blocks = (lambda i: 0, lambda i: jnp.maximum(i - 1, 0) + 1, lambda i: i + 1,
                  lambda i: jnp.minimum(i + 1, last) + 1)
    variants = (lambda i: jnp.where(i == 0, V_NONE, jnp.where(i == last, V_PREV_ALL, V_PREV)),
                lambda i: jnp.where((i == 0) | (i == last), V_CUR_ALL, V_CUR),
                lambda i: jnp.where(i == 0, V_NEXT_ALL, jnp.where(i == last, V_NONE, V_NEXT)))

    def q_map(b, t):
        i, g = unit(t, 0)
        return b, g, i, 0

    def key_map(tok):
        def f(b, t):
            i, g = unit(t, 0)
            return b, g, tok(i), 0
        return f

    def bias_map(var):
        def f(b, t):
            i, g = unit(t, 0)
            return var(i), g, 0, 0
        return f

    def value_map(tok):
        def f(b, t):
            i, g = unit(t, pipeline_depth)
            return b, g, 0, tok(i)
        return f

    def out_map(b, t):
        i, g = unit(t, pipeline_depth)
        return b, i, g

    pair_blk = (1, NA_UNIT_HEADS // 2, TOK_BLOCK, LANES)
    return pl.pallas_call(
        _na_kernel,
        grid=(B, n_units + pipeline_depth),
        in_specs=([pl.BlockSpec(pair_blk, q_map)]
                  + [pl.BlockSpec(pair_blk, key_map(tok)) for tok in tok_blocks]
                  + [pl.BlockSpec((1, NA_UNIT_HEADS, TOK_BLOCK, TOK_BLOCK), bias_map(var)) for var in variants]
                  + [pl.BlockSpec((1, NA_UNIT_HEADS, LANES, TOK_BLOCK), value_map(tok)) for tok in tok_blocks]),
        out_specs=pl.BlockSpec((1, TOK_BLOCK, NA_UNIT_HEADS * NA_DIM), out_map),
        out_shape=jax.ShapeDtypeStruct((B, N, NA_HEADS * NA_DIM), BF16),
        scratch_shapes=[pltpu.VMEM((NA_UNIT_HEADS, NA_KEYS, TOK_BLOCK), F32),
                        pltpu.VMEM((NA_UNIT_HEADS, 1, TOK_BLOCK), F32),
                        pltpu.VMEM((NA_UNIT_HEADS, NA_KEYS, TOK_BLOCK), BF16)],
        compiler_params=pltpu.CompilerParams(vmem_limit_bytes=VMEM_LIMIT),
        name="natten",
    )(nq, nk, nk, nk, nk, bias, bias, bias, nv, nv, nv, nv)


OUT_TM = 512
R_G1, R_SH2, R_SC2, R_G2 = 2, 3, 4, 5
GROUP_LANE = N_EXPERTS


def _first_max_lane(x, valid, lane):
    xm = jnp.where(valid, x, -jnp.inf)
    mx = jnp.max(xm, axis=1, keepdims=True)
    idx = jnp.min(jnp.where(valid & (xm == mx), lane, LANES), axis=1, keepdims=True)
    return mx, idx


def _outproj_kernel(om_ref, on_ref, x_ref, mod_ref, wo_ref, g_ref, wr_ref, br_ref,
                    x1_out, h2_out, comb_out, h2_scr):
    @pl.when(pl.program_id(0) == 0)
    def _():
        h2_scr[...] = jnp.zeros(h2_scr.shape, h2_scr.dtype)

    h2_prev = h2_scr[...]
    h2_hi = h2_prev.astype(BF16)
    h2_lo = (h2_prev - h2_hi.astype(F32)).astype(BF16)
    both = jnp.dot(h2_hi, wr_ref[...], preferred_element_type=F32)
    logits = (both[:, :LANES] + both[:, LANES:]
              + jnp.dot(h2_lo, wr_ref[:, :LANES], preferred_element_type=F32) + br_ref[...])
    lane = lax.broadcasted_iota(jnp.int32, logits.shape, 1)
    is_g = lane < N_GROUPS
    g_max, g_sel = _first_max_lane(logits, is_g, lane)
    g_w = 1.0 / jnp.sum(jnp.where(is_g, jnp.exp(logits - g_max), 0.0), axis=1, keepdims=True)
    e_lo = N_GROUPS + g_sel * EXPERTS_PER_GROUP
    in_grp = (lane >= e_lo) & (lane < e_lo + EXPERTS_PER_GROUP)
    m1, i1 = _first_max_lane(logits, in_grp, lane)
    m2, i2 = _first_max_lane(logits, in_grp & (lane != i1), lane)
    e2 = jnp.exp(m2 - m1)
    w1 = 1.0 / (1.0 + e2)
    w2 = e2 / (1.0 + e2)
    comb = jnp.where(lane == i1, g_w * w1, jnp.where(lane == i2, g_w * w2, 0.0))
    comb_out[...] = jnp.where(lane == GROUP_LANE, g_sel.astype(F32), pltpu.roll(comb, LANES - N_GROUPS, 1))

    half = wo_ref.shape[0] // 2
    mixed = (jnp.dot(om_ref[...], wo_ref[:half, :], preferred_element_type=F32)
             + jnp.dot(on_ref[...], wo_ref[half:, :], preferred_element_type=F32))
    x1 = x_ref[...] + mod_ref[0, R_G1:R_G1 + 1, :] * mixed
    x1_out[...] = x1
    h2 = _rms(x1, g_ref[...]) * (1.0 + mod_ref[0, R_SC2:R_SC2 + 1, :]) + mod_ref[0, R_SH2:R_SH2 + 1, :]
    h2_out[...] = h2.astype(BF16)
    h2_scr[...] = h2


def _outproj(o_mla, o_na, x2d, mod, wo, g, wr, br, n_per_batch):
    T, D = x2d.shape
    per_b = n_per_batch // OUT_TM
    nt = T // OUT_TM
    cur = lambda t: jnp.minimum(t, nt - 1)
    tok = lambda w: pl.BlockSpec((OUT_TM, w), lambda t: (cur(t), 0))
    full = lambda shape: pl.BlockSpec(shape, lambda t: (0,) * len(shape))
    return pl.pallas_call(
        _outproj_kernel,
        grid=(nt + 1,),
        in_specs=[tok(o_mla.shape[1]), tok(o_na.shape[1]), tok(D),
                  pl.BlockSpec((1, 6, D), lambda t: (cur(t) // per_b, 0, 0)),
                  full(wo.shape), full((1, D)), full(wr.shape), full(br.shape)],
        out_specs=[tok(D), tok(D), pl.BlockSpec((OUT_TM, LANES), lambda t: (jnp.maximum(t - 1, 0), 0))],
        out_shape=[jax.ShapeDtypeStruct((T, D), F32), jax.ShapeDtypeStruct((T, D), BF16),
                   jax.ShapeDtypeStruct((T, LANES), F32)],
        scratch_shapes=[pltpu.VMEM((OUT_TM, D), F32)],
        compiler_params=pltpu.CompilerParams(vmem_limit_bytes=VMEM_LIMIT),
        name="outproj",
    )(o_mla, o_na, x2d, mod, wo, g, wr, br)


MOE_TT = 1024
MOE_CH = 128
MOE_NBLK = MOE_TT // MOE_CH + (N_GROUPS - 1)
MOE_TTP = MOE_NBLK * MOE_CH


def _moe_sort_kernel(h_ref, comb_ref, xs_out, cs_out, pos_out, cnt_out, tri_scr):
    @pl.when(pl.program_id(0) == 0)
    def _():
        r = lax.broadcasted_iota(jnp.int32, tri_scr.shape, 0)
        c = lax.broadcasted_iota(jnp.int32, tri_scr.shape, 1)
        tri_scr[...] = jnp.where(c < r, 1.0, 0.0).astype(BF16)

    comb = comb_ref[...]
    lane = lax.broadcasted_iota(jnp.int32, comb.shape, 1)
    lane_f = lane.astype(F32)
    gid = jnp.sum(jnp.where(lane == GROUP_LANE, comb, 0.0), axis=1, keepdims=True)
    onehot = jnp.where((lane_f == gid) & (lane < N_GROUPS), 1.0, 0.0)
    ahead = jnp.dot(tri_scr[...], onehot.astype(BF16), preferred_element_type=F32)
    n = jnp.sum(onehot, axis=0, keepdims=True)
    padded = jnp.ceil(n * (1.0 / MOE_CH)) * MOE_CH
    start = jnp.zeros_like(padded)
    for k in range(1, N_GROUPS):
        start = start + jnp.where(lane[:1] >= k, pltpu.roll(padded, k, 1), 0.0)
    pos = jnp.sum(onehot * (ahead + start), axis=1, keepdims=True)
    pos_rep = jnp.broadcast_to(pos, comb.shape)
    pos_out[...] = pos_rep
    cnt_out[0] = jnp.broadcast_to(n, (8, LANES))

    pos_row = jnp.transpose(pos_rep)[0:1, :]
    row = lax.broadcasted_iota(jnp.int32, (MOE_TTP, MOE_TT), 0).astype(F32)
    perm = jnp.where(row == pos_row, 1.0, 0.0).astype(BF16)
    xs_out[...] = jnp.dot(perm, h_ref[...], preferred_element_type=F32).astype(BF16)
    hi = comb.astype(BF16)
    lo = (comb - hi.astype(F32)).astype(BF16)
    cs_out[...] = (jnp.dot(perm, hi, preferred_element_type=F32)
                   + jnp.dot(perm, lo, preferred_element_type=F32))


def _moe_sort(h2, comb):
    T, D = h2.shape
    nt = T // MOE_TT
    return pl.pallas_call(
        _moe_sort_kernel,
        grid=(nt,),
        in_specs=[pl.BlockSpec((MOE_TT, D), lambda t: (t, 0)),
                  pl.BlockSpec((MOE_TT, LANES), lambda t: (t, 0))],
        out_specs=[pl.BlockSpec((MOE_TTP, D), lambda t: (t, 0)),
                   pl.BlockSpec((MOE_TTP, LANES), lambda t: (t, 0)),
                   pl.BlockSpec((MOE_TT, LANES), lambda t: (t, 0)),
                   pl.BlockSpec((1, 8, LANES), lambda t: (t, 0, 0))],
        out_shape=[jax.ShapeDtypeStruct((nt * MOE_TTP, D), BF16),
                   jax.ShapeDtypeStruct((nt * MOE_TTP, LANES), F32),
                   jax.ShapeDtypeStruct((T, LANES), F32),
                   jax.ShapeDtypeStruct((nt, 8, LANES), F32)],
        scratch_shapes=[pltpu.VMEM((MOE_TT, MOE_TT), BF16)],
        compiler_params=pltpu.CompilerParams(vmem_limit_bytes=VMEM_LIMIT),
        name="moe_sort",
    )(h2, comb)


def _moe_experts_kernel(blk_ref, grp_ref, nvalid_ref, x_ref, c_ref, wg_ref, wu_ref, wd_ref, y_ref, hid_scr):
    s = pl.program_id(0)
    n_slots = pl.num_programs(0) - 1

    @pl.when(s == 0)
    def _():
        hid_scr[...] = jnp.zeros(hid_scr.shape, hid_scr.dtype)

    @pl.when(s > nvalid_ref[0])
    def _():
        y_ref[...] = jnp.zeros(y_ref.shape, y_ref.dtype)

    @pl.when(s <= nvalid_ref[0])
    def _():
        y = None
        for e in range(EXPERTS_PER_GROUP):
            part = jnp.dot(hid_scr[e], wd_ref[e], preferred_element_type=F32)
            y = part if y is None else y + part
        y_ref[...] = y.astype(y_ref.dtype)

        x = x_ref[...]
        comb = c_ref[...]
        lane = lax.broadcasted_iota(jnp.int32, comb.shape, 1)
        e0 = grp_ref[jnp.minimum(s, n_slots - 1)] * EXPERTS_PER_GROUP
        for e in range(EXPERTS_PER_GROUP):
            gate = jnp.dot(x, wg_ref[e], preferred_element_type=F32)
            up = jnp.dot(x, wu_ref[e], preferred_element_type=F32)
            w = jnp.sum(jnp.where(lane == e0 + e, comb, 0.0), axis=1, keepdims=True)
            hid_scr[e] = (gate * jax.nn.sigmoid(gate) * up * w).astype(BF16)


def _moe_experts(blk, grp, nvalid, xs, cs, wg, wu, wd):
    D = xs.shape[1]
    n_slots = blk.shape[0]

    def cur(arr):
        return lambda s, blk, grp, nvalid: (arr(blk, grp)[jnp.minimum(s, n_slots - 1)],)

    def prev(arr):
        return lambda s, blk, grp, nvalid: (arr(blk, grp)[jnp.maximum(s - 1, 0)],)

    blk_of = lambda blk, grp: blk
    grp_of = lambda blk, grp: grp
    rows = lambda f: (lambda *a: f(*a) + (0,))
    wts = lambda f: (lambda *a: f(*a) + (0, 0))
    return pl.pallas_call(
        _moe_experts_kernel,
        grid_spec=pltpu.PrefetchScalarGridSpec(
            num_scalar_prefetch=3,
            grid=(n_slots + 1,),
            in_specs=[pl.BlockSpec((MOE_CH, D), rows(cur(blk_of))),
                      pl.BlockSpec((MOE_CH, LANES), rows(cur(blk_of))),
                      pl.BlockSpec((EXPERTS_PER_GROUP, D, D_EXPERT), wts(cur(grp_of))),
                      pl.BlockSpec((EXPERTS_PER_GROUP, D, D_EXPERT), wts(cur(grp_of))),
                      pl.BlockSpec((EXPERTS_PER_GROUP, D_EXPERT, D), wts(prev(grp_of)))],
            out_specs=pl.BlockSpec((MOE_CH, D), rows(prev(blk_of))),
            scratch_shapes=[pltpu.VMEM((EXPERTS_PER_GROUP, MOE_CH, D_EXPERT), BF16)],
        ),
        out_shape=jax.ShapeDtypeStruct(xs.shape, BF16),
        compiler_params=pltpu.CompilerParams(vmem_limit_bytes=VMEM_LIMIT),
        name="moe_experts",
    )(blk, grp, nvalid, xs, cs, wg, wu, wd)


def _moe_final_kernel(ys_ref, pos_ref, x1_ref, mod_ref, fg_ref, o_ref):
    col = lax.broadcasted_iota(jnp.int32, (MOE_TT, MOE_TTP), 1).astype(F32)
    unperm = jnp.where(col == pltpu.repeat(pos_ref[...], MOE_NBLK, axis=1), 1.0, 0.0).astype(BF16)
    y = jnp.dot(unperm, ys_ref[...], preferred_element_type=F32)
    o_ref[...] = _rms(x1_ref[...] + mod_ref[0, R_G2:R_G2 + 1, :] * y, fg_ref[...])


def _moe_final(ys, pos, x1, mod, fg, n_per_batch):
    T, D = x1.shape
    per_b = n_per_batch // MOE_TT
    return pl.pallas_call(
        _moe_final_kernel,
        grid=(T // MOE_TT,),
        in_specs=[pl.BlockSpec((MOE_TTP, D), lambda t: (t, 0)),
                  pl.BlockSpec((MOE_TT, LANES), lambda t: (t, 0)),
                  pl.BlockSpec((MOE_TT, D), lambda t: (t, 0)),
                  pl.BlockSpec((1, 6, D), lambda t: (t // per_b, 0, 0)),
                  pl.BlockSpec((1, D), lambda t: (0, 0))],
        out_specs=pl.BlockSpec((MOE_TT, D), lambda t: (t, 0)),
        out_shape=jax.ShapeDtypeStruct((T, D), F32),
        compiler_params=pltpu.CompilerParams(vmem_limit_bytes=VMEM_LIMIT),
        name="moe_final",
    )(ys, pos, x1, mod, fg)


def _chunk_schedule(counts):
    nt = counts.shape[0]
    nchunks = jnp.ceil(counts * (1.0 / MOE_CH)).astype(jnp.int32)
    ends = jnp.cumsum(nchunks, axis=1)
    j = jnp.arange(MOE_NBLK, dtype=jnp.int32)
    grp = jnp.sum(j[None, :, None] >= ends[:, None, :], axis=2).astype(jnp.int32).reshape(-1)
    blk = (jnp.arange(nt, dtype=jnp.int32)[:, None] * MOE_NBLK + j[None, :]).reshape(-1)
    order = jnp.argsort(grp, stable=True)
    nvalid = jnp.sum(grp < N_GROUPS).astype(jnp.int32).reshape(1)
    return blk[order], jnp.minimum(grp, N_GROUPS - 1)[order], nvalid


def _rope_tables(n):
    pos = np.arange(n)
    half = MLA_ROPE // 2
    inv_freq = (ROPE_THETA ** (-np.arange(0, half, 2, dtype=np.float32) / half)).astype(np.float32)
    ang = np.concatenate([(pos // GRID_W).astype(np.float32)[:, None] * inv_freq,
                          (pos % GRID_W).astype(np.float32)[:, None] * inv_freq], axis=-1)
    cos, sin = np.cos(ang).astype(np.float32), np.sin(ang).astype(np.float32)
    one = np.ones((n, MLA_NOPE), np.float32)
    zero = np.zeros((n, MLA_NOPE), np.float32)
    pad = np.zeros((n, LANES - MLA_QK), np.float32)
    z16 = np.zeros((n, half), np.float32)
    c = np.concatenate([one, cos, cos, pad], axis=1)
    sa = np.concatenate([zero, -sin, z16, pad], axis=1)
    sb = np.concatenate([zero, z16, sin, pad], axis=1)
    ident = np.concatenate([np.ones((CTX_LEN, MLA_QK), np.float32),
                            np.zeros((CTX_LEN, LANES - MLA_QK), np.float32)], axis=1)
    zc = np.zeros((CTX_LEN, LANES), np.float32)
    return (np.concatenate([ident, c], axis=0), np.concatenate([zc, sa], axis=0),
            np.concatenate([zc, sb], axis=0))


def kernel(x, c, ctx, c_ctx, w_mod, b_mod, norm_attn_g, norm_ffn_g, w_in, q_a_norm_g, kv_a_norm_g, w_uq, w_ukv,
           na_rel_bias, w_out, w_router_group, b_router_group, w_router_expert, b_router_expert, w_gate, w_up,
           w_down, final_norm_g):
    B, N, D = x.shape
    assert w_mod.shape[0] == 1, "single-layer kernel"
    assert (B, N, D) == (2, 8192, D_MODEL) and ctx.shape == (B, CTX_LEN, D)

    cond8 = jnp.concatenate([c, c_ctx[None, :], jnp.zeros((8 - B - 1, D), F32)], axis=0)
    mod = _adaln(cond8, w_mod[0], b_mod[0]).reshape(8, 6, D)

    wi = w_in[0]
    na_w = NA_HEADS * NA_DIM
    k0 = Q_LORA + na_w + KV_LORA
    win = jnp.concatenate([wi[:, :k0], jnp.zeros((D, MLA_NOPE), F32), wi[:, k0:k0 + MLA_ROPE],
                           jnp.zeros((D, LANES - MLA_QK), F32), wi[:, k0 + MLA_ROPE:]], axis=1).astype(BF16)
    assert win.shape[1] == C_END
    wuq = jnp.pad(w_uq[0].reshape(Q_LORA, MLA_HEADS, MLA_QK), ((0, 0), (0, 0), (0, LANES - MLA_QK)))
    wuq = wuq.reshape(Q_LORA, MLA_HEADS * LANES).astype(BF16)
    wkv = w_ukv[0].reshape(KV_LORA, MLA_HEADS, MLA_NOPE + MLA_V)
    wk = jnp.pad(wkv[:, :, :MLA_NOPE], ((0, 0), (0, 0), (0, LANES - MLA_NOPE))).reshape(KV_LORA, MLA_HEADS * LANES)
    wv = jnp.pad(wkv[:, :, MLA_NOPE:], ((0, 0), (0, 0), (0, LANES - MLA_V))).reshape(KV_LORA, MLA_HEADS * LANES)
    wukv = jnp.concatenate([wk, wv], axis=1).astype(BF16)
    rc, rsa, rsb = _rope_tables(N)

    q_m, k_m, v_m, nq, nk, nv = _proj(x, ctx, mod, norm_attn_g, win, q_a_norm_g, kv_a_norm_g, wuq, wukv,
                                      rc, rsa, rsb)
    o_mla, wg_bf, wu_bf, wd_bf = _mla(q_m, k_m, v_m, w_gate[0], w_up[0], w_down[0])
    o_na = _natten(nq, nk, nv, _na_bias(na_rel_bias[0]))

    n_r = N_GROUPS + N_EXPERTS
    wr = jnp.pad(jnp.concatenate([w_router_group[0], w_router_expert[0]], axis=1), ((0, 0), (0, LANES - n_r)))
    wr_hi = lax.bitcast_convert_type(lax.bitcast_convert_type(wr, jnp.uint32) & jnp.uint32(0xFFFF0000), F32)
    wr = jnp.concatenate([wr_hi.astype(BF16), (wr - wr_hi).astype(BF16)], axis=1)
    br = jnp.pad(jnp.concatenate([b_router_group[0], b_router_expert[0]]), (0, LANES - n_r)).reshape(1, LANES)
    T = B * N
    x1, h2, comb = _outproj(o_mla.reshape(T, -1), o_na.reshape(T, -1), x.reshape(T, D), mod,
                            w_out[0].astype(BF16), norm_ffn_g, wr, br, N)
    xs, cs, pos, counts = _moe_sort(h2, comb)
    blk, grp, nvalid = _chunk_schedule(counts[:, 0, :N_GROUPS])
    ys = _moe_experts(blk, grp, nvalid, xs, cs, wg_bf, wu_bf, wd_bf)
    out = _moe_final(ys, pos, x1, mod, final_norm_g.reshape(1, D), N)
    return out.reshape(B, N, D)
```

```python
import jax
import jax.numpy as jnp
import numpy as np
from jax import lax
from jax.experimental import pallas as pl
from jax.experimental.pallas import tpu as pltpu

D_MODEL = 1024
GRID_W = 64
CTX_LEN = 256
MLA_HEADS = 8
MLA_NOPE = 64
MLA_ROPE = 32
MLA_QK = MLA_NOPE + MLA_ROPE
MLA_V = 64
Q_LORA = 384
KV_LORA = 256
NA_HEADS = 8
NA_DIM = 64
NA_WIN_H = 8
NA_WIN_W = 16
N_GROUPS = 4
EXPERTS_PER_GROUP = 4
N_EXPERTS = N_GROUPS * EXPERTS_PER_GROUP
D_EXPERT = 512
ROPE_THETA = 10000.0
NORM_EPS = 1e-6
MASK_VALUE = -1e30
LOG2E = 1.4426950408889634

LANES = 128
SUBLANES = 8
HEAD_PAIRS = 4
TOK_BLOCK = 256
ROWS_PER_BLOCK = TOK_BLOCK // GRID_W
VMEM_LIMIT = 48 * 1024 * 1024

F32 = jnp.float32
BF16 = jnp.bfloat16


def _rms(x, g):
    return x * lax.rsqrt(jnp.mean(x * x, axis=-1, keepdims=True) + NORM_EPS) * g


def _adaln_kernel(c_ref, w_ref, b_ref, o_ref):
    c = c_ref[...]
    s = c * jax.nn.sigmoid(c)
    o_ref[...] = jnp.dot(s, w_ref[...], precision=lax.Precision.HIGHEST,
                         preferred_element_type=F32) + b_ref[...]


def _adaln(cond_rows, w_mod, b_mod):
    n_out = w_mod.shape[1]
    bn = D_MODEL
    return pl.pallas_call(
        _adaln_kernel,
        grid=(n_out // bn,),
        in_specs=[pl.BlockSpec((SUBLANES, D_MODEL), lambda n: (0, 0)),
                  pl.BlockSpec((D_MODEL, bn), lambda n: (0, n)),
                  pl.BlockSpec((1, bn), lambda n: (0, n))],
        out_specs=pl.BlockSpec((SUBLANES, bn), lambda n: (0, n)),
        out_shape=jax.ShapeDtypeStruct((SUBLANES, n_out), F32),
        name="adaln",
    )(cond_rows, w_mod, b_mod.reshape(1, n_out))


NA_VARIANTS = ((-1, False), (0, False), (1, False), (0, True), (1, True), (-1, True), (0, None))
V_PREV, V_CUR, V_NEXT, V_CUR_ALL, V_NEXT_ALL, V_PREV_ALL, V_NONE = range(7)
N_DR = 2 * NA_WIN_H - 1
N_DC = 2 * NA_WIN_W - 1


def _na_bias_kernel(rel_ref, o_ref):
    h = pl.program_id(0)
    kc = lax.broadcasted_iota(jnp.int32, (GRID_W, TOK_BLOCK), 0)
    qq = lax.broadcasted_iota(jnp.int32, (GRID_W, TOK_BLOCK), 1)
    qc = qq % GRID_W
    qblk = qq // GRID_W
    col_start = jnp.clip(qc - NA_WIN_W // 2, 0, GRID_W - NA_WIN_W)
    col_in = (kc >= col_start) & (kc < col_start + NA_WIN_W)
    dc = kc - qc + (NA_WIN_W - 1)
    tables = []
    for d in range(N_DR):
        t = jnp.full((GRID_W, TOK_BLOCK), MASK_VALUE, F32)
        base = (h * N_DR + d) * N_DC
        for m in range(N_DC):
            t = jnp.where(dc == m, rel_ref[base + m] * LOG2E, t)
        tables.append(jnp.where(col_in, t, MASK_VALUE))
    for v, (boff, all_valid) in enumerate(NA_VARIANTS):
        for cb in range(ROWS_PER_BLOCK):
            tile = jnp.full((GRID_W, TOK_BLOCK), MASK_VALUE, F32)
            if all_valid is not None:
                for a in range(ROWS_PER_BLOCK):
                    dr = ROWS_PER_BLOCK * boff + cb - a
                    if all_valid or (-(NA_WIN_H // 2) <= dr < NA_WIN_H - NA_WIN_H // 2):
                        tile = jnp.where(qblk == a, tables[dr + NA_WIN_H - 1], tile)
            o_ref[v, 0, cb * GRID_W:(cb + 1) * GRID_W, :] = tile


def _na_bias(rel_bias):
    return pl.pallas_call(
        _na_bias_kernel,
        grid=(NA_HEADS,),
        in_specs=[pl.BlockSpec(memory_space=pltpu.SMEM)],
        out_specs=pl.BlockSpec((len(NA_VARIANTS), 1, TOK_BLOCK, TOK_BLOCK), lambda h: (0, h, 0, 0)),
        out_shape=jax.ShapeDtypeStruct((len(NA_VARIANTS), NA_HEADS, TOK_BLOCK, TOK_BLOCK), F32),
        name="na_bias",
    )(rel_bias.reshape(-1))


C_QLAT = 0
C_NAQ = Q_LORA
C_KVLAT = C_NAQ + NA_HEADS * NA_DIM
C_KROPE = C_KVLAT + KV_LORA
C_NAK = C_KROPE + LANES
C_NAV = C_NAK + NA_HEADS * NA_DIM
C_END = C_NAV + NA_HEADS * NA_DIM
MLA_SCALE = MLA_QK ** -0.5 * LOG2E
NA_SCALE = NA_DIM ** -0.5 * LOG2E
ONES_LANE = MLA_V


def _rope(x, c, sa, sb):
    return x * c + pltpu.roll(x, LANES - MLA_ROPE // 2, 1) * sa + pltpu.roll(x, MLA_ROPE // 2, 1) * sb


def _proj_kernel(x_ref, ctx_ref, mod_ref, g_ref, win_ref, qg_ref, kvg_ref, wuq_ref, wukv_ref,
                 c_ref, sa_ref, sb_ref,
                 q_out, k_out, v_out, nq_out, nk_out, nv_out, h_scr, proj_scr):
    j = pl.program_id(1)

    @pl.when(j == 0)
    def _():
        h_scr[...] = jnp.zeros(h_scr.shape, h_scr.dtype)
        proj_scr[...] = jnp.zeros(proj_scr.shape, proj_scr.dtype)

    proj = proj_scr
    c, sa, sb = c_ref[...], sa_ref[...], sb_ref[...]

    qn = _rms(proj[:, C_QLAT:C_NAQ], qg_ref[...]).astype(BF16)
    q = jnp.dot(qn, wuq_ref[...], preferred_element_type=F32) * MLA_SCALE
    for h in range(MLA_HEADS):
        q_out[0, h] = _rope(q[:, h * LANES:(h + 1) * LANES], c, sa, sb).astype(BF16)

    kvn = _rms(proj[:, C_KVLAT:C_KROPE], kvg_ref[...]).astype(BF16)
    kv = jnp.dot(kvn, wukv_ref[...], preferred_element_type=F32)
    kr = _rope(proj[:, C_KROPE:C_NAK], c, sa, sb)
    v0 = MLA_HEADS * LANES
    lane = lax.broadcasted_iota(jnp.int32, (TOK_BLOCK, LANES), 1)
    for h in range(MLA_HEADS):
        k_out[0, h] = (kv[:, h * LANES:(h + 1) * LANES] + kr).astype(BF16)
        v = jnp.where(lane == ONES_LANE, 1.0, kv[:, v0 + h * LANES:v0 + (h + 1) * LANES])
        v_out[0, h] = jnp.transpose(v).astype(BF16)
    for p in range(HEAD_PAIRS):
        nq_out[0, p] = (proj[:, C_NAQ + p * LANES:C_NAQ + (p + 1) * LANES] * NA_SCALE).astype(BF16)
        nk_out[0, p] = proj[:, C_NAK + p * LANES:C_NAK + (p + 1) * LANES].astype(BF16)
        nv_pair = proj[:, C_NAV + p * LANES:C_NAV + (p + 1) * LANES]
        for half in range(2):
            nv = pltpu.roll(nv_pair, NA_DIM, 1) if half else nv_pair
            nv = jnp.where(lane < NA_DIM, nv, jnp.where(lane == ONES_LANE, 1.0, 0.0))
            nv_out[0, 2 * p + half] = jnp.transpose(nv).astype(BF16)

    proj_scr[...] = jnp.dot(h_scr[...], win_ref[...], preferred_element_type=F32)
    xt = jnp.where(j == 0, ctx_ref[0], x_ref[0])
    h_scr[...] = (_rms(xt, g_ref[...]) * (1.0 + mod_ref[0, 1:2, :]) + mod_ref[0, 0:1, :]).astype(BF16)


def _proj(x, ctx, mod, g, win, qg, kvg, wuq, wukv, rc, rsa, rsb):
    B, N, D = x.shape
    nb = N // TOK_BLOCK
    ntot = N + CTX_LEN
    full = lambda shape: pl.BlockSpec(shape, lambda b, j: (0,) * len(shape))
    depth = 2
    tile_in = lambda j: jnp.minimum(j, nb)
    tile_out = lambda j: jnp.clip(j - depth, 0, nb)
    lat = lambda b, j: (b, 0, jnp.maximum(tile_out(j) - 1, 0), 0)
    alltok = lambda b, j: (b, 0, tile_out(j), 0)
    feat = lambda b, j: (b, 0, 0, tile_out(j))
    rope = lambda b, j: (tile_out(j), 0)
    return pl.pallas_call(
        _proj_kernel,
        grid=(B, nb + 1 + depth),
        in_specs=[pl.BlockSpec((1, TOK_BLOCK, D), lambda b, j: (b, jnp.maximum(tile_in(j) - 1, 0), 0)),
                  pl.BlockSpec((1, CTX_LEN, D), lambda b, j: (b, 0, 0)),
                  pl.BlockSpec((1, 6, D), lambda b, j: (jnp.where(j == 0, B, b), 0, 0)),
                  full((1, D)), full(win.shape), full((1, Q_LORA)), full((1, KV_LORA)),
                  full(wuq.shape), full(wukv.shape),
                  pl.BlockSpec((TOK_BLOCK, LANES), rope),
                  pl.BlockSpec((TOK_BLOCK, LANES), rope),
                  pl.BlockSpec((TOK_BLOCK, LANES), rope)],
        out_specs=[pl.BlockSpec((1, MLA_HEADS, TOK_BLOCK, LANES), lat),
                   pl.BlockSpec((1, MLA_HEADS, TOK_BLOCK, LANES), alltok),
                   pl.BlockSpec((1, MLA_HEADS, LANES, TOK_BLOCK), feat),
                   pl.BlockSpec((1, HEAD_PAIRS, TOK_BLOCK, LANES), lat),
                   pl.BlockSpec((1, HEAD_PAIRS, TOK_BLOCK, LANES), alltok),
                   pl.BlockSpec((1, NA_HEADS, LANES, TOK_BLOCK), feat)],
        out_shape=[jax.ShapeDtypeStruct((B, MLA_HEADS, N, LANES), BF16),
                   jax.ShapeDtypeStruct((B, MLA_HEADS, ntot, LANES), BF16),
                   jax.ShapeDtypeStruct((B, MLA_HEADS, LANES, ntot), BF16),
                   jax.ShapeDtypeStruct((B, HEAD_PAIRS, N, LANES), BF16),
                   jax.ShapeDtypeStruct((B, HEAD_PAIRS, ntot, LANES), BF16),
                   jax.ShapeDtypeStruct((B, NA_HEADS, LANES, ntot), BF16)],
        scratch_shapes=[pltpu.VMEM((TOK_BLOCK, D), BF16), pltpu.VMEM((TOK_BLOCK, C_END), F32)],
        compiler_params=pltpu.CompilerParams(vmem_limit_bytes=VMEM_LIMIT),
        name="proj",
    )(x, ctx, mod, g, win, qg, kvg, wuq, wukv, rc, rsa, rsb)


MLA_TQ = 1024
MLA_TK = 768


def _mla_kernel(q_ref, k_ref, v_ref, wg_ref, wu_ref, wd_ref, o_ref, wg_out, wu_out, wd_out,
                m_scr, alpha_scr, acc_scr, p_scr, s_scr, smax_scr):
    wg_out[...] = wg_ref[...].astype(BF16)
    wu_out[...] = wu_ref[...].astype(BF16)
    wd_out[...] = wd_ref[...].astype(BF16)

    n_chunks = k_ref.shape[2] // MLA_TK
    m_scr[...] = jnp.full(m_scr.shape, -jnp.inf, F32)
    acc_scr[...] = jnp.zeros(acc_scr.shape, F32)

    def scores(ci, hh):
        off = pl.multiple_of(ci * MLA_TK, MLA_TK)
        k = k_ref[0, hh, pl.ds(off, MLA_TK), :]
        s = lax.dot_general(k, q_ref[0, hh], (((1,), (1,)), ((), ())), preferred_element_type=F32)
        s_scr[hh] = s
        smax_scr[hh] = jnp.max(s, axis=0, keepdims=True)

    def softmax(hh):
        m_prev = m_scr[hh]
        m_next = jnp.maximum(m_prev, smax_scr[hh])
        p_scr[hh] = jnp.exp2((s_scr[hh] - m_next).astype(BF16))
        alpha_scr[hh] = jnp.exp2(m_prev - m_next)
        m_scr[hh] = m_next

    def values(ci, hh):
        off = pl.multiple_of(ci * MLA_TK, MLA_TK)
        vt = v_ref[0, hh, :, pl.ds(off, MLA_TK)]
        acc_scr[hh] = alpha_scr[hh] * acc_scr[hh] + jnp.dot(vt, p_scr[hh], preferred_element_type=F32)

    for hh in range(2):
        scores(0, hh)
    for hh in range(2):
        softmax(hh)
        scores(1, hh)

    def body(ci, carry):
        for hh in range(2):
            values(ci - 1, hh)
            softmax(hh)
            scores(ci + 1, hh)
        return carry

    lax.fori_loop(1, n_chunks - 1, body, 0)
    for hh in range(2):
        values(n_chunks - 2, hh)
        softmax(hh)
    for hh in range(2):
        values(n_chunks - 1, hh)
    outs = []
    for hh in range(2):
        acc = acc_scr[hh]
        outs.append(jnp.transpose(acc / acc[ONES_LANE:ONES_LANE + 1, :]))
    lane = lax.broadcasted_iota(jnp.int32, (MLA_TQ, LANES), 1)
    o_ref[0] = jnp.where(lane < MLA_V, outs[0], pltpu.roll(outs[1], MLA_V, 1)).astype(o_ref.dtype)


def _mla(q_m, k_m, v_m, w_gate, w_up, w_down):
    B, _, N, _ = q_m.shape
    ntot = k_m.shape[2]
    n_q = N // MLA_TQ
    steps = B * HEAD_PAIRS * n_q
    flat = [w.reshape(-1, w.shape[-1]) for w in (w_gate, w_up, w_down)]
    wblk = [(w.shape[0] // steps, w.shape[1]) for w in flat]
    step = lambda b, p, i: ((b * HEAD_PAIRS + p) * n_q + i, 0)
    outs = pl.pallas_call(
        _mla_kernel,
        grid=(B, HEAD_PAIRS, n_q),
        in_specs=[pl.BlockSpec((1, 2, MLA_TQ, LANES), lambda b, p, i: (b, p, i, 0)),
                  pl.BlockSpec((1, 2, ntot, LANES), lambda b, p, i: (b, p, 0, 0)),
                  pl.BlockSpec((1, 2, LANES, ntot), lambda b, p, i: (b, p, 0, 0))]
        + [pl.BlockSpec(blk, step) for blk in wblk],
        out_specs=[pl.BlockSpec((1, MLA_TQ, LANES), lambda b, p, i: (b, i, p))]
        + [pl.BlockSpec(blk, step) for blk in wblk],
        out_shape=[jax.ShapeDtypeStruct((B, N, MLA_HEADS * MLA_V), BF16)]
        + [jax.ShapeDtypeStruct(w.shape, BF16) for w in flat],
        scratch_shapes=[pltpu.VMEM((2, 1, MLA_TQ), F32), pltpu.VMEM((2, 1, MLA_TQ), F32),
                        pltpu.VMEM((2, LANES, MLA_TQ), F32), pltpu.VMEM((2, MLA_TK, MLA_TQ), BF16),
                        pltpu.VMEM((2, MLA_TK, MLA_TQ), F32), pltpu.VMEM((2, 1, MLA_TQ), F32)],
        compiler_params=pltpu.CompilerParams(vmem_limit_bytes=VMEM_LIMIT),
        name="mla",
    )(q_m, k_m, v_m, *flat)
    return (outs[0],) + tuple(o.reshape(w.shape) for o, w in zip(outs[1:], (w_gate, w_up, w_down)))


NA_UNIT_HEADS = 8
NA_UNITS_PER_BLOCK = NA_HEADS // NA_UNIT_HEADS
NA_KEYS = 4 * TOK_BLOCK


def _na_kernel(q_ref, kc_ref, kp_ref, k0_ref, kn_ref, bp_ref, b0_ref, bn_ref,
               vc_ref, vp_ref, v0_ref, vn_ref, o_ref, s_scr, smax_scr, p_scr):
    @pl.when(pl.program_id(1) == 0)
    def _():
        s_scr[...] = jnp.zeros(s_scr.shape, F32)
        smax_scr[...] = jnp.zeros(smax_scr.shape, F32)
        p_scr[...] = jnp.zeros(p_scr.shape, BF16)

    lane = lax.broadcasted_iota(jnp.int32, (TOK_BLOCK, LANES), 1)

    for pp in range(NA_UNIT_HEADS // 2):
        outs = []
        for half in range(2):
            hq = 2 * pp + half
            vt = jnp.concatenate([vc_ref[0, hq], vp_ref[0, hq], v0_ref[0, hq], vn_ref[0, hq]], axis=1)
            o = jnp.dot(vt, p_scr[hq], preferred_element_type=F32)
            outs.append(jnp.transpose(o / o[ONES_LANE:ONES_LANE + 1, :]))
        o_ref[0, :, pp * LANES:(pp + 1) * LANES] = jnp.where(
            lane < NA_DIM, outs[0], pltpu.roll(outs[1], NA_DIM, 1)).astype(o_ref.dtype)

    for hq in range(NA_UNIT_HEADS):
        p_scr[hq] = jnp.exp2((s_scr[hq] - smax_scr[hq]).astype(BF16))

    dn = (((1,), (1,)), ((), ()))
    for pp in range(NA_UNIT_HEADS // 2):
        keys = jnp.concatenate([kc_ref[0, pp], kp_ref[0, pp], k0_ref[0, pp], kn_ref[0, pp]], axis=0)
        for half in range(2):
            hq = 2 * pp + half
            in_half = (lane >= NA_DIM) if half else (lane < NA_DIM)
            q = jnp.where(in_half, q_ref[0, pp], jnp.zeros((), BF16))
            s = lax.dot_general(keys, q, dn, preferred_element_type=F32)
            parts = [s[:TOK_BLOCK],
                     s[TOK_BLOCK:2 * TOK_BLOCK] + bp_ref[0, hq],
                     s[2 * TOK_BLOCK:3 * TOK_BLOCK] + b0_ref[0, hq],
                     s[3 * TOK_BLOCK:] + bn_ref[0, hq]]
            for j, part in enumerate(parts):
                s_scr[hq, j * TOK_BLOCK:(j + 1) * TOK_BLOCK, :] = part
            smax_scr[hq] = jnp.max(jnp.maximum(jnp.maximum(parts[0], parts[1]),
                                               jnp.maximum(parts[2], parts[3])), axis=0, keepdims=True)


def _natten(nq, nk, nv, bias):
    B, _, N, _ = nq.shape
    nb = N // TOK_BLOCK
    last = nb - 1
    n_units = nb * NA_UNITS_PER_BLOCK
    pipeline_depth = 2

    def unit(t, lag):
        u = jnp.clip(t - lag, 0, n_units - 1)
        return u // NA_UNITS_PER_BLOCK, u % NA_UNITS_PER_BLOCK

    tok_blocks = (lambda i: 0, lambda i: jnp.maximum(i - 1, 0) + 1, lambda i: i + 1,
                  lambda i: jnp.minimum(i + 1, last) + 1)
    variants = (lambda i: jnp.where(i == 0, V_NONE, jnp.where(i == last, V_PREV_ALL, V_PREV)),
                lambda i: jnp.where((i == 0) | (i == last), V_CUR_ALL, V_CUR),
                lambda i: jnp.where(i == 0, V_NEXT_ALL, jnp.where(i == last, V_NONE, V_NEXT)))

    def q_map(b, t):
        i, g = unit(t, 0)
        return b, g, i, 0

    def key_map(tok):
        def f(b, t):
            i, g = unit(t, 0)
            return b, g, tok(i), 0
        return f

    def bias_map(var):
        def f(b, t):
            i, g = unit(t, 0)
            return var(i), g, 0, 0
        return f

    def value_map(tok):
        def f(b, t):
            i, g = unit(t, pipeline_depth)
            return b, g, 0, tok(i)
        return f

    def out_map(b, t):
        i, g = unit(t, pipeline_depth)
        return b, i, g

    pair_blk = (1, NA_UNIT_HEADS // 2, TOK_BLOCK, LANES)
    return pl.pallas_call(
        _na_kernel,
        grid=(B, n_units + pipeline_depth),
        in_specs=([pl.BlockSpec(pair_blk, q_map)]
                  + [pl.BlockSpec(pair_blk, key_map(tok)) for tok in tok_blocks]
                  + [pl.BlockSpec((1, NA_UNIT_HEADS, TOK_BLOCK, TOK_BLOCK), bias_map(var)) for var in variants]
                  + [pl.BlockSpec((1, NA_UNIT_HEADS, LANES, TOK_BLOCK), value_map(tok)) for tok in tok_blocks]),
        out_specs=pl.BlockSpec((1, TOK_BLOCK, NA_UNIT_HEADS * NA_DIM), out_map),
        out_shape=jax.ShapeDtypeStruct((B, N, NA_HEADS * NA_DIM), BF16),
        scratch_shapes=[pltpu.VMEM((NA_UNIT_HEADS, NA_KEYS, TOK_BLOCK), F32),
                        pltpu.VMEM((NA_UNIT_HEADS, 1, TOK_BLOCK), F32),
                        pltpu.VMEM((NA_UNIT_HEADS, NA_KEYS, TOK_BLOCK), BF16)],
        compiler_params=pltpu.CompilerParams(vmem_limit_bytes=VMEM_LIMIT),
        name="natten",
    )(nq, nk, nk, nk, nk, bias, bias, bias, nv, nv, nv, nv)


OUT_TM = 512
R_G1, R_SH2, R_SC2, R_G2 = 2, 3, 4, 5
GROUP_LANE = N_EXPERTS


def _first_max_lane(x, valid, lane):
    xm = jnp.where(valid, x, -jnp.inf)
    mx = jnp.max(xm, axis=1, keepdims=True)
    idx = jnp.min(jnp.where(valid & (xm == mx), lane, LANES), axis=1, keepdims=True)
    return mx, idx


def _outproj_kernel(om_ref, on_ref, x_ref, mod_ref, wo_ref, g_ref, wr_ref, br_ref,
                    x1_out, h2_out, comb_out, h2_scr):
    @pl.when(pl.program_id(0) == 0)
    def _():
        h2_scr[...] = jnp.zeros(h2_scr.shape, h2_scr.dtype)

    h2_prev = h2_scr[...]
    h2_hi = h2_prev.astype(BF16)
    h2_lo = (h2_prev - h2_hi.astype(F32)).astype(BF16)
    both = jnp.dot(h2_hi, wr_ref[...], preferred_element_type=F32)
    logits = (both[:, :LANES] + both[:, LANES:]
              + jnp.dot(h2_lo, wr_ref[:, :LANES], preferred_element_type=F32) + br_ref[...])
    lane = lax.broadcasted_iota(jnp.int32, logits.shape, 1)
    is_g = lane < N_GROUPS
    g_max, g_sel = _first_max_lane(logits, is_g, lane)
    g_w = 1.0 / jnp.sum(jnp.where(is_g, jnp.exp(logits - g_max), 0.0), axis=1, keepdims=True)
    e_lo = N_GROUPS + g_sel * EXPERTS_PER_GROUP
    in_grp = (lane >= e_lo) & (lane < e_lo + EXPERTS_PER_GROUP)
    m1, i1 = _first_max_lane(logits, in_grp, lane)
    m2, i2 = _first_max_lane(logits, in_grp & (lane != i1), lane)
    e2 = jnp.exp(m2 - m1)
    w1 = 1.0 / (1.0 + e2)
    w2 = e2 / (1.0 + e2)
    comb = jnp.where(lane == i1, g_w * w1, jnp.where(lane == i2, g_w * w2, 0.0))
    comb_out[...] = jnp.where(lane == GROUP_LANE, g_sel.astype(F32), pltpu.roll(comb, LANES - N_GROUPS, 1))

    half = wo_ref.shape[0] // 2
    mixed = (jnp.dot(om_ref[...], wo_ref[:half, :], preferred_element_type=F32)
             + jnp.dot(on_ref[...], wo_ref[half:, :], preferred_element_type=F32))
    x1 = x_ref[...] + mod_ref[0, R_G1:R_G1 + 1, :] * mixed
    x1_out[...] = x1
    h2 = _rms(x1, g_ref[...]) * (1.0 + mod_ref[0, R_SC2:R_SC2 + 1, :]) + mod_ref[0, R_SH2:R_SH2 + 1, :]
    h2_out[...] = h2.astype(BF16)
    h2_scr[...] = h2


def _outproj(o_mla, o_na, x2d, mod, wo, g, wr, br, n_per_batch):
    T, D = x2d.shape
    per_b = n_per_batch // OUT_TM
    nt = T // OUT_TM
    cur = lambda t: jnp.minimum(t, nt - 1)
    tok = lambda w: pl.BlockSpec((OUT_TM, w), lambda t: (cur(t), 0))
    full = lambda shape: pl.BlockSpec(shape, lambda t: (0,) * len(shape))
    return pl.pallas_call(
        _outproj_kernel,
        grid=(nt + 1,),
        in_specs=[tok(o_mla.shape[1]), tok(o_na.shape[1]), tok(D),
                  pl.BlockSpec((1, 6, D), lambda t: (cur(t) // per_b, 0, 0)),
                  full(wo.shape), full((1, D)), full(wr.shape), full(br.shape)],
        out_specs=[tok(D), tok(D), pl.BlockSpec((OUT_TM, LANES), lambda t: (jnp.maximum(t - 1, 0), 0))],
        out_shape=[jax.ShapeDtypeStruct((T, D), F32), jax.ShapeDtypeStruct((T, D), BF16),
                   jax.ShapeDtypeStruct((T, LANES), F32)],
        scratch_shapes=[pltpu.VMEM((OUT_TM, D), F32)],
        compiler_params=pltpu.CompilerParams(vmem_limit_bytes=VMEM_LIMIT),
        name="outproj",
    )(o_mla, o_na, x2d, mod, wo, g, wr, br)


MOE_TT = 1024
MOE_CH = 128
MOE_NBLK = MOE_TT // MOE_CH + (N_GROUPS - 1)
MOE_TTP = MOE_NBLK * MOE_CH


def _moe_sort_kernel(h_ref, comb_ref, xs_out, cs_out, pos_out, cnt_out, tri_scr):
    @pl.when(pl.program_id(0) == 0)
    def _():
        r = lax.broadcasted_iota(jnp.int32, tri_scr.shape, 0)
        c = lax.broadcasted_iota(jnp.int32, tri_scr.shape, 1)
        tri_scr[...] = jnp.where(c < r, 1.0, 0.0).astype(BF16)

    comb = comb_ref[...]
    lane = lax.broadcasted_iota(jnp.int32, comb.shape, 1)
    lane_f = lane.astype(F32)
    gid = jnp.sum(jnp.where(lane == GROUP_LANE, comb, 0.0), axis=1, keepdims=True)
    onehot = jnp.where((lane_f == gid) & (lane < N_GROUPS), 1.0, 0.0)
    ahead = jnp.dot(tri_scr[...], onehot.astype(BF16), preferred_element_type=F32)
    n = jnp.sum(onehot, axis=0, keepdims=True)
    padded = jnp.ceil(n * (1.0 / MOE_CH)) * MOE_CH
    start = jnp.zeros_like(padded)
    for k in range(1, N_GROUPS):
        start = start + jnp.where(lane[:1] >= k, pltpu.roll(padded, k, 1), 0.0)
    pos = jnp.sum(onehot * (ahead + start), axis=1, keepdims=True)
    pos_rep = jnp.broadcast_to(pos, comb.shape)
    pos_out[...] = pos_rep
    cnt_out[0] = jnp.broadcast_to(n, (SUBLANES, LANES))

    pos_row = jnp.transpose(pos_rep)[0:1, :]
    row = lax.broadcasted_iota(jnp.int32, (MOE_TTP, MOE_TT), 0).astype(F32)
    perm = jnp.where(row == pos_row, 1.0, 0.0).astype(BF16)
    xs_out[...] = jnp.dot(perm, h_ref[...], preferred_element_type=F32).astype(BF16)
    hi = comb.astype(BF16)
    lo = (comb - hi.astype(F32)).astype(BF16)
    both = jnp.dot(perm, jnp.concatenate([hi, lo], axis=1), preferred_element_type=F32)
    cs_out[...] = both[:, :LANES] + both[:, LANES:]


def _moe_sort(h2, comb):
    T, D = h2.shape
    nt = T // MOE_TT
    return pl.pallas_call(
        _moe_sort_kernel,
        grid=(nt,),
        in_specs=[pl.BlockSpec((MOE_TT, D), lambda t: (t, 0)),
                  pl.BlockSpec((MOE_TT, LANES), lambda t: (t, 0))],
        out_specs=[pl.BlockSpec((MOE_TTP, D), lambda t: (t, 0)),
                   pl.BlockSpec((MOE_TTP, LANES), lambda t: (t, 0)),
                   pl.BlockSpec((MOE_TT, LANES), lambda t: (t, 0)),
                   pl.BlockSpec((1, SUBLANES, LANES), lambda t: (t, 0, 0))],
        out_shape=[jax.ShapeDtypeStruct((nt * MOE_TTP, D), BF16),
                   jax.ShapeDtypeStruct((nt * MOE_TTP, LANES), F32),
                   jax.ShapeDtypeStruct((T, LANES), F32),
                   jax.ShapeDtypeStruct((nt, SUBLANES, LANES), F32)],
        scratch_shapes=[pltpu.VMEM((MOE_TT, MOE_TT), BF16)],
        compiler_params=pltpu.CompilerParams(vmem_limit_bytes=VMEM_LIMIT),
        name="moe_sort",
    )(h2, comb)


def _moe_experts_kernel(blk_ref, grp_ref, nvalid_ref, x_ref, c_ref, wg_ref, wu_ref, wd_ref, y_ref, hid_scr):
    s = pl.program_id(0)
    n_slots = pl.num_programs(0) - 1

    @pl.when(s == 0)
    def _():
        hid_scr[...] = jnp.zeros(hid_scr.shape, hid_scr.dtype)

    @pl.when(s > nvalid_ref[0])
    def _():
        y_ref[...] = jnp.zeros(y_ref.shape, y_ref.dtype)

    @pl.when(s <= nvalid_ref[0])
    def _():
        y = None
        for e in range(EXPERTS_PER_GROUP):
            part = jnp.dot(hid_scr[e], wd_ref[e], preferred_element_type=F32)
            y = part if y is None else y + part
        y_ref[...] = y.astype(y_ref.dtype)

        x = x_ref[...]
        comb = c_ref[...]
        lane = lax.broadcasted_iota(jnp.int32, comb.shape, 1)
        e0 = grp_ref[jnp.minimum(s, n_slots - 1)] * EXPERTS_PER_GROUP
        for e in range(EXPERTS_PER_GROUP):
            gate = jnp.dot(x, wg_ref[e], preferred_element_type=F32)
            up = jnp.dot(x, wu_ref[e], preferred_element_type=F32)
            w = jnp.sum(jnp.where(lane == e0 + e, comb, 0.0), axis=1, keepdims=True)
            hid_scr[e] = (gate * jax.nn.sigmoid(gate) * up * w).astype(BF16)


def _moe_experts(blk, grp, nvalid, xs, cs, wg, wu, wd):
    D = xs.shape[1]
    n_slots = blk.shape[0]

    def cur(arr):
        return lambda s, blk, grp, nvalid: (arr(blk, grp)[jnp.minimum(s, n_slots - 1)],)

    def prev(arr):
        return lambda s, blk, grp, nvalid: (arr(blk, grp)[jnp.maximum(s - 1, 0)],)

    blk_of = lambda blk, grp: blk
    grp_of = lambda blk, grp: grp
    rows = lambda f: (lambda *a: f(*a) + (0,))
    wts = lambda f: (lambda *a: f(*a) + (0, 0))
    return pl.pallas_call(
        _moe_experts_kernel,
        grid_spec=pltpu.PrefetchScalarGridSpec(
            num_scalar_prefetch=3,
            grid=(n_slots + 1,),
            in_specs=[pl.BlockSpec((MOE_CH, D), rows(cur(blk_of))),
                      pl.BlockSpec((MOE_CH, LANES), rows(cur(blk_of))),
                      pl.BlockSpec((EXPERTS_PER_GROUP, D, D_EXPERT), wts(cur(grp_of))),
                      pl.BlockSpec((EXPERTS_PER_GROUP, D, D_EXPERT), wts(cur(grp_of))),
                      pl.BlockSpec((EXPERTS_PER_GROUP, D_EXPERT, D), wts(prev(grp_of)))],
            out_specs=pl.BlockSpec((MOE_CH, D), rows(prev(blk_of))),
            scratch_shapes=[pltpu.VMEM((EXPERTS_PER_GROUP, MOE_CH, D_EXPERT), BF16)],
        ),
        out_shape=jax.ShapeDtypeStruct(xs.shape, BF16),
        compiler_params=pltpu.CompilerParams(vmem_limit_bytes=VMEM_LIMIT),
        name="moe_experts",
    )(blk, grp, nvalid, xs, cs, wg, wu, wd)


def _moe_final_kernel(ys_ref, pos_ref, x1_ref, mod_ref, fg_ref, o_ref):
    col = lax.broadcasted_iota(jnp.int32, (MOE_TT, MOE_TTP), 1).astype(F32)
    unperm = jnp.where(col == pltpu.repeat(pos_ref[...], MOE_NBLK, axis=1), 1.0, 0.0).astype(BF16)
    y = jnp.dot(unperm, ys_ref[...], preferred_element_type=F32)
    o_ref[...] = _rms(x1_ref[...] + mod_ref[0, R_G2:R_G2 + 1, :] * y, fg_ref[...])


def _moe_final(ys, pos, x1, mod, fg, n_per_batch):
    T, D = x1.shape
    per_b = n_per_batch // MOE_TT
    return pl.pallas_call(
        _moe_final_kernel,
        grid=(T // MOE_TT,),
        in_specs=[pl.BlockSpec((MOE_TTP, D), lambda t: (t, 0)),
                  pl.BlockSpec((MOE_TT, LANES), lambda t: (t, 0)),
                  pl.BlockSpec((MOE_TT, D), lambda t: (t, 0)),
                  pl.BlockSpec((1, 6, D), lambda t: (t // per_b, 0, 0)),
                  pl.BlockSpec((1, D), lambda t: (0, 0))],
        out_specs=pl.BlockSpec((MOE_TT, D), lambda t: (t, 0)),
        out_shape=jax.ShapeDtypeStruct((T, D), F32),
        compiler_params=pltpu.CompilerParams(vmem_limit_bytes=VMEM_LIMIT),
        name="moe_final",
    )(ys, pos, x1, mod, fg)


def _chunk_schedule(counts):
    nt = counts.shape[0]
    nchunks = jnp.ceil(counts * (1.0 / MOE_CH)).astype(jnp.int32)
    ends = jnp.cumsum(nchunks, axis=1)
    j = jnp.arange(MOE_NBLK, dtype=jnp.int32)
    grp = jnp.sum(j[None, :, None] >= ends[:, None, :], axis=2).astype(jnp.int32).reshape(-1)
    blk = (jnp.arange(nt, dtype=jnp.int32)[:, None] * MOE_NBLK + j[None, :]).reshape(-1)
    order = jnp.argsort(grp, stable=True)
    nvalid = jnp.sum(grp < N_GROUPS).astype(jnp.int32).reshape(1)
    return blk[order], jnp.minimum(grp, N_GROUPS - 1)[order], nvalid


def _rope_tables(n):
    pos = np.arange(n)
    half = MLA_ROPE // 2
    inv_freq = (ROPE_THETA ** (-np.arange(0, half, 2, dtype=np.float32) / half)).astype(np.float32)
    ang = np.concatenate([(pos // GRID_W).astype(np.float32)[:, None] * inv_freq,
                          (pos % GRID_W).astype(np.float32)[:, None] * inv_freq], axis=-1)
    cos, sin = np.cos(ang).astype(np.float32), np.sin(ang).astype(np.float32)
    one = np.ones((n, MLA_NOPE), np.float32)
    zero = np.zeros((n, MLA_NOPE), np.float32)
    pad = np.zeros((n, LANES - MLA_QK), np.float32)
    z16 = np.zeros((n, half), np.float32)
    c = np.concatenate([one, cos, cos, pad], axis=1)
    sa = np.concatenate([zero, -sin, z16, pad], axis=1)
    sb = np.concatenate([zero, z16, sin, pad], axis=1)
    ident = np.concatenate([np.ones((CTX_LEN, MLA_QK), np.float32),
                            np.zeros((CTX_LEN, LANES - MLA_QK), np.float32)], axis=1)
    zc = np.zeros((CTX_LEN, LANES), np.float32)
    return (np.concatenate([ident, c], axis=0), np.concatenate([zc, sa], axis=0),
            np.concatenate([zc, sb], axis=0))


def kernel(x, c, ctx, c_ctx, w_mod, b_mod, norm_attn_g, norm_ffn_g, w_in, q_a_norm_g, kv_a_norm_g, w_uq, w_ukv,
           na_rel_bias, w_out, w_router_group, b_router_group, w_router_expert, b_router_expert, w_gate, w_up,
           w_down, final_norm_g):
    B, N, D = x.shape
    assert w_mod.shape[0] == 1, "single-layer kernel"
    assert (B, N, D) == (2, 8192, D_MODEL) and ctx.shape == (B, CTX_LEN, D)

    cond_rows = jnp.concatenate([c, c_ctx[None, :], jnp.zeros((SUBLANES - B - 1, D), F32)], axis=0)
    mod = _adaln(cond_rows, w_mod[0], b_mod[0]).reshape(SUBLANES, 6, D)

    wi = w_in[0]
    na_w = NA_HEADS * NA_DIM
    k0 = Q_LORA + na_w + KV_LORA
    win = jnp.concatenate([wi[:, :k0], jnp.zeros((D, MLA_NOPE), F32), wi[:, k0:k0 + MLA_ROPE],
                           jnp.zeros((D, LANES - MLA_QK), F32), wi[:, k0 + MLA_ROPE:]], axis=1).astype(BF16)
    assert win.shape[1] == C_END
    wuq = jnp.pad(w_uq[0].reshape(Q_LORA, MLA_HEADS, MLA_QK), ((0, 0), (0, 0), (0, LANES - MLA_QK)))
    wuq = wuq.reshape(Q_LORA, MLA_HEADS * LANES).astype(BF16)
    wkv = w_ukv[0].reshape(KV_LORA, MLA_HEADS, MLA_NOPE + MLA_V)
    wk = jnp.pad(wkv[:, :, :MLA_NOPE], ((0, 0), (0, 0), (0, LANES - MLA_NOPE))).reshape(KV_LORA, MLA_HEADS * LANES)
    wv = jnp.pad(wkv[:, :, MLA_NOPE:], ((0, 0), (0, 0), (0, LANES - MLA_V))).reshape(KV_LORA, MLA_HEADS * LANES)
    wukv = jnp.concatenate([wk, wv], axis=1).astype(BF16)
    rc, rsa, rsb = _rope_tables(N)

    q_m, k_m, v_m, nq, nk, nv = _proj(x, ctx, mod, norm_attn_g, win, q_a_norm_g, kv_a_norm_g, wuq, wukv,
                                      rc, rsa, rsb)
    o_mla, wg_bf, wu_bf, wd_bf = _mla(q_m, k_m, v_m, w_gate[0], w_up[0], w_down[0])
    o_na = _natten(nq, nk, nv, _na_bias(na_rel_bias[0]))

    n_r = N_GROUPS + N_EXPERTS
    wr = jnp.pad(jnp.concatenate([w_router_group[0], w_router_expert[0]], axis=1), ((0, 0), (0, LANES - n_r)))
    wr_hi = lax.bitcast_convert_type(lax.bitcast_convert_type(wr, jnp.uint32) & jnp.uint32(0xFFFF0000), F32)
    wr = jnp.concatenate([wr_hi.astype(BF16), (wr - wr_hi).astype(BF16)], axis=1)
    br = jnp.pad(jnp.concatenate([b_router_group[0], b_router_expert[0]]), (0, LANES - n_r)).reshape(1, LANES)
    T = B * N
    x1, h2, comb = _outproj(o_mla.reshape(T, -1), o_na.reshape(T, -1), x.reshape(T, D), mod,
                            w_out[0].astype(BF16), norm_ffn_g, wr, br, N)
    xs, cs, pos, counts = _moe_sort(h2, comb)
    blk, grp, nvalid = _chunk_schedule(counts[:, 0, :N_GROUPS])
    ys = _moe_experts(blk, grp, nvalid, xs, cs, wg_bf, wu_bf, wd_bf)
    out = _moe_final(ys, pos, x1, mod, final_norm_g.reshape(1, D), N)
    return out.reshape(B, N, D)
```

```python
import jax
import jax.numpy as jnp
import numpy as np
from jax import lax
from jax.experimental import pallas as pl
from jax.experimental.pallas import tpu as pltpu

D_MODEL = 1024
GRID_W = 64
CTX_LEN = 256
MLA_HEADS = 8
MLA_NOPE = 64
MLA_ROPE = 32
MLA_QK = MLA_NOPE + MLA_ROPE
MLA_V = 64
Q_LORA = 384
KV_LORA = 256
NA_HEADS = 8
NA_DIM = 64
NA_WIN_H = 8
NA_WIN_W = 16
N_GROUPS = 4
EXPERTS_PER_GROUP = 4
N_EXPERTS = N_GROUPS * EXPERTS_PER_GROUP
D_EXPERT = 512
ROPE_THETA = 10000.0
NORM_EPS = 1e-6
MASK_VALUE = -1e30
LOG2E = 1.4426950408889634

LANES = 128
SUBLANES = 8
HEAD_PAIRS = 4
TOK_BLOCK = 256
ROWS_PER_BLOCK = TOK_BLOCK // GRID_W
VMEM_LIMIT = 48 * 1024 * 1024

F32 = jnp.float32
BF16 = jnp.bfloat16


def _rms(x, g):
    return x * lax.rsqrt(jnp.mean(x * x, axis=-1, keepdims=True) + NORM_EPS) * g


def _split_bf16(x):
    hi = x.astype(BF16)
    return hi, (x - hi.astype(F32)).astype(BF16)


def _adaln_kernel(c_ref, w_ref, b_ref, o_ref):
    c = c_ref[...]
    s_hi, s_lo = _split_bf16(c * jax.nn.sigmoid(c))
    w_hi, w_lo = _split_bf16(w_ref[...])
    s_both = jnp.concatenate([s_hi, s_lo], axis=0)
    by_hi = jnp.dot(s_both, w_hi, preferred_element_type=F32)
    by_lo = jnp.dot(s_both, w_lo, preferred_element_type=F32)
    o_ref[...] = by_hi[:SUBLANES] + by_hi[SUBLANES:] + by_lo[:SUBLANES] + b_ref[...]


def _adaln(cond_rows, w_mod, b_mod):
    n_out = w_mod.shape[1]
    bn = D_MODEL
    return pl.pallas_call(
        _adaln_kernel,
        grid=(n_out // bn,),
        in_specs=[pl.BlockSpec((SUBLANES, D_MODEL), lambda n: (0, 0)),
                  pl.BlockSpec((D_MODEL, bn), lambda n: (0, n)),
                  pl.BlockSpec((1, bn), lambda n: (0, n))],
        out_specs=pl.BlockSpec((SUBLANES, bn), lambda n: (0, n)),
        out_shape=jax.ShapeDtypeStruct((SUBLANES, n_out), F32),
        name="adaln",
    )(cond_rows, w_mod, b_mod.reshape(1, n_out))


NA_VARIANTS = ((-1, False), (0, False), (1, False), (0, True), (1, True), (-1, True), (0, None))
V_PREV, V_CUR, V_NEXT, V_CUR_ALL, V_NEXT_ALL, V_PREV_ALL, V_NONE = range(7)
N_DR = 2 * NA_WIN_H - 1
N_DC = 2 * NA_WIN_W - 1


def _na_bias_kernel(rel_ref, o_ref):
    h = pl.program_id(0)
    kc = lax.broadcasted_iota(jnp.int32, (GRID_W, LANES), 0)
    qc = lax.broadcasted_iota(jnp.int32, (GRID_W, LANES), 1) % GRID_W
    col_start = jnp.clip(qc - NA_WIN_W // 2, 0, GRID_W - NA_WIN_W)
    col_in = (kc >= col_start) & (kc < col_start + NA_WIN_W)
    dc = kc - qc + (NA_WIN_W - 1)
    tables = []
    for d in range(N_DR):
        t = jnp.full((GRID_W, LANES), MASK_VALUE, F32)
        base = (h * N_DR + d) * N_DC
        for m in range(N_DC):
            t = jnp.where(dc == m, rel_ref[base + m] * LOG2E, t)
        t = jnp.where(col_in, t, MASK_VALUE)
        tables.append(jnp.concatenate([t] * (TOK_BLOCK // LANES), axis=1))
    qblk = lax.broadcasted_iota(jnp.int32, (GRID_W, TOK_BLOCK), 1) // GRID_W
    for v, (boff, all_valid) in enumerate(NA_VARIANTS):
        for cb in range(ROWS_PER_BLOCK):
            tile = jnp.full((GRID_W, TOK_BLOCK), MASK_VALUE, F32)
            if all_valid is not None:
                for a in range(ROWS_PER_BLOCK):
                    dr = ROWS_PER_BLOCK * boff + cb - a
                    if all_valid or (-(NA_WIN_H // 2) <= dr < NA_WIN_H - NA_WIN_H // 2):
                        tile = jnp.where(qblk == a, tables[dr + NA_WIN_H - 1], tile)
            o_ref[v, 0, cb * GRID_W:(cb + 1) * GRID_W, :] = tile


def _na_bias(rel_bias):
    return pl.pallas_call(
        _na_bias_kernel,
        grid=(NA_HEADS,),
        in_specs=[pl.BlockSpec(memory_space=pltpu.SMEM)],
        out_specs=pl.BlockSpec((len(NA_VARIANTS), 1, TOK_BLOCK, TOK_BLOCK), lambda h: (0, h, 0, 0)),
        out_shape=jax.ShapeDtypeStruct((len(NA_VARIANTS), NA_HEADS, TOK_BLOCK, TOK_BLOCK), F32),
        name="na_bias",
    )(rel_bias.reshape(-1))


C_QLAT = 0
C_NAQ = Q_LORA
C_KVLAT = C_NAQ + NA_HEADS * NA_DIM
C_KROPE = C_KVLAT + KV_LORA
C_NAK = C_KROPE + LANES
C_NAV = C_NAK + NA_HEADS * NA_DIM
C_END = C_NAV + NA_HEADS * NA_DIM
MLA_SCALE = MLA_QK ** -0.5 * LOG2E
NA_SCALE = NA_DIM ** -0.5 * LOG2E
ONES_LANE = MLA_V


def _rope(x, c, sa, sb):
    return x * c + pltpu.roll(x, LANES - MLA_ROPE // 2, 1) * sa + pltpu.roll(x, MLA_ROPE // 2, 1) * sb


def _proj_kernel(x_ref, ctx_ref, mod_ref, g_ref, win_ref, qg_ref, kvg_ref, wuq_ref, wukv_ref,
                 c_ref, sa_ref, sb_ref,
                 q_out, k_out, v_out, nq_out, nk_out, nv_out, h_scr, proj_scr):
    j = pl.program_id(1)

    @pl.when(j == 0)
    def _():
        h_scr[...] = jnp.zeros(h_scr.shape, h_scr.dtype)
        proj_scr[...] = jnp.zeros(proj_scr.shape, proj_scr.dtype)

    proj = proj_scr
    c, sa, sb = c_ref[...], sa_ref[...], sb_ref[...]

    qn = _rms(proj[:, C_QLAT:C_NAQ], qg_ref[...]).astype(BF16)
    q = jnp.dot(qn, wuq_ref[...], preferred_element_type=F32) * MLA_SCALE
    for h in range(MLA_HEADS):
        q_out[0, h] = _rope(q[:, h * LANES:(h + 1) * LANES], c, sa, sb).astype(BF16)

    kvn = _rms(proj[:, C_KVLAT:C_KROPE], kvg_ref[...]).astype(BF16)
    kv = jnp.dot(kvn, wukv_ref[...], preferred_element_type=F32)
    kr = _rope(proj[:, C_KROPE:C_NAK], c, sa, sb)
    v0 = MLA_HEADS * LANES
    lane = lax.broadcasted_iota(jnp.int32, (TOK_BLOCK, LANES), 1)
    for h in range(MLA_HEADS):
        k_out[0, h] = (kv[:, h * LANES:(h + 1) * LANES] + kr).astype(BF16)
        v = jnp.where(lane == ONES_LANE, 1.0, kv[:, v0 + h * LANES:v0 + (h + 1) * LANES])
        v_out[0, h] = jnp.transpose(v).astype(BF16)
    for p in range(HEAD_PAIRS):
        nq_out[0, p] = (proj[:, C_NAQ + p * LANES:C_NAQ + (p + 1) * LANES] * NA_SCALE).astype(BF16)
        nk_out[0, p] = proj[:, C_NAK + p * LANES:C_NAK + (p + 1) * LANES].astype(BF16)
        nv_pair = proj[:, C_NAV + p * LANES:C_NAV + (p + 1) * LANES]
        for half in range(2):
            nv = pltpu.roll(nv_pair, NA_DIM, 1) if half else nv_pair
            nv = jnp.where(lane < NA_DIM, nv, jnp.where(lane == ONES_LANE, 1.0, 0.0))
            nv_out[0, 2 * p + half] = jnp.transpose(nv).astype(BF16)

    proj_scr[...] = jnp.dot(h_scr[...], win_ref[...], preferred_element_type=F32)
    xt = jnp.where(j == 0, ctx_ref[0], x_ref[0])
    h_scr[...] = (_rms(xt, g_ref[...]) * (1.0 + mod_ref[0, 1:2, :]) + mod_ref[0, 0:1, :]).astype(BF16)


def _proj(x, ctx, mod, g, win, qg, kvg, wuq, wukv, rc, rsa, rsb):
    B, N, D = x.shape
    nb = N // TOK_BLOCK
    ntot = N + CTX_LEN
    full = lambda shape: pl.BlockSpec(shape, lambda b, j: (0,) * len(shape))
    depth = 2
    tile_in = lambda j: jnp.minimum(j, nb)
    tile_out = lambda j: jnp.clip(j - depth, 0, nb)
    lat = lambda b, j: (b, 0, jnp.maximum(tile_out(j) - 1, 0), 0)
    alltok = lambda b, j: (b, 0, tile_out(j), 0)
    feat = lambda b, j: (b, 0, 0, tile_out(j))
    rope = lambda b, j: (tile_out(j), 0)
    return pl.pallas_call(
        _proj_kernel,
        grid=(B, nb + 1 + depth),
        in_specs=[pl.BlockSpec((1, TOK_BLOCK, D), lambda b, j: (b, jnp.maximum(tile_in(j) - 1, 0), 0)),
                  pl.BlockSpec((1, CTX_LEN, D), lambda b, j: (b, 0, 0)),
                  pl.BlockSpec((1, 6, D), lambda b, j: (jnp.where(j == 0, B, b), 0, 0)),
                  full((1, D)), full(win.shape), full((1, Q_LORA)), full((1, KV_LORA)),
                  full(wuq.shape), full(wukv.shape),
                  pl.BlockSpec((TOK_BLOCK, LANES), rope),
                  pl.BlockSpec((TOK_BLOCK, LANES), rope),
                  pl.BlockSpec((TOK_BLOCK, LANES), rope)],
        out_specs=[pl.BlockSpec((1, MLA_HEADS, TOK_BLOCK, LANES), lat),
                   pl.BlockSpec((1, MLA_HEADS, TOK_BLOCK, LANES), alltok),
                   pl.BlockSpec((1, MLA_HEADS, LANES, TOK_BLOCK), feat),
                   pl.BlockSpec((1, HEAD_PAIRS, TOK_BLOCK, LANES), lat),
                   pl.BlockSpec((1, HEAD_PAIRS, TOK_BLOCK, LANES), alltok),
                   pl.BlockSpec((1, NA_HEADS, LANES, TOK_BLOCK), feat)],
        out_shape=[jax.ShapeDtypeStruct((B, MLA_HEADS, N, LANES), BF16),
                   jax.ShapeDtypeStruct((B, MLA_HEADS, ntot, LANES), BF16),
                   jax.ShapeDtypeStruct((B, MLA_HEADS, LANES, ntot), BF16),
                   jax.ShapeDtypeStruct((B, HEAD_PAIRS, N, LANES), BF16),
                   jax.ShapeDtypeStruct((B, HEAD_PAIRS, ntot, LANES), BF16),
                   jax.ShapeDtypeStruct((B, NA_HEADS, LANES, ntot), BF16)],
        scratch_shapes=[pltpu.VMEM((TOK_BLOCK, D), BF16), pltpu.VMEM((TOK_BLOCK, C_END), F32)],
        compiler_params=pltpu.CompilerParams(vmem_limit_bytes=VMEM_LIMIT),
        name="proj",
    )(x, ctx, mod, g, win, qg, kvg, wuq, wukv, rc, rsa, rsb)


MLA_TQ = 1024
MLA_TK = 768


def _mla_kernel(q_ref, k_ref, v_ref, wg_ref, wu_ref, wd_ref, o_ref, wg_out, wu_out, wd_out,
                m_scr, alpha_scr, acc_scr, p_scr, s_scr, smax_scr):
    wg_out[...] = wg_ref[...].astype(BF16)
    wu_out[...] = wu_ref[...].astype(BF16)
    wd_out[...] = wd_ref[...].astype(BF16)

    n_chunks = k_ref.shape[2] // MLA_TK
    m_scr[...] = jnp.full(m_scr.shape, -jnp.inf, F32)
    acc_scr[...] = jnp.zeros(acc_scr.shape, F32)

    def scores(ci, hh):
        off = pl.multiple_of(ci * MLA_TK, MLA_TK)
        k = k_ref[0, hh, pl.ds(off, MLA_TK), :]
        s = lax.dot_general(k, q_ref[0, hh], (((1,), (1,)), ((), ())), preferred_element_type=F32)
        s_scr[hh] = s
        smax_scr[hh] = jnp.max(s, axis=0, keepdims=True)

    def softmax(hh):
        m_prev = m_scr[hh]
        m_next = jnp.maximum(m_prev, smax_scr[hh])
        p_scr[hh] = jnp.exp2((s_scr[hh] - m_next).astype(BF16))
        alpha_scr[hh] = jnp.exp2(m_prev - m_next)
        m_scr[hh] = m_next

    def values(ci, hh):
        off = pl.multiple_of(ci * MLA_TK, MLA_TK)
        vt = v_ref[0, hh, :, pl.ds(off, MLA_TK)]
        acc_scr[hh] = alpha_scr[hh] * acc_scr[hh] + jnp.dot(vt, p_scr[hh], preferred_element_type=F32)

    for hh in range(2):
        scores(0, hh)
    for hh in range(2):
        softmax(hh)
        scores(1, hh)

    def body(ci, carry):
        for hh in range(2):
            values(ci - 1, hh)
            softmax(hh)
            scores(ci + 1, hh)
        return carry

    lax.fori_loop(1, n_chunks - 1, body, 0)
    for hh in range(2):
        values(n_chunks - 2, hh)
        softmax(hh)
    for hh in range(2):
        values(n_chunks - 1, hh)
    outs = []
    for hh in range(2):
        acc = acc_scr[hh]
        outs.append(jnp.transpose(acc / acc[ONES_LANE:ONES_LANE + 1, :]))
    lane = lax.broadcasted_iota(jnp.int32, (MLA_TQ, LANES), 1)
    o_ref[0] = jnp.where(lane < MLA_V, outs[0], pltpu.roll(outs[1], MLA_V, 1)).astype(o_ref.dtype)


def _mla(q_m, k_m, v_m, w_gate, w_up, w_down):
    B, _, N, _ = q_m.shape
    ntot = k_m.shape[2]
    n_q = N // MLA_TQ
    steps = B * HEAD_PAIRS * n_q
    flat = [w.reshape(-1, w.shape[-1]) for w in (w_gate, w_up, w_down)]
    wblk = [(w.shape[0] // steps, w.shape[1]) for w in flat]
    step = lambda b, p, i: ((b * HEAD_PAIRS + p) * n_q + i, 0)
    outs = pl.pallas_call(
        _mla_kernel,
        grid=(B, HEAD_PAIRS, n_q),
        in_specs=[pl.BlockSpec((1, 2, MLA_TQ, LANES), lambda b, p, i: (b, p, i, 0)),
                  pl.BlockSpec((1, 2, ntot, LANES), lambda b, p, i: (b, p, 0, 0)),
                  pl.BlockSpec((1, 2, LANES, ntot), lambda b, p, i: (b, p, 0, 0))]
        + [pl.BlockSpec(blk, step) for blk in wblk],
        out_specs=[pl.BlockSpec((1, MLA_TQ, LANES), lambda b, p, i: (b, i, p))]
        + [pl.BlockSpec(blk, step) for blk in wblk],
        out_shape=[jax.ShapeDtypeStruct((B, N, MLA_HEADS * MLA_V), BF16)]
        + [jax.ShapeDtypeStruct(w.shape, BF16) for w in flat],
        scratch_shapes=[pltpu.VMEM((2, 1, MLA_TQ), F32), pltpu.VMEM((2, 1, MLA_TQ), F32),
                        pltpu.VMEM((2, LANES, MLA_TQ), F32), pltpu.VMEM((2, MLA_TK, MLA_TQ), BF16),
                        pltpu.VMEM((2, MLA_TK, MLA_TQ), F32), pltpu.VMEM((2, 1, MLA_TQ), F32)],
        compiler_params=pltpu.CompilerParams(vmem_limit_bytes=VMEM_LIMIT),
        name="mla",
    )(q_m, k_m, v_m, *flat)
    return (outs[0],) + tuple(o.reshape(w.shape) for o, w in zip(outs[1:], (w_gate, w_up, w_down)))


NA_UNIT_HEADS = 8
NA_UNITS_PER_BLOCK = NA_HEADS // NA_UNIT_HEADS
NA_KEYS = 4 * TOK_BLOCK


def _na_kernel(q_ref, kc_ref, kp_ref, k0_ref, kn_ref, bp_ref, b0_ref, bn_ref,
               vc_ref, vp_ref, v0_ref, vn_ref, o_ref, s_scr, smax_scr, p_scr):
    @pl.when(pl.program_id(1) == 0)
    def _():
        s_scr[...] = jnp.zeros(s_scr.shape, F32)
        smax_scr[...] = jnp.zeros(smax_scr.shape, F32)
        p_scr[...] = jnp.zeros(p_scr.shape, BF16)

    lane = lax.broadcasted_iota(jnp.int32, (TOK_BLOCK, LANES), 1)

    for pp in range(NA_UNIT_HEADS // 2):
        outs = []
        for half in range(2):
            hq = 2 * pp + half
            vt = jnp.concatenate([vc_ref[0, hq], vp_ref[0, hq], v0_ref[0, hq], vn_ref[0, hq]], axis=1)
            o = jnp.dot(vt, p_scr[hq], preferred_element_type=F32)
            outs.append(jnp.transpose(o / o[ONES_LANE:ONES_LANE + 1, :]))
        o_ref[0, :, pp * LANES:(pp + 1) * LANES] = jnp.where(
            lane < NA_DIM, outs[0], pltpu.roll(outs[1], NA_DIM, 1)).astype(o_ref.dtype)

    for hq in range(NA_UNIT_HEADS):
        p_scr[hq] = jnp.exp2((s_scr[hq] - smax_scr[hq]).astype(BF16))

    dn = (((1,), (1,)), ((), ()))
    for pp in range(NA_UNIT_HEADS // 2):
        keys = jnp.concatenate([kc_ref[0, pp], kp_ref[0, pp], k0_ref[0, pp], kn_ref[0, pp]], axis=0)
        for half in range(2):
            hq = 2 * pp + half
            in_half = (lane >= NA_DIM) if half else (lane < NA_DIM)
            q = jnp.where(in_half, q_ref[0, pp], jnp.zeros((), BF16))
            s = lax.dot_general(keys, q, dn, preferred_element_type=F32)
            parts = [s[:TOK_BLOCK],
                     s[TOK_BLOCK:2 * TOK_BLOCK] + bp_ref[0, hq],
                     s[2 * TOK_BLOCK:3 * TOK_BLOCK] + b0_ref[0, hq],
                     s[3 * TOK_BLOCK:] + bn_ref[0, hq]]
            for j, part in enumerate(parts):
                s_scr[hq, j * TOK_BLOCK:(j + 1) * TOK_BLOCK, :] = part
            smax_scr[hq] = jnp.max(jnp.maximum(jnp.maximum(parts[0], parts[1]),
                                               jnp.maximum(parts[2], parts[3])), axis=0, keepdims=True)


def _natten(nq, nk, nv, bias):
    B, _, N, _ = nq.shape
    nb = N // TOK_BLOCK
    last = nb - 1
    n_units = nb * NA_UNITS_PER_BLOCK
    pipeline_depth = 2

    def unit(t, lag):
        u = jnp.clip(t - lag, 0, n_units - 1)
        return u // NA_UNITS_PER_BLOCK, u % NA_UNITS_PER_BLOCK

    tok_blocks = (lambda i: 0, lambda i: jnp.maximum(i - 1, 0) + 1, lambda i: i + 1,
                  lambda i: jnp.minimum(i + 1, last) + 1)
    variants = (lambda i: jnp.where(i == 0, V_NONE, jnp.where(i == last, V_PREV_ALL, V_PREV)),
                lambda i: jnp.where((i == 0) | (i == last), V_CUR_ALL, V_CUR),
                lambda i: jnp.where(i == 0, V_NEXT_ALL, jnp.where(i == last, V_NONE, V_NEXT)))

    def q_map(b, t):
        i, g = unit(t, 0)
        return b, g, i, 0

    def key_map(tok):
        def f(b, t):
            i, g = unit(t, 0)
            return b, g, tok(i), 0
        return f

    def bias_map(var):
        def f(b, t):
            i, g = unit(t, 0)
            return var(i), g, 0, 0
        return f

    def value_map(tok):
        def f(b, t):
            i, g = unit(t, pipeline_depth)
            return b, g, 0, tok(i)
        return f

    def out_map(b, t):
        i, g = unit(t, pipeline_depth)
        return b, i, g

    pair_blk = (1, NA_UNIT_HEADS // 2, TOK_BLOCK, LANES)
    return pl.pallas_call(
        _na_kernel,
        grid=(B, n_units + pipeline_depth),
        in_specs=([pl.BlockSpec(pair_blk, q_map)]
                  + [pl.BlockSpec(pair_blk, key_map(tok)) for tok in tok_blocks]
                  + [pl.BlockSpec((1, NA_UNIT_HEADS, TOK_BLOCK, TOK_BLOCK), bias_map(var)) for var in variants]
                  + [pl.BlockSpec((1, NA_UNIT_HEADS, LANES, TOK_BLOCK), value_map(tok)) for tok in tok_blocks]),
        out_specs=pl.BlockSpec((1, TOK_BLOCK, NA_UNIT_HEADS * NA_DIM), out_map),
        out_shape=jax.ShapeDtypeStruct((B, N, NA_HEADS * NA_DIM), BF16),
        scratch_shapes=[pltpu.VMEM((NA_UNIT_HEADS, NA_KEYS, TOK_BLOCK), F32),
                        pltpu.VMEM((NA_UNIT_HEADS, 1, TOK_BLOCK), F32),
                        pltpu.VMEM((NA_UNIT_HEADS, NA_KEYS, TOK_BLOCK), BF16)],
        compiler_params=pltpu.CompilerParams(vmem_limit_bytes=VMEM_LIMIT),
        name="natten",
    )(nq, nk, nk, nk, nk, bias, bias, bias, nv, nv, nv, nv)


OUT_TM = 512
R_G1, R_SH2, R_SC2, R_G2 = 2, 3, 4, 5
GROUP_LANE = N_EXPERTS


def _first_max_lane(x, valid, lane):
    xm = jnp.where(valid, x, -jnp.inf)
    mx = jnp.max(xm, axis=1, keepdims=True)
    idx = jnp.min(jnp.where(valid & (xm == mx), lane, LANES), axis=1, keepdims=True)
    return mx, idx


def _outproj_kernel(om_ref, on_ref, x_ref, mod_ref, wo_ref, g_ref, wr_ref, br_ref,
                    x1_out, h2_out, comb_out, h2_scr):
    @pl.when(pl.program_id(0) == 0)
    def _():
        h2_scr[...] = jnp.zeros(h2_scr.shape, h2_scr.dtype)

    h2_prev = h2_scr[...]
    h2_hi = h2_prev.astype(BF16)
    h2_lo = (h2_prev - h2_hi.astype(F32)).astype(BF16)
    both = jnp.dot(h2_hi, wr_ref[...], preferred_element_type=F32)
    logits = (both[:, :LANES] + both[:, LANES:]
              + jnp.dot(h2_lo, wr_ref[:, :LANES], preferred_element_type=F32) + br_ref[...])
    lane = lax.broadcasted_iota(jnp.int32, logits.shape, 1)
    is_g = lane < N_GROUPS
    g_max, g_sel = _first_max_lane(logits, is_g, lane)
    g_w = 1.0 / jnp.sum(jnp.where(is_g, jnp.exp(logits - g_max), 0.0), axis=1, keepdims=True)
    e_lo = N_GROUPS + g_sel * EXPERTS_PER_GROUP
    in_grp = (lane >= e_lo) & (lane < e_lo + EXPERTS_PER_GROUP)
    m1, i1 = _first_max_lane(logits, in_grp, lane)
    m2, i2 = _first_max_lane(logits, in_grp & (lane != i1), lane)
    e2 = jnp.exp(m2 - m1)
    w1 = 1.0 / (1.0 + e2)
    w2 = e2 / (1.0 + e2)
    comb = jnp.where(lane == i1, g_w * w1, jnp.where(lane == i2, g_w * w2, 0.0))
    comb_out[...] = jnp.where(lane == GROUP_LANE, g_sel.astype(F32), pltpu.roll(comb, LANES - N_GROUPS, 1))

    half = wo_ref.shape[0] // 2
    mixed = (jnp.dot(om_ref[...], wo_ref[:half, :], preferred_element_type=F32)
             + jnp.dot(on_ref[...], wo_ref[half:, :], preferred_element_type=F32))
    x1 = x_ref[...] + mod_ref[0, R_G1:R_G1 + 1, :] * mixed
    x1_out[...] = x1
    h2 = _rms(x1, g_ref[...]) * (1.0 + mod_ref[0, R_SC2:R_SC2 + 1, :]) + mod_ref[0, R_SH2:R_SH2 + 1, :]
    h2_out[...] = h2.astype(BF16)
    h2_scr[...] = h2


def _outproj(o_mla, o_na, x2d, mod, wo, g, wr, br, n_per_batch):
    T, D = x2d.shape
    per_b = n_per_batch // OUT_TM
    nt = T // OUT_TM
    cur = lambda t: jnp.minimum(t, nt - 1)
    tok = lambda w: pl.BlockSpec((OUT_TM, w), lambda t: (cur(t), 0))
    full = lambda shape: pl.BlockSpec(shape, lambda t: (0,) * len(shape))
    return pl.pallas_call(
        _outproj_kernel,
        grid=(nt + 1,),
        in_specs=[tok(o_mla.shape[1]), tok(o_na.shape[1]), tok(D),
                  pl.BlockSpec((1, 6, D), lambda t: (cur(t) // per_b, 0, 0)),
                  full(wo.shape), full((1, D)), full(wr.shape), full(br.shape)],
        out_specs=[tok(D), tok(D), pl.BlockSpec((OUT_TM, LANES), lambda t: (jnp.maximum(t - 1, 0), 0))],
        out_shape=[jax.ShapeDtypeStruct((T, D), F32), jax.ShapeDtypeStruct((T, D), BF16),
                   jax.ShapeDtypeStruct((T, LANES), F32)],
        scratch_shapes=[pltpu.VMEM((OUT_TM, D), F32)],
        compiler_params=pltpu.CompilerParams(vmem_limit_bytes=VMEM_LIMIT),
        name="outproj",
    )(o_mla, o_na, x2d, mod, wo, g, wr, br)


MOE_TT = 1024
MOE_CH = 128
MOE_NBLK = MOE_TT // MOE_CH + (N_GROUPS - 1)
MOE_TTP = MOE_NBLK * MOE_CH


def _moe_sort_kernel(h_ref, comb_ref, xs_out, cs_out, pos_out, cnt_out, tri_scr):
    @pl.when(pl.program_id(0) == 0)
    def _():
        r = lax.broadcasted_iota(jnp.int32, tri_scr.shape, 0)
        c = lax.broadcasted_iota(jnp.int32, tri_scr.shape, 1)
        tri_scr[...] = jnp.where(c < r, 1.0, 0.0).astype(BF16)

    comb = comb_ref[...]
    lane = lax.broadcasted_iota(jnp.int32, comb.shape, 1)
    lane_f = lane.astype(F32)
    gid = jnp.sum(jnp.where(lane == GROUP_LANE, comb, 0.0), axis=1, keepdims=True)
    onehot = jnp.where((lane_f == gid) & (lane < N_GROUPS), 1.0, 0.0)
    ahead = jnp.dot(tri_scr[...], onehot.astype(BF16), preferred_element_type=F32)
    n = jnp.sum(onehot, axis=0, keepdims=True)
    padded = jnp.ceil(n * (1.0 / MOE_CH)) * MOE_CH
    start = jnp.zeros_like(padded)
    for k in range(1, N_GROUPS):
        start = start + jnp.where(lane[:1] >= k, pltpu.roll(padded, k, 1), 0.0)
    pos = jnp.sum(onehot * (ahead + start), axis=1, keepdims=True)
    pos_rep = jnp.broadcast_to(pos, comb.shape)
    pos_out[...] = pos_rep
    cnt_out[0] = jnp.broadcast_to(n, (SUBLANES, LANES))

    pos_row = jnp.transpose(pos_rep)[0:1, :]
    row = lax.broadcasted_iota(jnp.int32, (MOE_TTP, MOE_TT), 0).astype(F32)
    perm = jnp.where(row == pos_row, 1.0, 0.0).astype(BF16)
    xs_out[...] = jnp.dot(perm, h_ref[...], preferred_element_type=F32).astype(BF16)
    hi = comb.astype(BF16)
    lo = (comb - hi.astype(F32)).astype(BF16)
    both = jnp.dot(perm, jnp.concatenate([hi, lo], axis=1), preferred_element_type=F32)
    cs_out[...] = both[:, :LANES] + both[:, LANES:]


def _moe_sort(h2, comb):
    T, D = h2.shape
    nt = T // MOE_TT
    return pl.pallas_call(
        _moe_sort_kernel,
        grid=(nt,),
        in_specs=[pl.BlockSpec((MOE_TT, D), lambda t: (t, 0)),
                  pl.BlockSpec((MOE_TT, LANES), lambda t: (t, 0))],
        out_specs=[pl.BlockSpec((MOE_TTP, D), lambda t: (t, 0)),
                   pl.BlockSpec((MOE_TTP, LANES), lambda t: (t, 0)),
                   pl.BlockSpec((MOE_TT, LANES), lambda t: (t, 0)),
                   pl.BlockSpec((1, SUBLANES, LANES), lambda t: (t, 0, 0))],
        out_shape=[jax.ShapeDtypeStruct((nt * MOE_TTP, D), BF16),
                   jax.ShapeDtypeStruct((nt * MOE_TTP, LANES), F32),
                   jax.ShapeDtypeStruct((T, LANES), F32),
                   jax.ShapeDtypeStruct((nt, SUBLANES, LANES), F32)],
        scratch_shapes=[pltpu.VMEM((MOE_TT, MOE_TT), BF16)],
        compiler_params=pltpu.CompilerParams(vmem_limit_bytes=VMEM_LIMIT),
        name="moe_sort",
    )(h2, comb)


def _moe_experts_kernel(blk_ref, grp_ref, nvalid_ref, x_ref, c_ref, wg_ref, wu_ref, wd_ref, y_ref, hid_scr):
    s = pl.program_id(0)
    n_slots = pl.num_programs(0) - 1

    @pl.when(s == 0)
    def _():
        hid_scr[...] = jnp.zeros(hid_scr.shape, hid_scr.dtype)

    @pl.when(s > nvalid_ref[0])
    def _():
        y_ref[...] = jnp.zeros(y_ref.shape, y_ref.dtype)

    @pl.when(s <= nvalid_ref[0])
    def _():
        y = None
        for e in range(EXPERTS_PER_GROUP):
            part = jnp.dot(hid_scr[e], wd_ref[e], preferred_element_type=F32)
            y = part if y is None else y + part
        y_ref[...] = y.astype(y_ref.dtype)

        x = x_ref[...]
        comb = c_ref[...]
        lane = lax.broadcasted_iota(jnp.int32, comb.shape, 1)
        e0 = grp_ref[jnp.minimum(s, n_slots - 1)] * EXPERTS_PER_GROUP
        for e in range(EXPERTS_PER_GROUP):
            gate = jnp.dot(x, wg_ref[e], preferred_element_type=F32)
            up = jnp.dot(x, wu_ref[e], preferred_element_type=F32)
            w = jnp.sum(jnp.where(lane == e0 + e, comb, 0.0), axis=1, keepdims=True)
            hid_scr[e] = (gate * jax.nn.sigmoid(gate) * up * w).astype(BF16)


def _moe_experts(blk, grp, nvalid, xs, cs, wg, wu, wd):
    D = xs.shape[1]
    n_slots = blk.shape[0]

    def cur(arr):
        return lambda s, blk, grp, nvalid: (arr(blk, grp)[jnp.minimum(s, n_slots - 1)],)

    def prev(arr):
        return lambda s, blk, grp, nvalid: (arr(blk, grp)[jnp.maximum(s - 1, 0)],)

    blk_of = lambda blk, grp: blk
    grp_of = lambda blk, grp: grp
    rows = lambda f: (lambda *a: f(*a) + (0,))
    wts = lambda f: (lambda *a: f(*a) + (0, 0))
    return pl.pallas_call(
        _moe_experts_kernel,
        grid_spec=pltpu.PrefetchScalarGridSpec(
            num_scalar_prefetch=3,
            grid=(n_slots + 1,),
            in_specs=[pl.BlockSpec((MOE_CH, D), rows(cur(blk_of))),
                      pl.BlockSpec((MOE_CH, LANES), rows(cur(blk_of))),
                      pl.BlockSpec((EXPERTS_PER_GROUP, D, D_EXPERT), wts(cur(grp_of))),
                      pl.BlockSpec((EXPERTS_PER_GROUP, D, D_EXPERT), wts(cur(grp_of))),
                      pl.BlockSpec((EXPERTS_PER_GROUP, D_EXPERT, D), wts(prev(grp_of)))],
            out_specs=pl.BlockSpec((MOE_CH, D), rows(prev(blk_of))),
            scratch_shapes=[pltpu.VMEM((EXPERTS_PER_GROUP, MOE_CH, D_EXPERT), BF16)],
        ),
        out_shape=jax.ShapeDtypeStruct(xs.shape, BF16),
        compiler_params=pltpu.CompilerParams(vmem_limit_bytes=VMEM_LIMIT),
        name="moe_experts",
    )(blk, grp, nvalid, xs, cs, wg, wu, wd)


def _moe_final_kernel(ys_ref, pos_ref, x1_ref, mod_ref, fg_ref, o_ref):
    col = lax.broadcasted_iota(jnp.int32, (MOE_TT, MOE_TTP), 1).astype(F32)
    unperm = jnp.where(col == pltpu.repeat(pos_ref[...], MOE_NBLK, axis=1), 1.0, 0.0).astype(BF16)
    y = jnp.dot(unperm, ys_ref[...], preferred_element_type=F32)
    o_ref[...] = _rms(x1_ref[...] + mod_ref[0, R_G2:R_G2 + 1, :] * y, fg_ref[...])


def _moe_final(ys, pos, x1, mod, fg, n_per_batch):
    T, D = x1.shape
    per_b = n_per_batch // MOE_TT
    return pl.pallas_call(
        _moe_final_kernel,
        grid=(T // MOE_TT,),
        in_specs=[pl.BlockSpec((MOE_TTP, D), lambda t: (t, 0)),
                  pl.BlockSpec((MOE_TT, LANES), lambda t: (t, 0)),
                  pl.BlockSpec((MOE_TT, D), lambda t: (t, 0)),
                  pl.BlockSpec((1, 6, D), lambda t: (t // per_b, 0, 0)),
                  pl.BlockSpec((1, D), lambda t: (0, 0))],
        out_specs=pl.BlockSpec((MOE_TT, D), lambda t: (t, 0)),
        out_shape=jax.ShapeDtypeStruct((T, D), F32),
        compiler_params=pltpu.CompilerParams(vmem_limit_bytes=VMEM_LIMIT),
        name="moe_final",
    )(ys, pos, x1, mod, fg)


def _chunk_schedule(counts):
    nt = counts.shape[0]
    nchunks = jnp.ceil(counts * (1.0 / MOE_CH)).astype(jnp.int32)
    ends = jnp.cumsum(nchunks, axis=1)
    j = jnp.arange(MOE_NBLK, dtype=jnp.int32)
    grp = jnp.sum(j[None, :, None] >= ends[:, None, :], axis=2).astype(jnp.int32).reshape(-1)
    blk = (jnp.arange(nt, dtype=jnp.int32)[:, None] * MOE_NBLK + j[None, :]).reshape(-1)
    order = jnp.argsort(grp, stable=True)
    nvalid = jnp.sum(grp < N_GROUPS).astype(jnp.int32).reshape(1)
    return blk[order], jnp.minimum(grp, N_GROUPS - 1)[order], nvalid


def _rope_tables(n):
    pos = np.arange(n)
    half = MLA_ROPE // 2
    inv_freq = (ROPE_THETA ** (-np.arange(0, half, 2, dtype=np.float32) / half)).astype(np.float32)
    ang = np.concatenate([(pos // GRID_W).astype(np.float32)[:, None] * inv_freq,
                          (pos % GRID_W).astype(np.float32)[:, None] * inv_freq], axis=-1)
    cos, sin = np.cos(ang).astype(np.float32), np.sin(ang).astype(np.float32)
    one = np.ones((n, MLA_NOPE), np.float32)
    zero = np.zeros((n, MLA_NOPE), np.float32)
    pad = np.zeros((n, LANES - MLA_QK), np.float32)
    z16 = np.zeros((n, half), np.float32)
    c = np.concatenate([one, cos, cos, pad], axis=1)
    sa = np.concatenate([zero, -sin, z16, pad], axis=1)
    sb = np.concatenate([zero, z16, sin, pad], axis=1)
    ident = np.concatenate([np.ones((CTX_LEN, MLA_QK), np.float32),
                            np.zeros((CTX_LEN, LANES - MLA_QK), np.float32)], axis=1)
    zc = np.zeros((CTX_LEN, LANES), np.float32)
    return (np.concatenate([ident, c], axis=0), np.concatenate([zc, sa], axis=0),
            np.concatenate([zc, sb], axis=0))


def kernel(x, c, ctx, c_ctx, w_mod, b_mod, norm_attn_g, norm_ffn_g, w_in, q_a_norm_g, kv_a_norm_g, w_uq, w_ukv,
           na_rel_bias, w_out, w_router_group, b_router_group, w_router_expert, b_router_expert, w_gate, w_up,
           w_down, final_norm_g):
    B, N, D = x.shape
    assert w_mod.shape[0] == 1, "single-layer kernel"
    assert (B, N, D) == (2, 8192, D_MODEL) and ctx.shape == (B, CTX_LEN, D)

    cond_rows = jnp.concatenate([c, c_ctx[None, :], jnp.zeros((SUBLANES - B - 1, D), F32)], axis=0)
    mod = _adaln(cond_rows, w_mod[0], b_mod[0]).reshape(SUBLANES, 6, D)

    wi = w_in[0].astype(BF16)
    na_w = NA_HEADS * NA_DIM
    k0 = Q_LORA + na_w + KV_LORA
    win = jnp.concatenate([wi[:, :k0], jnp.zeros((D, MLA_NOPE), BF16), wi[:, k0:k0 + MLA_ROPE],
                           jnp.zeros((D, LANES - MLA_QK), BF16), wi[:, k0 + MLA_ROPE:]], axis=1)
    assert win.shape[1] == C_END
    wuq = jnp.pad(w_uq[0].reshape(Q_LORA, MLA_HEADS, MLA_QK), ((0, 0), (0, 0), (0, LANES - MLA_QK)))
    wuq = wuq.reshape(Q_LORA, MLA_HEADS * LANES).astype(BF16)
    wkv = w_ukv[0].reshape(KV_LORA, MLA_HEADS, MLA_NOPE + MLA_V)
    wk = jnp.pad(wkv[:, :, :MLA_NOPE], ((0, 0), (0, 0), (0, LANES - MLA_NOPE))).reshape(KV_LORA, MLA_HEADS * LANES)
    wv = jnp.pad(wkv[:, :, MLA_NOPE:], ((0, 0), (0, 0), (0, LANES - MLA_V))).reshape(KV_LORA, MLA_HEADS * LANES)
    wukv = jnp.concatenate([wk, wv], axis=1).astype(BF16)
    rc, rsa, rsb = _rope_tables(N)

    q_m, k_m, v_m, nq, nk, nv = _proj(x, ctx, mod, norm_attn_g, win, q_a_norm_g, kv_a_norm_g, wuq, wukv,
                                      rc, rsa, rsb)
    o_mla, wg_bf, wu_bf, wd_bf = _mla(q_m, k_m, v_m, w_gate[0], w_up[0], w_down[0])
    o_na = _natten(nq, nk, nv, _na_bias(na_rel_bias[0]))

    n_r = N_GROUPS + N_EXPERTS
    wr = jnp.pad(jnp.concatenate([w_router_group[0], w_router_expert[0]], axis=1), ((0, 0), (0, LANES - n_r)))
    wr_hi = lax.bitcast_convert_type(lax.bitcast_convert_type(wr, jnp.uint32) & jnp.uint32(0xFFFF0000), F32)
    wr = jnp.concatenate([wr_hi.astype(BF16), (wr - wr_hi).astype(BF16)], axis=1)
    br = jnp.pad(jnp.concatenate([b_router_group[0], b_router_expert[0]]), (0, LANES - n_r)).reshape(1, LANES)
    T = B * N
    x1, h2, comb = _outproj(o_mla.reshape(T, -1), o_na.reshape(T, -1), x.reshape(T, D), mod,
                            w_out[0].astype(BF16), norm_ffn_g, wr, br, N)
    xs, cs, pos, counts = _moe_sort(h2, comb)
    blk, grp, nvalid = _chunk_schedule(counts[:, 0, :N_GROUPS])
    ys = _moe_experts(blk, grp, nvalid, xs, cs, wg_bf, wu_bf, wd_bf)
    out = _moe_final(ys, pos, x1, mod, final_norm_g.reshape(1, D), N)
    return out.reshape(B, N, D)
```

```python
import jax
import jax.numpy as jnp
import numpy as np
from jax import lax
from jax.experimental import pallas as pl
from jax.experimental.pallas import tpu as pltpu

D_MODEL = 1024
GRID_W = 64
CTX_LEN = 256
MLA_HEADS = 8
MLA_NOPE = 64
MLA_ROPE = 32
MLA_QK = MLA_NOPE + MLA_ROPE
MLA_V = 64
Q_LORA = 384
KV_LORA = 256
NA_HEADS = 8
NA_DIM = 64
NA_WIN_H = 8
NA_WIN_W = 16
N_GROUPS = 4
EXPERTS_PER_GROUP = 4
N_EXPERTS = N_GROUPS * EXPERTS_PER_GROUP
D_EXPERT = 512
ROPE_THETA = 10000.0
NORM_EPS = 1e-6
MASK_VALUE = -1e30
LOG2E = 1.4426950408889634

LANES = 128
SUBLANES = 8
HEAD_PAIRS = 4
TOK_BLOCK = 256
ROWS_PER_BLOCK = TOK_BLOCK // GRID_W
VMEM_LIMIT = 48 * 1024 * 1024

F32 = jnp.float32
BF16 = jnp.bfloat16


def _rms(x, g):
    return x * lax.rsqrt(jnp.mean(x * x, axis=-1, keepdims=True) + NORM_EPS) * g


def _split_bf16(x):
    hi = x.astype(BF16)
    return hi, (x - hi.astype(F32)).astype(BF16)


def _adaln_kernel(c_ref, w_ref, b_ref, o_ref):
    c = c_ref[...]
    s_hi, s_lo = _split_bf16(c * jax.nn.sigmoid(c))
    w_hi, w_lo = _split_bf16(w_ref[...])
    s_both = jnp.concatenate([s_hi, s_lo], axis=0)
    by_hi = jnp.dot(s_both, w_hi, preferred_element_type=F32)
    by_lo = jnp.dot(s_both, w_lo, preferred_element_type=F32)
    o_ref[...] = by_hi[:SUBLANES] + by_hi[SUBLANES:] + by_lo[:SUBLANES] + b_ref[...]


def _adaln(cond_rows, w_mod, b_mod):
    n_out = w_mod.shape[1]
    bn = D_MODEL
    return pl.pallas_call(
        _adaln_kernel,
        grid=(n_out // bn,),
        in_specs=[pl.BlockSpec((SUBLANES, D_MODEL), lambda n: (0, 0)),
                  pl.BlockSpec((D_MODEL, bn), lambda n: (0, n)),
                  pl.BlockSpec((1, bn), lambda n: (0, n))],
        out_specs=pl.BlockSpec((SUBLANES, bn), lambda n: (0, n)),
        out_shape=jax.ShapeDtypeStruct((SUBLANES, n_out), F32),
        name="adaln",
    )(cond_rows, w_mod, b_mod.reshape(1, n_out))


NA_VARIANTS = ((-1, False), (0, False), (1, False), (0, True), (1, True), (-1, True), (0, None))
V_PREV, V_CUR, V_NEXT, V_CUR_ALL, V_NEXT_ALL, V_PREV_ALL, V_NONE = range(7)
N_DR = 2 * NA_WIN_H - 1
N_DC = 2 * NA_WIN_W - 1


def _na_bias_kernel(rel_ref, o_ref):
    h = pl.program_id(0)
    kc = lax.broadcasted_iota(jnp.int32, (GRID_W, LANES), 0)
    qc = lax.broadcasted_iota(jnp.int32, (GRID_W, LANES), 1) % GRID_W
    col_start = jnp.clip(qc - NA_WIN_W // 2, 0, GRID_W - NA_WIN_W)
    col_in = (kc >= col_start) & (kc < col_start + NA_WIN_W)
    dc = kc - qc + (NA_WIN_W - 1)
    tables = []
    for d in range(N_DR):
        t = jnp.full((GRID_W, LANES), MASK_VALUE, F32)
        base = (h * N_DR + d) * N_DC
        for m in range(N_DC):
            t = jnp.where(dc == m, rel_ref[base + m] * LOG2E, t)
        t = jnp.where(col_in, t, MASK_VALUE)
        tables.append(jnp.concatenate([t] * (TOK_BLOCK // LANES), axis=1))
    qblk = lax.broadcasted_iota(jnp.int32, (GRID_W, TOK_BLOCK), 1) // GRID_W
    for v, (boff, all_valid) in enumerate(NA_VARIANTS):
        for cb in range(ROWS_PER_BLOCK):
            tile = jnp.full((GRID_W, TOK_BLOCK), MASK_VALUE, F32)
            if all_valid is not None:
                for a in range(ROWS_PER_BLOCK):
                    dr = ROWS_PER_BLOCK * boff + cb - a
                    if all_valid or (-(NA_WIN_H // 2) <= dr < NA_WIN_H - NA_WIN_H // 2):
                        tile = jnp.where(qblk == a, tables[dr + NA_WIN_H - 1], tile)
            o_ref[v, 0, cb * GRID_W:(cb + 1) * GRID_W, :] = tile


def _na_bias(rel_bias):
    return pl.pallas_call(
        _na_bias_kernel,
        grid=(NA_HEADS,),
        in_specs=[pl.BlockSpec(memory_space=pltpu.SMEM)],
        out_specs=pl.BlockSpec((len(NA_VARIANTS), 1, TOK_BLOCK, TOK_BLOCK), lambda h: (0, h, 0, 0)),
        out_shape=jax.ShapeDtypeStruct((len(NA_VARIANTS), NA_HEADS, TOK_BLOCK, TOK_BLOCK), F32),
        name="na_bias",
    )(rel_bias.reshape(-1))


C_QLAT = 0
C_NAQ = Q_LORA
C_KVLAT = C_NAQ + NA_HEADS * NA_DIM
C_KROPE = C_KVLAT + KV_LORA
C_NAK = C_KROPE + LANES
C_NAV = C_NAK + NA_HEADS * NA_DIM
C_END = C_NAV + NA_HEADS * NA_DIM
MLA_SCALE = MLA_QK ** -0.5 * LOG2E
NA_SCALE = NA_DIM ** -0.5 * LOG2E
ONES_LANE = MLA_V


def _rope(x, c, sa, sb):
    return x * c + pltpu.roll(x, LANES - MLA_ROPE // 2, 1) * sa + pltpu.roll(x, MLA_ROPE // 2, 1) * sb


def _proj_kernel(x_ref, ctx_ref, mod_ref, g_ref, win_ref, qg_ref, kvg_ref, wuq_ref, wukv_ref,
                 c_ref, sa_ref, sb_ref,
                 q_out, k_out, v_out, nq_out, nk_out, nv_out, h_scr, proj_scr):
    j = pl.program_id(1)

    @pl.when(j == 0)
    def _():
        h_scr[...] = jnp.zeros(h_scr.shape, h_scr.dtype)
        proj_scr[...] = jnp.zeros(proj_scr.shape, proj_scr.dtype)

    proj = proj_scr
    c, sa, sb = c_ref[...], sa_ref[...], sb_ref[...]

    qn = _rms(proj[:, C_QLAT:C_NAQ], qg_ref[...]).astype(BF16)
    q = jnp.dot(qn, wuq_ref[...], preferred_element_type=F32) * MLA_SCALE
    for h in range(MLA_HEADS):
        q_out[0, h] = _rope(q[:, h * LANES:(h + 1) * LANES], c, sa, sb).astype(BF16)

    kvn = _rms(proj[:, C_KVLAT:C_KROPE], kvg_ref[...]).astype(BF16)
    kv = jnp.dot(kvn, wukv_ref[...], preferred_element_type=F32)
    kr = _rope(proj[:, C_KROPE:C_NAK], c, sa, sb)
    v0 = MLA_HEADS * LANES
    lane = lax.broadcasted_iota(jnp.int32, (TOK_BLOCK, LANES), 1)
    for h in range(MLA_HEADS):
        k_out[0, h] = (kv[:, h * LANES:(h + 1) * LANES] + kr).astype(BF16)
        v = jnp.where(lane == ONES_LANE, 1.0, kv[:, v0 + h * LANES:v0 + (h + 1) * LANES])
        v_out[0, h] = jnp.transpose(v).astype(BF16)
    for p in range(HEAD_PAIRS):
        nq_out[0, p] = (proj[:, C_NAQ + p * LANES:C_NAQ + (p + 1) * LANES] * NA_SCALE).astype(BF16)
        nk_out[0, p] = proj[:, C_NAK + p * LANES:C_NAK + (p + 1) * LANES].astype(BF16)
        nv_pair = proj[:, C_NAV + p * LANES:C_NAV + (p + 1) * LANES]
        for half in range(2):
            nv = pltpu.roll(nv_pair, NA_DIM, 1) if half else nv_pair
            nv = jnp.where(lane < NA_DIM, nv, jnp.where(lane == ONES_LANE, 1.0, 0.0))
            nv_out[0, 2 * p + half] = jnp.transpose(nv).astype(BF16)

    proj_scr[...] = jnp.dot(h_scr[...], win_ref[...], preferred_element_type=F32)
    xt = jnp.where(j == 0, ctx_ref[0], x_ref[0])
    h_scr[...] = (_rms(xt, g_ref[...]) * (1.0 + mod_ref[0, 1:2, :]) + mod_ref[0, 0:1, :]).astype(BF16)


def _proj(x, ctx, mod, g, win, qg, kvg, wuq, wukv, rc, rsa, rsb):
    B, N, D = x.shape
    nb = N // TOK_BLOCK
    ntot = N + CTX_LEN
    full = lambda shape: pl.BlockSpec(shape, lambda b, j: (0,) * len(shape))
    depth = 2
    tile_in = lambda j: jnp.minimum(j, nb)
    tile_out = lambda j: jnp.clip(j - depth, 0, nb)
    lat = lambda b, j: (b, 0, jnp.maximum(tile_out(j) - 1, 0), 0)
    alltok = lambda b, j: (b, 0, tile_out(j), 0)
    feat = lambda b, j: (b, 0, 0, tile_out(j))
    rope = lambda b, j: (tile_out(j), 0)
    return pl.pallas_call(
        _proj_kernel,
        grid=(B, nb + 1 + depth),
        in_specs=[pl.BlockSpec((1, TOK_BLOCK, D), lambda b, j: (b, jnp.maximum(tile_in(j) - 1, 0), 0)),
                  pl.BlockSpec((1, CTX_LEN, D), lambda b, j: (b, 0, 0)),
                  pl.BlockSpec((1, 6, D), lambda b, j: (jnp.where(j == 0, B, b), 0, 0)),
                  full((1, D)), full(win.shape), full((1, Q_LORA)), full((1, KV_LORA)),
                  full(wuq.shape), full(wukv.shape),
                  pl.BlockSpec((TOK_BLOCK, LANES), rope),
                  pl.BlockSpec((TOK_BLOCK, LANES), rope),
                  pl.BlockSpec((TOK_BLOCK, LANES), rope)],
        out_specs=[pl.BlockSpec((1, MLA_HEADS, TOK_BLOCK, LANES), lat),
                   pl.BlockSpec((1, MLA_HEADS, TOK_BLOCK, LANES), alltok),
                   pl.BlockSpec((1, MLA_HEADS, LANES, TOK_BLOCK), feat),
                   pl.BlockSpec((1, HEAD_PAIRS, TOK_BLOCK, LANES), lat),
                   pl.BlockSpec((1, HEAD_PAIRS, TOK_BLOCK, LANES), alltok),
                   pl.BlockSpec((1, NA_HEADS, LANES, TOK_BLOCK), feat)],
        out_shape=[jax.ShapeDtypeStruct((B, MLA_HEADS, N, LANES), BF16),
                   jax.ShapeDtypeStruct((B, MLA_HEADS, ntot, LANES), BF16),
                   jax.ShapeDtypeStruct((B, MLA_HEADS, LANES, ntot), BF16),
                   jax.ShapeDtypeStruct((B, HEAD_PAIRS, N, LANES), BF16),
                   jax.ShapeDtypeStruct((B, HEAD_PAIRS, ntot, LANES), BF16),
                   jax.ShapeDtypeStruct((B, NA_HEADS, LANES, ntot), BF16)],
        scratch_shapes=[pltpu.VMEM((TOK_BLOCK, D), BF16), pltpu.VMEM((TOK_BLOCK, C_END), F32)],
        compiler_params=pltpu.CompilerParams(vmem_limit_bytes=VMEM_LIMIT),
        name="proj",
    )(x, ctx, mod, g, win, qg, kvg, wuq, wukv, rc, rsa, rsb)


MLA_TQ = 1024
MLA_TK = 768
CAST_PIECES = 8


def _mla_kernel(q_ref, k_ref, v_ref, wg_ref, wu_ref, wd_ref, o_ref, wg_out, wu_out, wd_out,
                m_scr, alpha_scr, acc_scr, p_scr, s_scr, smax_scr):
    def cast_piece(j):
        for src, dst in ((wg_ref, wg_out), (wu_ref, wu_out), (wd_ref, wd_out)):
            n = src.shape[0] // CAST_PIECES
            rows = pl.ds(pl.multiple_of(j * n, n), n)
            dst[rows, :] = src[rows, :].astype(BF16)

    n_chunks = k_ref.shape[2] // MLA_TK
    assert n_chunks - 2 >= CAST_PIECES
    m_scr[...] = jnp.full(m_scr.shape, -jnp.inf, F32)
    acc_scr[...] = jnp.zeros(acc_scr.shape, F32)

    def scores(ci, hh):
        off = pl.multiple_of(ci * MLA_TK, MLA_TK)
        k = k_ref[0, hh, pl.ds(off, MLA_TK), :]
        s = lax.dot_general(k, q_ref[0, hh], (((1,), (1,)), ((), ())), preferred_element_type=F32)
        s_scr[hh] = s
        smax_scr[hh] = jnp.max(s, axis=0, keepdims=True)

    def softmax(hh):
        m_prev = m_scr[hh]
        m_next = jnp.maximum(m_prev, smax_scr[hh])
        p_scr[hh] = jnp.exp2((s_scr[hh] - m_next).astype(BF16))
        alpha_scr[hh] = jnp.exp2(m_prev - m_next)
        m_scr[hh] = m_next

    def values(ci, hh):
        off = pl.multiple_of(ci * MLA_TK, MLA_TK)
        vt = v_ref[0, hh, :, pl.ds(off, MLA_TK)]
        acc_scr[hh] = alpha_scr[hh] * acc_scr[hh] + jnp.dot(vt, p_scr[hh], preferred_element_type=F32)

    for hh in range(2):
        scores(0, hh)
    for hh in range(2):
        softmax(hh)
        scores(1, hh)

    def body(ci, carry):
        for hh in range(2):
            values(ci - 1, hh)
            softmax(hh)
            scores(ci + 1, hh)
        cast_piece(jnp.minimum(ci - 1, CAST_PIECES - 1))
        return carry

    lax.fori_loop(1, n_chunks - 1, body, 0)
    for hh in range(2):
        values(n_chunks - 2, hh)
        softmax(hh)
    for hh in range(2):
        values(n_chunks - 1, hh)
    outs = []
    for hh in range(2):
        acc = acc_scr[hh]
        outs.append(jnp.transpose(acc / acc[ONES_LANE:ONES_LANE + 1, :]))
    lane = lax.broadcasted_iota(jnp.int32, (MLA_TQ, LANES), 1)
    o_ref[0] = jnp.where(lane < MLA_V, outs[0], pltpu.roll(outs[1], MLA_V, 1)).astype(o_ref.dtype)


def _mla(q_m, k_m, v_m, w_gate, w_up, w_down):
    B, _, N, _ = q_m.shape
    ntot = k_m.shape[2]
    n_q = N // MLA_TQ
    steps = B * HEAD_PAIRS * n_q
    flat = [w.reshape(-1, w.shape[-1]) for w in (w_gate, w_up, w_down)]
    wblk = [(w.shape[0] // steps, w.shape[1]) for w in flat]
    step = lambda b, p, i: ((b * HEAD_PAIRS + p) * n_q + i, 0)
    outs = pl.pallas_call(
        _mla_kernel,
        grid=(B, HEAD_PAIRS, n_q),
        in_specs=[pl.BlockSpec((1, 2, MLA_TQ, LANES), lambda b, p, i: (b, p, i, 0)),
                  pl.BlockSpec((1, 2, ntot, LANES), lambda b, p, i: (b, p, 0, 0)),
                  pl.BlockSpec((1, 2, LANES, ntot), lambda b, p, i: (b, p, 0, 0))]
        + [pl.BlockSpec(blk, step) for blk in wblk],
        out_specs=[pl.BlockSpec((1, MLA_TQ, LANES), lambda b, p, i: (b, i, p))]
        + [pl.BlockSpec(blk, step) for blk in wblk],
        out_shape=[jax.ShapeDtypeStruct((B, N, MLA_HEADS * MLA_V), BF16)]
        + [jax.ShapeDtypeStruct(w.shape, BF16) for w in flat],
        scratch_shapes=[pltpu.VMEM((2, 1, MLA_TQ), F32), pltpu.VMEM((2, 1, MLA_TQ), F32),
                        pltpu.VMEM((2, LANES, MLA_TQ), F32), pltpu.VMEM((2, MLA_TK, MLA_TQ), BF16),
                        pltpu.VMEM((2, MLA_TK, MLA_TQ), F32), pltpu.VMEM((2, 1, MLA_TQ), F32)],
        compiler_params=pltpu.CompilerParams(vmem_limit_bytes=VMEM_LIMIT),
        name="mla",
    )(q_m, k_m, v_m, *flat)
    return (outs[0],) + tuple(o.reshape(w.shape) for o, w in zip(outs[1:], (w_gate, w_up, w_down)))


NA_UNIT_HEADS = 8
NA_UNITS_PER_BLOCK = NA_HEADS // NA_UNIT_HEADS
NA_KEYS = 4 * TOK_BLOCK


def _na_kernel(q_ref, kc_ref, kp_ref, k0_ref, kn_ref, bp_ref, b0_ref, bn_ref,
               vc_ref, vp_ref, v0_ref, vn_ref, o_ref, s_scr, smax_scr, p_scr):
    @pl.when(pl.program_id(1) == 0)
    def _():
        s_scr[...] = jnp.zeros(s_scr.shape, F32)
        smax_scr[...] = jnp.zeros(smax_scr.shape, F32)
        p_scr[...] = jnp.zeros(p_scr.shape, BF16)

    lane = lax.broadcasted_iota(jnp.int32, (TOK_BLOCK, LANES), 1)

    for pp in range(NA_UNIT_HEADS // 2):
        outs = []
        for half in range(2):
            hq = 2 * pp + half
            vt = jnp.concatenate([vc_ref[0, hq], vp_ref[0, hq], v0_ref[0, hq], vn_ref[0, hq]], axis=1)
            o = jnp.dot(vt, p_scr[hq], preferred_element_type=F32)
            outs.append(jnp.transpose(o / o[ONES_LANE:ONES_LANE + 1, :]))
        o_ref[0, :, pp * LANES:(pp + 1) * LANES] = jnp.where(
            lane < NA_DIM, outs[0], pltpu.roll(outs[1], NA_DIM, 1)).astype(o_ref.dtype)

    for hq in range(NA_UNIT_HEADS):
        p_scr[hq] = jnp.exp2((s_scr[hq] - smax_scr[hq]).astype(BF16))

    dn = (((1,), (1,)), ((), ()))
    for pp in range(NA_UNIT_HEADS // 2):
        keys = jnp.concatenate([kc_ref[0, pp], kp_ref[0, pp], k0_ref[0, pp], kn_ref[0, pp]], axis=0)
        for half in range(2):
            hq = 2 * pp + half
            in_half = (lane >= NA_DIM) if half else (lane < NA_DIM)
            q = jnp.where(in_half, q_ref[0, pp], jnp.zeros((), BF16))
            s = lax.dot_general(keys, q, dn, preferred_element_type=F32)
            parts = [s[:TOK_BLOCK],
                     s[TOK_BLOCK:2 * TOK_BLOCK] + bp_ref[0, hq],
                     s[2 * TOK_BLOCK:3 * TOK_BLOCK] + b0_ref[0, hq],
                     s[3 * TOK_BLOCK:] + bn_ref[0, hq]]
            for j, part in enumerate(parts):
                s_scr[hq, j * TOK_BLOCK:(j + 1) * TOK_BLOCK, :] = part
            smax_scr[hq] = jnp.max(jnp.maximum(jnp.maximum(parts[0], parts[1]),
                                               jnp.maximum(parts[2], parts[3])), axis=0, keepdims=True)


def _natten(nq, nk, nv, bias):
    B, _, N, _ = nq.shape
    nb = N // TOK_BLOCK
    last = nb - 1
    n_units = nb * NA_UNITS_PER_BLOCK
    pipeline_depth = 2

    def unit(t, lag):
        u = jnp.clip(t - lag, 0, n_units - 1)
        return u // NA_UNITS_PER_BLOCK, u % NA_UNITS_PER_BLOCK

    tok_blocks = (lambda i: 0, lambda i: jnp.maximum(i - 1, 0) + 1, lambda i: i + 1,
                  lambda i: jnp.minimum(i + 1, last) + 1)
    variants = (lambda i: jnp.where(i == 0, V_NONE, jnp.where(i == last, V_PREV_ALL, V_PREV)),
                lambda i: jnp.where((i == 0) | (i == last), V_CUR_ALL, V_CUR),
                lambda i: jnp.where(i == 0, V_NEXT_ALL, jnp.where(i == last, V_NONE, V_NEXT)))

    def q_map(b, t):
        i, g = unit(t, 0)
        return b, g, i, 0

    def key_map(tok):
        def f(b, t):
            i, g = unit(t, 0)
            return b, g, tok(i), 0
        return f

    def bias_map(var):
        def f(b, t):
            i, g = unit(t, 0)
            return var(i), g, 0, 0
        return f

    def value_map(tok):
        def f(b, t):
            i, g = unit(t, pipeline_depth)
            return b, g, 0, tok(i)
        return f

    def out_map(b, t):
        i, g = unit(t, pipeline_depth)
        return b, i, g

    pair_blk = (1, NA_UNIT_HEADS // 2, TOK_BLOCK, LANES)
    return pl.pallas_call(
        _na_kernel,
        grid=(B, n_units + pipeline_depth),
        in_specs=([pl.BlockSpec(pair_blk, q_map)]
                  + [pl.BlockSpec(pair_blk, key_map(tok)) for tok in tok_blocks]
                  + [pl.BlockSpec((1, NA_UNIT_HEADS, TOK_BLOCK, TOK_BLOCK), bias_map(var)) for var in variants]
                  + [pl.BlockSpec((1, NA_UNIT_HEADS, LANES, TOK_BLOCK), value_map(tok)) for tok in tok_blocks]),
        out_specs=pl.BlockSpec((1, TOK_BLOCK, NA_UNIT_HEADS * NA_DIM), out_map),
        out_shape=jax.ShapeDtypeStruct((B, N, NA_HEADS * NA_DIM), BF16),
        scratch_shapes=[pltpu.VMEM((NA_UNIT_HEADS, NA_KEYS, TOK_BLOCK), F32),
                        pltpu.VMEM((NA_UNIT_HEADS, 1, TOK_BLOCK), F32),
                        pltpu.VMEM((NA_UNIT_HEADS, NA_KEYS, TOK_BLOCK), BF16)],
        compiler_params=pltpu.CompilerParams(vmem_limit_bytes=VMEM_LIMIT),
        name="natten",
    )(nq, nk, nk, nk, nk, bias, bias, bias, nv, nv, nv, nv)


OUT_TM = 512
R_G1, R_SH2, R_SC2, R_G2 = 2, 3, 4, 5
GROUP_LANE = N_EXPERTS


def _first_max_lane(x, valid, lane):
    xm = jnp.where(valid, x, -jnp.inf)
    mx = jnp.max(xm, axis=1, keepdims=True)
    idx = jnp.min(jnp.where(valid & (xm == mx), lane, LANES), axis=1, keepdims=True)
    return mx, idx


def _outproj_kernel(om_ref, on_ref, x_ref, mod_ref, wo_ref, g_ref, wr_ref, br_ref,
                    x1_out, h2_out, comb_out, h2_scr):
    @pl.when(pl.program_id(0) == 0)
    def _():
        h2_scr[...] = jnp.zeros(h2_scr.shape, h2_scr.dtype)

    h2_prev = h2_scr[...]
    h2_hi = h2_prev.astype(BF16)
    h2_lo = (h2_prev - h2_hi.astype(F32)).astype(BF16)
    both = jnp.dot(h2_hi, wr_ref[...], preferred_element_type=F32)
    logits = (both[:, :LANES] + both[:, LANES:]
              + jnp.dot(h2_lo, wr_ref[:, :LANES], preferred_element_type=F32) + br_ref[...])
    lane = lax.broadcasted_iota(jnp.int32, logits.shape, 1)
    is_g = lane < N_GROUPS
    g_max, g_sel = _first_max_lane(logits, is_g, lane)
    g_w = 1.0 / jnp.sum(jnp.where(is_g, jnp.exp(logits - g_max), 0.0), axis=1, keepdims=True)
    e_lo = N_GROUPS + g_sel * EXPERTS_PER_GROUP
    in_grp = (lane >= e_lo) & (lane < e_lo + EXPERTS_PER_GROUP)
    m1, i1 = _first_max_lane(logits, in_grp, lane)
    m2, i2 = _first_max_lane(logits, in_grp & (lane != i1), lane)
    e2 = jnp.exp(m2 - m1)
    w1 = 1.0 / (1.0 + e2)
    w2 = e2 / (1.0 + e2)
    comb = jnp.where(lane == i1, g_w * w1, jnp.where(lane == i2, g_w * w2, 0.0))
    comb_out[...] = jnp.where(lane == GROUP_LANE, g_sel.astype(F32), pltpu.roll(comb, LANES - N_GROUPS, 1))

    half = wo_ref.shape[0] // 2
    mixed = (jnp.dot(om_ref[...], wo_ref[:half, :], preferred_element_type=F32)
             + jnp.dot(on_ref[...], wo_ref[half:, :], preferred_element_type=F32))
    x1 = x_ref[...] + mod_ref[0, R_G1:R_G1 + 1, :] * mixed
    x1_out[...] = x1
    h2 = _rms(x1, g_ref[...]) * (1.0 + mod_ref[0, R_SC2:R_SC2 + 1, :]) + mod_ref[0, R_SH2:R_SH2 + 1, :]
    h2_out[...] = h2.astype(BF16)
    h2_scr[...] = h2


def _outproj(o_mla, o_na, x2d, mod, wo, g, wr, br, n_per_batch):
    T, D = x2d.shape
    per_b = n_per_batch // OUT_TM
    nt = T // OUT_TM
    cur = lambda t: jnp.minimum(t, nt - 1)
    tok = lambda w: pl.BlockSpec((OUT_TM, w), lambda t: (cur(t), 0))
    full = lambda shape: pl.BlockSpec(shape, lambda t: (0,) * len(shape))
    return pl.pallas_call(
        _outproj_kernel,
        grid=(nt + 1,),
        in_specs=[tok(o_mla.shape[1]), tok(o_na.shape[1]), tok(D),
                  pl.BlockSpec((1, 6, D), lambda t: (cur(t) // per_b, 0, 0)),
                  full(wo.shape), full((1, D)), full(wr.shape), full(br.shape)],
        out_specs=[tok(D), tok(D), pl.BlockSpec((OUT_TM, LANES), lambda t: (jnp.maximum(t - 1, 0), 0))],
        out_shape=[jax.ShapeDtypeStruct((T, D), F32), jax.ShapeDtypeStruct((T, D), BF16),
                   jax.ShapeDtypeStruct((T, LANES), F32)],
        scratch_shapes=[pltpu.VMEM((OUT_TM, D), F32)],
        compiler_params=pltpu.CompilerParams(vmem_limit_bytes=VMEM_LIMIT),
        name="outproj",
    )(o_mla, o_na, x2d, mod, wo, g, wr, br)


MOE_TT = 1024
MOE_CH = 128
MOE_NBLK = MOE_TT // MOE_CH + (N_GROUPS - 1)
MOE_TTP = MOE_NBLK * MOE_CH


def _moe_sort_kernel(h_ref, comb_ref, xs_out, cs_out, pos_out, cnt_out, tri_scr):
    @pl.when(pl.program_id(0) == 0)
    def _():
        r = lax.broadcasted_iota(jnp.int32, tri_scr.shape, 0)
        c = lax.broadcasted_iota(jnp.int32, tri_scr.shape, 1)
        tri_scr[...] = jnp.where(c < r, 1.0, 0.0).astype(BF16)

    comb = comb_ref[...]
    lane = lax.broadcasted_iota(jnp.int32, comb.shape, 1)
    lane_f = lane.astype(F32)
    gid = jnp.sum(jnp.where(lane == GROUP_LANE, comb, 0.0), axis=1, keepdims=True)
    onehot = jnp.where((lane_f == gid) & (lane < N_GROUPS), 1.0, 0.0)
    ahead = jnp.dot(tri_scr[...], onehot.astype(BF16), preferred_element_type=F32)
    n = jnp.sum(onehot, axis=0, keepdims=True)
    padded = jnp.ceil(n * (1.0 / MOE_CH)) * MOE_CH
    start = jnp.zeros_like(padded)
    for k in range(1, N_GROUPS):
        start = start + jnp.where(lane[:1] >= k, pltpu.roll(padded, k, 1), 0.0)
    pos = jnp.sum(onehot * (ahead + start), axis=1, keepdims=True)
    pos_rep = jnp.broadcast_to(pos, comb.shape)
    pos_out[...] = pos_rep
    cnt_out[0] = jnp.broadcast_to(n, (SUBLANES, LANES))

    pos_row = jnp.transpose(pos_rep)[0:1, :]
    row = lax.broadcasted_iota(jnp.int32, (MOE_TTP, MOE_TT), 0).astype(F32)
    perm = jnp.where(row == pos_row, 1.0, 0.0).astype(BF16)
    xs_out[...] = jnp.dot(perm, h_ref[...], preferred_element_type=F32).astype(BF16)
    hi = comb.astype(BF16)
    lo = (comb - hi.astype(F32)).astype(BF16)
    both = jnp.dot(perm, jnp.concatenate([hi, lo], axis=1), preferred_element_type=F32)
    cs_out[...] = both[:, :LANES] + both[:, LANES:]


def _moe_sort(h2, comb):
    T, D = h2.shape
    nt = T // MOE_TT
    return pl.pallas_call(
        _moe_sort_kernel,
        grid=(nt,),
        in_specs=[pl.BlockSpec((MOE_TT, D), lambda t: (t, 0)),
                  pl.BlockSpec((MOE_TT, LANES), lambda t: (t, 0))],
        out_specs=[pl.BlockSpec((MOE_TTP, D), lambda t: (t, 0)),
                   pl.BlockSpec((MOE_TTP, LANES), lambda t: (t, 0)),
                   pl.BlockSpec((MOE_TT, LANES), lambda t: (t, 0)),
                   pl.BlockSpec((1, SUBLANES, LANES), lambda t: (t, 0, 0))],
        out_shape=[jax.ShapeDtypeStruct((nt * MOE_TTP, D), BF16),
                   jax.ShapeDtypeStruct((nt * MOE_TTP, LANES), F32),
                   jax.ShapeDtypeStruct((T, LANES), F32),
                   jax.ShapeDtypeStruct((nt, SUBLANES, LANES), F32)],
        scratch_shapes=[pltpu.VMEM((MOE_TT, MOE_TT), BF16)],
        compiler_params=pltpu.CompilerParams(vmem_limit_bytes=VMEM_LIMIT),
        name="moe_sort",
    )(h2, comb)


def _moe_experts_kernel(blk_ref, grp_ref, nvalid_ref, x_ref, c_ref, wg_ref, wu_ref, wd_ref, y_ref, hid_scr):
    s = pl.program_id(0)
    n_slots = pl.num_programs(0) - 1

    @pl.when(s == 0)
    def _():
        hid_scr[...] = jnp.zeros(hid_scr.shape, hid_scr.dtype)

    @pl.when(s > nvalid_ref[0])
    def _():
        y_ref[...] = jnp.zeros(y_ref.shape, y_ref.dtype)

    @pl.when(s <= nvalid_ref[0])
    def _():
        y = None
        for e in range(EXPERTS_PER_GROUP):
            part = jnp.dot(hid_scr[e], wd_ref[e], preferred_element_type=F32)
            y = part if y is None else y + part
        y_ref[...] = y.astype(y_ref.dtype)

        x = x_ref[...]
        comb = c_ref[...]
        lane = lax.broadcasted_iota(jnp.int32, comb.shape, 1)
        e0 = grp_ref[jnp.minimum(s, n_slots - 1)] * EXPERTS_PER_GROUP
        for e in range(EXPERTS_PER_GROUP):
            gate = jnp.dot(x, wg_ref[e], preferred_element_type=F32)
            up = jnp.dot(x, wu_ref[e], preferred_element_type=F32)
            w = jnp.sum(jnp.where(lane == e0 + e, comb, 0.0), axis=1, keepdims=True)
            hid_scr[e] = (gate * jax.nn.sigmoid(gate) * up * w).astype(BF16)


def _moe_experts(blk, grp, nvalid, xs, cs, wg, wu, wd):
    D = xs.shape[1]
    n_slots = blk.shape[0]

    def cur(arr):
        return lambda s, blk, grp, nvalid: (arr(blk, grp)[jnp.minimum(s, n_slots - 1)],)

    def prev(arr):
        return lambda s, blk, grp, nvalid: (arr(blk, grp)[jnp.maximum(s - 1, 0)],)

    blk_of = lambda blk, grp: blk
    grp_of = lambda blk, grp: grp
    rows = lambda f: (lambda *a: f(*a) + (0,))
    wts = lambda f: (lambda *a: f(*a) + (0, 0))
    return pl.pallas_call(
        _moe_experts_kernel,
        grid_spec=pltpu.PrefetchScalarGridSpec(
            num_scalar_prefetch=3,
            grid=(n_slots + 1,),
            in_specs=[pl.BlockSpec((MOE_CH, D), rows(cur(blk_of))),
                      pl.BlockSpec((MOE_CH, LANES), rows(cur(blk_of))),
                      pl.BlockSpec((EXPERTS_PER_GROUP, D, D_EXPERT), wts(cur(grp_of))),
                      pl.BlockSpec((EXPERTS_PER_GROUP, D, D_EXPERT), wts(cur(grp_of))),
                      pl.BlockSpec((EXPERTS_PER_GROUP, D_EXPERT, D), wts(prev(grp_of)))],
            out_specs=pl.BlockSpec((MOE_CH, D), rows(prev(blk_of))),
            scratch_shapes=[pltpu.VMEM((EXPERTS_PER_GROUP, MOE_CH, D_EXPERT), BF16)],
        ),
        out_shape=jax.ShapeDtypeStruct(xs.shape, BF16),
        compiler_params=pltpu.CompilerParams(vmem_limit_bytes=VMEM_LIMIT),
        name="moe_experts",
    )(blk, grp, nvalid, xs, cs, wg, wu, wd)


def _moe_final_kernel(ys_ref, pos_ref, x1_ref, mod_ref, fg_ref, o_ref):
    col = lax.broadcasted_iota(jnp.int32, (MOE_TT, MOE_TTP), 1).astype(F32)
    unperm = jnp.where(col == pltpu.repeat(pos_ref[...], MOE_NBLK, axis=1), 1.0, 0.0).astype(BF16)
    y = jnp.dot(unperm, ys_ref[...], preferred_element_type=F32)
    o_ref[...] = _rms(x1_ref[...] + mod_ref[0, R_G2:R_G2 + 1, :] * y, fg_ref[...])


def _moe_final(ys, pos, x1, mod, fg, n_per_batch):
    T, D = x1.shape
    per_b = n_per_batch // MOE_TT
    return pl.pallas_call(
        _moe_final_kernel,
        grid=(T // MOE_TT,),
        in_specs=[pl.BlockSpec((MOE_TTP, D), lambda t: (t, 0)),
                  pl.BlockSpec((MOE_TT, LANES), lambda t: (t, 0)),
                  pl.BlockSpec((MOE_TT, D), lambda t: (t, 0)),
                  pl.BlockSpec((1, 6, D), lambda t: (t // per_b, 0, 0)),
                  pl.BlockSpec((1, D), lambda t: (0, 0))],
        out_specs=pl.BlockSpec((MOE_TT, D), lambda t: (t, 0)),
        out_shape=jax.ShapeDtypeStruct((T, D), F32),
        compiler_params=pltpu.CompilerParams(vmem_limit_bytes=VMEM_LIMIT),
        name="moe_final",
    )(ys, pos, x1, mod, fg)


def _chunk_schedule(counts):
    nt = counts.shape[0]
    nchunks = jnp.ceil(counts * (1.0 / MOE_CH)).astype(jnp.int32)
    ends = jnp.cumsum(nchunks, axis=1)
    j = jnp.arange(MOE_NBLK, dtype=jnp.int32)
    grp = jnp.sum(j[None, :, None] >= ends[:, None, :], axis=2).astype(jnp.int32).reshape(-1)
    blk = (jnp.arange(nt, dtype=jnp.int32)[:, None] * MOE_NBLK + j[None, :]).reshape(-1)
    order = jnp.argsort(grp, stable=True)
    nvalid = jnp.sum(grp < N_GROUPS).astype(jnp.int32).reshape(1)
    return blk[order], jnp.minimum(grp, N_GROUPS - 1)[order], nvalid


def _rope_tables(n):
    pos = np.arange(n)
    half = MLA_ROPE // 2
    inv_freq = (ROPE_THETA ** (-np.arange(0, half, 2, dtype=np.float32) / half)).astype(np.float32)
    ang = np.concatenate([(pos // GRID_W).astype(np.float32)[:, None] * inv_freq,
                          (pos % GRID_W).astype(np.float32)[:, None] * inv_freq], axis=-1)
    cos, sin = np.cos(ang).astype(np.float32), np.sin(ang).astype(np.float32)
    one = np.ones((n, MLA_NOPE), np.float32)
    zero = np.zeros((n, MLA_NOPE), np.float32)
    pad = np.zeros((n, LANES - MLA_QK), np.float32)
    z16 = np.zeros((n, half), np.float32)
    c = np.concatenate([one, cos, cos, pad], axis=1)
    sa = np.concatenate([zero, -sin, z16, pad], axis=1)
    sb = np.concatenate([zero, z16, sin, pad], axis=1)
    ident = np.concatenate([np.ones((CTX_LEN, MLA_QK), np.float32),
                            np.zeros((CTX_LEN, LANES - MLA_QK), np.float32)], axis=1)
    zc = np.zeros((CTX_LEN, LANES), np.float32)
    return (np.concatenate([ident, c], axis=0), np.concatenate([zc, sa], axis=0),
            np.concatenate([zc, sb], axis=0))


def kernel(x, c, ctx, c_ctx, w_mod, b_mod, norm_attn_g, norm_ffn_g, w_in, q_a_norm_g, kv_a_norm_g, w_uq, w_ukv,
           na_rel_bias, w_out, w_router_group, b_router_group, w_router_expert, b_router_expert, w_gate, w_up,
           w_down, final_norm_g):
    B, N, D = x.shape
    assert w_mod.shape[0] == 1, "single-layer kernel"
    assert (B, N, D) == (2, 8192, D_MODEL) and ctx.shape == (B, CTX_LEN, D)

    cond_rows = jnp.concatenate([c, c_ctx[None, :], jnp.zeros((SUBLANES - B - 1, D), F32)], axis=0)
    mod = _adaln(cond_rows, w_mod[0], b_mod[0]).reshape(SUBLANES, 6, D)

    wi = w_in[0].astype(BF16)
    na_w = NA_HEADS * NA_DIM
    k0 = Q_LORA + na_w + KV_LORA
    win = jnp.concatenate([wi[:, :k0], jnp.zeros((D, MLA_NOPE), BF16), wi[:, k0:k0 + MLA_ROPE],
                           jnp.zeros((D, LANES - MLA_QK), BF16), wi[:, k0 + MLA_ROPE:]], axis=1)
    assert win.shape[1] == C_END
    wuq = jnp.pad(w_uq[0].reshape(Q_LORA, MLA_HEADS, MLA_QK), ((0, 0), (0, 0), (0, LANES - MLA_QK)))
    wuq = wuq.reshape(Q_LORA, MLA_HEADS * LANES).astype(BF16)
    wkv = w_ukv[0].reshape(KV_LORA, MLA_HEADS, MLA_NOPE + MLA_V)
    wk = jnp.pad(wkv[:, :, :MLA_NOPE], ((0, 0), (0, 0), (0, LANES - MLA_NOPE))).reshape(KV_LORA, MLA_HEADS * LANES)
    wv = jnp.pad(wkv[:, :, MLA_NOPE:], ((0, 0), (0, 0), (0, LANES - MLA_V))).reshape(KV_LORA, MLA_HEADS * LANES)
    wukv = jnp.concatenate([wk, wv], axis=1).astype(BF16)
    rc, rsa, rsb = _rope_tables(N)

    q_m, k_m, v_m, nq, nk, nv = _proj(x, ctx, mod, norm_attn_g, win, q_a_norm_g, kv_a_norm_g, wuq, wukv,
                                      rc, rsa, rsb)
    o_mla, wg_bf, wu_bf, wd_bf = _mla(q_m, k_m, v_m, w_gate[0], w_up[0], w_down[0])
    o_na = _natten(nq, nk, nv, _na_bias(na_rel_bias[0]))

    n_r = N_GROUPS + N_EXPERTS
    wr = jnp.pad(jnp.concatenate([w_router_group[0], w_router_expert[0]], axis=1), ((0, 0), (0, LANES - n_r)))
    wr_hi = lax.bitcast_convert_type(lax.bitcast_convert_type(wr, jnp.uint32) & jnp.uint32(0xFFFF0000), F32)
    wr = jnp.concatenate([wr_hi.astype(BF16), (wr - wr_hi).astype(BF16)], axis=1)
    br = jnp.pad(jnp.concatenate([b_router_group[0], b_router_expert[0]]), (0, LANES - n_r)).reshape(1, LANES)
    T = B * N
    x1, h2, comb = _outproj(o_mla.reshape(T, -1), o_na.reshape(T, -1), x.reshape(T, D), mod,
                            w_out[0].astype(BF16), norm_ffn_g, wr, br, N)
    xs, cs, pos, counts = _moe_sort(h2, comb)
    blk, grp, nvalid = _chunk_schedule(counts[:, 0, :N_GROUPS])
    ys = _moe_experts(blk, grp, nvalid, xs, cs, wg_bf, wu_bf, wd_bf)
    out = _moe_final(ys, pos, x1, mod, final_norm_g.reshape(1, D), N)
    return out.reshape(B, N, D)
```
